```python
import math
import jax, jax.numpy as jnp
from jax import lax
import numpy as np

D_MODEL = 2048
BATCH = 4
SEQ = 4096
DEPTH = 2

GRID_W = 64
CTX_LEN = 256
N_AB = (DEPTH + 1) // 2
N_CD = DEPTH // 2
N_MOD = 9
D_FF = 5632
EPS = 1e-6
ROPE_THETA = 10000.0
Q_BLOCK = 128

A_HEADS = 8
A_KV = 2
A_DH = 128
B_HEADS = 4
B_DQK = 128
B_DV = 256
B_CHUNK = 64
C_WIDTH = 1024
C_GROUP = 16
C_GROUPS = C_WIDTH // C_GROUP
C_STATE = 64
D_HEADS = 16
D_KV = 2
D_DH = 64
D_WINDOW = 128

AB_SIZES = (A_HEADS * A_DH, A_KV * A_DH, A_KV * A_DH, B_HEADS * B_DQK, B_HEADS * B_DQK,
            B_HEADS * B_DV, B_HEADS * B_DV, 4 * B_HEADS)
AB_IN = sum(AB_SIZES)
AB_MIX = A_HEADS * A_DH + B_HEADS * B_DV
CD_SIZES = (C_WIDTH, D_HEADS * D_DH, D_KV * D_DH, D_KV * D_DH)
CD_IN = sum(CD_SIZES)
CD_MIX = C_WIDTH + D_HEADS * D_DH
F32 = jnp.float32

kernel_name = 'hybrid_dit_gqa_mlstm_s5_swa'


def split_cols(z, sizes):
    out, start = [], 0
    for s in sizes:
        out.append(z[..., start:start + s])
        start += s
    return out


def rmsnorm(x, g):
    xf = x.astype(F32)
    y = xf * lax.rsqrt(jnp.mean(xf * xf, axis=-1, keepdims=True) + EPS)
    return (y * g.astype(F32)).astype(x.dtype)


def modulate(x, g, shift, scale):
    return rmsnorm(x, g) * (1 + scale) + shift


def swiglu(h, w1, w3, w2):
    return (jax.nn.silu(h @ w1) * (h @ w3)) @ w2


def to_heads(z, n, dh):
    b, t, _ = z.shape
    return z.reshape(b, t, n, dh).transpose(0, 2, 1, 3)


def from_heads(z):
    b, n, t, dh = z.shape
    return z.transpose(0, 2, 1, 3).reshape(b, t, n * dh)


def axial_rope(rows, head_dim):
    r, cidx = jnp.meshgrid(jnp.arange(rows, dtype=F32), jnp.arange(GRID_W, dtype=F32), indexing='ij')
    r, cidx = r.reshape(-1), cidx.reshape(-1)
    n_freq = head_dim // 4
    inv = ROPE_THETA ** (-jnp.arange(n_freq, dtype=F32) / n_freq)
    ang = jnp.concatenate([r[:, None] * inv, cidx[:, None] * inv], axis=-1)
    return jnp.cos(ang), jnp.sin(ang)


def apply_rope(x, cos, sin):
    half = x.shape[-1] // 2
    xf = x.astype(F32)
    x1, x2 = xf[..., :half], xf[..., half:]
    return jnp.concatenate([x1 * cos - x2 * sin, x2 * cos + x1 * sin], axis=-1).astype(x.dtype)


def attend(q, k, v, scale):
    s = jnp.einsum('bkgqd,bksd->bkgqs', q, k).astype(F32) * scale
    p = jax.nn.softmax(s, axis=-1).astype(v.dtype)
    return jnp.einsum('bkgqs,bksd->bkgqd', p, v)


def attn_a(zq, zk, zv, cq, ck, cv, g_q, g_k, cos, sin, want_ctx):
    b, t, _ = zq.shape
    grp = A_HEADS // A_KV
    scale = A_DH ** -0.5
    q = apply_rope(rmsnorm(to_heads(zq, A_HEADS, A_DH), g_q), cos, sin)
    k = apply_rope(rmsnorm(to_heads(zk, A_KV, A_DH), g_k), cos, sin)
    v = to_heads(zv, A_KV, A_DH)
    kc = rmsnorm(to_heads(ck, A_KV, A_DH), g_k)
    vc = to_heads(cv, A_KV, A_DH)
    k_all = jnp.concatenate([kc, k], axis=2)
    v_all = jnp.concatenate([vc, v], axis=2)
    nb = t // Q_BLOCK
    qb = jnp.moveaxis(q.reshape(b, A_KV, grp, nb, Q_BLOCK, A_DH), 3, 0)
    o = lax.map(lambda qq: attend(qq, k_all, v_all, scale), qb)
    y = from_heads(jnp.moveaxis(o, 0, 3).reshape(b, A_HEADS, t, A_DH))
    yc = None
    if want_ctx:
        tc = cq.shape[1]
        qc = rmsnorm(to_heads(cq, A_HEADS, A_DH), g_q).reshape(b, A_KV, grp, tc, A_DH)
        yc = from_heads(attend(qc, kc, vc, scale).reshape(b, A_HEADS, tc, A_DH))
    return y, yc


def mlstm_chunkwise(q, k, v, li, lf, state):
    b, h, t, _ = q.shape
    L = B_CHUNK
    nc = t // L

    def chunks(a):
        return jnp.moveaxis(a.reshape(a.shape[:2] + (nc, L) + a.shape[3:]), 2, 0)

    causal = jnp.tril(jnp.ones((L, L), dtype=bool))

    def step(carry, xs):
        C, n, m = carry
        qc, kc, vc, ic, fc = xs
        bcum = jnp.cumsum(fc, axis=-1)
        log_d = jnp.where(causal, bcum[..., :, None] - bcum[..., None, :] + ic[..., None, :], -jnp.inf)
        m_inter = bcum + m[..., None]
        m_t = jnp.maximum(m_inter, jnp.max(log_d, axis=-1))
        w_intra = jnp.exp(log_d - m_t[..., None])
        w_inter = jnp.exp(m_inter - m_t)
        s = jnp.einsum('bhtd,bhsd->bhts', qc, kc) * w_intra
        num = jnp.einsum('bhts,bhsv->bhtv', s, vc) + w_inter[..., None] * jnp.einsum('bhvd,bhtd->bhtv', C, qc)
        den = jnp.sum(s, axis=-1) + w_inter * jnp.einsum('bhd,bhtd->bht', n, qc)
        h_out = num / jnp.maximum(jnp.abs(den), jnp.exp(-m_t))[..., None]
        b_last = bcum[..., -1]
        log_w = b_last[..., None] - bcum + ic
        m_new = jnp.maximum(b_last + m, jnp.max(log_w, axis=-1))
        w = jnp.exp(log_w - m_new[..., None])
        decay = jnp.exp(b_last + m - m_new)
        C_new = decay[..., None, None] * C + jnp.einsum('bhsv,bhsd->bhvd', w[..., None] * vc, kc)
        n_new = decay[..., None] * n + jnp.einsum('bhs,bhsd->bhd', w, kc)
        return (C_new, n_new, m_new), h_out

    state, hs = lax.scan(step, state, tuple(chunks(a) for a in (q, k, v, li, lf)))
    return jnp.moveaxis(hs, 0, 2).reshape(b, h, t, v.shape[-1]), state


def mlstm_mixer(zq, zk, zv, zo, zg, cq, ck, cv, co, cg, b_gate, head_g, want_ctx):
    def prep(q, k, v, g):
        bz, tz, _ = q.shape
        qh = to_heads(q.astype(F32), B_HEADS, B_DQK) * (B_DQK ** -0.5)
        kh = to_heads(k.astype(F32), B_HEADS, B_DQK)
        vh = to_heads(v.astype(F32), B_HEADS, B_DV)
        gg = (g.astype(F32) + b_gate.astype(F32)).reshape(bz, tz, 4, B_HEADS).transpose(2, 0, 3, 1)
        return qh, kh, vh, (gg[0], jax.nn.log_sigmoid(gg[1])), (gg[2], jax.nn.log_sigmoid(gg[3]))

    def flip(a):
        return jnp.flip(a, axis=2)

    def bidir(q, k, v, gf, gb, st_f, st_b):
        hf, sf = mlstm_chunkwise(q, k, v, gf[0], gf[1], st_f)
        hb, sb = mlstm_chunkwise(flip(q), flip(k), flip(v), flip(gb[0]), flip(gb[1]), st_b)
        return hf + flip(hb), sf, sb

    def finish(hh, o):
        hn = hh * lax.rsqrt(jnp.mean(hh * hh, axis=-1, keepdims=True) + EPS)
        return (jax.nn.sigmoid(o.astype(F32)) * from_heads(hn) * head_g.astype(F32)).astype(o.dtype)

    bz = zq.shape[0]
    zero = (jnp.zeros((bz, B_HEADS, B_DV, B_DQK), F32), jnp.zeros((bz, B_HEADS, B_DQK), F32),
            jnp.zeros((bz, B_HEADS), F32))
    hc, sf, sb = bidir(*prep(cq, ck, cv, cg), zero, zero)
    h, _, _ = bidir(*prep(zq, zk, zv, zg), sf, sb)
    yc = finish(hc, co) if want_ctx else None
    return finish(h, zo), yc


def s5_combine(e1, e2):
    ar1, ai1, br1, bi1 = e1
    ar2, ai2, br2, bi2 = e2
    return (ar1 * ar2 - ai1 * ai2, ar1 * ai2 + ai1 * ar2,
            ar2 * br1 - ai2 * bi1 + br2, ar2 * bi1 + ai2 * br1 + bi2)


def s5_direction(u, a_re, a_im, log_dt, b_re, b_im, c_re, c_im, x0_re, x0_im):
    dt = jnp.exp(log_dt)[:, None]
    mag = jnp.exp(a_re * dt)
    ab_re, ab_im = mag * jnp.cos(a_im * dt), mag * jnp.sin(a_im * dt)
    den = a_re * a_re + a_im * a_im
    nr, ni = ab_re - 1.0, ab_im
    k_re, k_im = (nr * a_re + ni * a_im) / den, (ni * a_re - nr * a_im) / den
    bb_re = k_re[..., None] * b_re - k_im[..., None] * b_im
    bb_im = k_re[..., None] * b_im + k_im[..., None] * b_re
    bu_re = jnp.einsum('tbgi,gpi->tbgp', u, bb_re)
    bu_im = jnp.einsum('tbgi,gpi->tbgp', u, bb_im)
    bu_re = bu_re.at[0].add(ab_re * x0_re - ab_im * x0_im)
    bu_im = bu_im.at[0].add(ab_re * x0_im + ab_im * x0_re)
    t = u.shape[0]
    a_r = jnp.broadcast_to(ab_re[None, None], (t, 1) + ab_re.shape)
    a_i = jnp.broadcast_to(ab_im[None, None], (t, 1) + ab_im.shape)
    _, _, xr, xi = lax.associative_scan(s5_combine, (a_r, a_i, bu_re, bu_im), axis=0)
    y = jnp.einsum('tbgp,gip->tbgi', xr, c_re) - jnp.einsum('tbgp,gip->tbgi', xi, c_im)
    return y, xr[-1], xi[-1]


def s5_mixer(u, uc, a_re, a_im, log_dt, b_re, b_im, c_re, c_im, d_skip, glu_w, glu_b, want_ctx):
    def to_tbgi(z):
        bz, tz, _ = z.shape
        return z.astype(F32).reshape(bz, tz, C_GROUPS, C_GROUP).transpose(1, 0, 2, 3)

    def prm(d):
        return (a_re[d].astype(F32), a_im[d].astype(F32), log_dt[d].astype(F32), b_re[d].astype(F32),
                b_im[d].astype(F32), c_re[d].astype(F32), c_im[d].astype(F32))

    def flip(a):
        return jnp.flip(a, axis=0)

    def finish(y_f, y_b_rev, z):
        y = (y_f + flip(y_b_rev)).transpose(1, 0, 2, 3).reshape(z.shape) + d_skip.astype(F32) * z.astype(F32)
        g = jax.nn.gelu(y)
        return (g * jax.nn.sigmoid(g @ glu_w.astype(F32) + glu_b.astype(F32))).astype(z.dtype)

    ut, uct = to_tbgi(u), to_tbgi(uc)
    zero = jnp.zeros((u.shape[0], C_GROUPS, C_STATE), F32)
    ycf, sfr, sfi = s5_direction(uct, *prm(0), zero, zero)
    ycb, sbr, sbi = s5_direction(flip(uct), *prm(1), zero, zero)
    yf, _, _ = s5_direction(ut, *prm(0), sfr, sfi)
    yb, _, _ = s5_direction(flip(ut), *prm(1), sbr, sbi)
    yc = finish(ycf, ycb, uc) if want_ctx else None
    return finish(yf, yb, u), yc


def attn_d(zq, zk, zv, cq, ck, cv, sink, cos, sin, want_ctx):
    b, t, _ = zq.shape
    grp = D_HEADS // D_KV
    scale = D_DH ** -0.5
    q = apply_rope(to_heads(zq, D_HEADS, D_DH), cos, sin).reshape(b, D_KV, grp, t, D_DH)
    k = apply_rope(to_heads(zk, D_KV, D_DH), cos, sin)
    v = to_heads(zv, D_KV, D_DH)
    kc, vc = to_heads(ck, D_KV, D_DH), to_heads(cv, D_KV, D_DH)
    n_ctx = kc.shape[2]
    sink_l = sink.astype(F32).reshape(1, D_KV, grp, 1, 1)
    span = Q_BLOCK + 2 * D_WINDOW
    pad = ((0, 0), (0, 0), (D_WINDOW, D_WINDOW), (0, 0))
    kp, vp = jnp.pad(k, pad), jnp.pad(v, pad)

    def block(bi):
        start = bi * Q_BLOCK
        qb = lax.dynamic_slice_in_dim(q, start, Q_BLOCK, axis=3)
        kb = lax.dynamic_slice_in_dim(kp, start, span, axis=2)
        vb = lax.dynamic_slice_in_dim(vp, start, span, axis=2)
        qpos = start + jnp.arange(Q_BLOCK)
        kpos = start - D_WINDOW + jnp.arange(span)
        ok = (kpos[None, :] >= 0) & (kpos[None, :] < t) & (jnp.abs(qpos[:, None] - kpos[None, :]) <= D_WINDOW)
        s_loc = jnp.where(ok, jnp.einsum('bkgqd,bksd->bkgqs', qb, kb).astype(F32) * scale, -jnp.inf)
        s_ctx = jnp.einsum('bkgqd,bksd->bkgqs', qb, kc).astype(F32) * scale
        s = jnp.concatenate([jnp.broadcast_to(sink_l, s_ctx.shape[:-1] + (1,)), s_ctx, s_loc], axis=-1)
        p = jax.nn.softmax(s, axis=-1).astype(v.dtype)
        return (jnp.einsum('bkgqs,bksd->bkgqd', p[..., 1:1 + n_ctx], vc)
                + jnp.einsum('bkgqs,bksd->bkgqd', p[..., 1 + n_ctx:], vb))

    o = lax.map(block, jnp.arange(t // Q_BLOCK))
    y = from_heads(jnp.moveaxis(o, 0, 3).reshape(b, D_HEADS, t, D_DH))
    yc = None
    if want_ctx:
        tc = cq.shape[1]
        qc = to_heads(cq, D_HEADS, D_DH).reshape(b, D_KV, grp, tc, D_DH)
        s = jnp.einsum('bkgqd,bksd->bkgqs', qc, kc).astype(F32) * scale
        s = jnp.concatenate([jnp.broadcast_to(sink_l, s.shape[:-1] + (1,)), s], axis=-1)
        p = jax.nn.softmax(s, axis=-1).astype(vc.dtype)
        oc = jnp.einsum('bkgqs,bksd->bkgqd', p[..., 1:], vc)
        yc = from_heads(oc.reshape(b, D_HEADS, tc, D_DH))
    return y, yc


def ab_mixer(hn, hcn, w_in, b_gate, w_out, g_q, g_k, head_g, cos, sin, want_ctx):
    qa, ka, va, qb, kb, vb, ob, gb = split_cols(hn @ w_in, AB_SIZES)
    cqa, cka, cva, cqb, ckb, cvb, cob, cgb = split_cols(hcn @ w_in, AB_SIZES)
    ya, yac = attn_a(qa, ka, va, cqa, cka, cva, g_q, g_k, cos, sin, want_ctx)
    yb, ybc = mlstm_mixer(qb, kb, vb, ob, gb, cqb, ckb, cvb, cob, cgb, b_gate, head_g, want_ctx)
    y = jnp.concatenate([ya, yb], axis=-1) @ w_out
    yc = jnp.concatenate([yac, ybc], axis=-1) @ w_out if want_ctx else None
    return y, yc


def cd_mixer(hn, hcn, w_in, w_out, a_re, a_im, log_dt, b_re, b_im, c_re, c_im, d_skip, glu_w, glu_b,
             sink, cos, sin, want_ctx):
    u, qd, kd, vd = split_cols(hn @ w_in, CD_SIZES)
    uc, cqd, ckd, cvd = split_cols(hcn @ w_in, CD_SIZES)
    ys, ysc = s5_mixer(u, uc, a_re, a_im, log_dt, b_re, b_im, c_re, c_im, d_skip, glu_w, glu_b, want_ctx)
    yd, ydc = attn_d(qd, kd, vd, cqd, ckd, cvd, sink, cos, sin, want_ctx)
    y = jnp.concatenate([ys, yd], axis=-1) @ w_out
    yc = jnp.concatenate([ysc, ydc], axis=-1) @ w_out if want_ctx else None
    return y, yc


def setup_inputs(seed: int = 0) -> dict:
    key = jax.random.key(seed)
    ks = iter(jax.random.split(key, 40))

    def nrm(shape, scale):
        return jax.random.normal(next(ks), shape, F32) * scale

    D = D_MODEL
    ig = nrm((N_AB, 2, B_HEADS), 0.1)
    fg = jnp.linspace(3.0, 6.0, B_HEADS, dtype=F32)[None, None, :] + nrm((N_AB, 2, B_HEADS), 0.1)
    ab_b_gate = jnp.stack([ig[:, 0], fg[:, 0], ig[:, 1], fg[:, 1]], axis=1).reshape(N_AB, 4 * B_HEADS)
    n_idx = jnp.arange(C_STATE, dtype=F32)
    return {
        'x': nrm((BATCH, SEQ, D), 1.0),
        'c': nrm((BATCH, D), 1.0),
        'ctx': nrm((BATCH, CTX_LEN, D), 1.0),
        'c_ctx': nrm((D,), 1.0),
        'mod_w': nrm((DEPTH, D, N_MOD * D), 0.5 * D ** -0.5),
        'mod_b': nrm((DEPTH, N_MOD * D), 0.02),
        'norm_g': 1.0 + nrm((DEPTH, 3, D), 0.02),
        'ffn_w1': nrm((DEPTH, 2, D, D_FF), D ** -0.5),
        'ffn_w3': nrm((DEPTH, 2, D, D_FF), D ** -0.5),
        'ffn_w2': nrm((DEPTH, 2, D_FF, D), D_FF ** -0.5),
        'ab_w_in': nrm((N_AB, D, AB_IN), D ** -0.5),
        'ab_b_gate': ab_b_gate,
        'ab_w_out': nrm((N_AB, AB_MIX, D), AB_MIX ** -0.5),
        'a_gq': 1.0 + nrm((N_AB, A_DH), 0.02),
        'a_gk': 1.0 + nrm((N_AB, A_DH), 0.02),
        'b_head_g': 1.0 + nrm((N_AB, B_HEADS * B_DV), 0.02),
        'cd_w_in': nrm((N_CD, D, CD_IN), D ** -0.5),
        'cd_w_out': nrm((N_CD, CD_MIX, D), CD_MIX ** -0.5),
        's5_a_re': -0.5 * jnp.exp(nrm((N_CD, 2, C_GROUPS, C_STATE), 0.05)),
        's5_a_im': math.pi * n_idx + nrm((N_CD, 2, C_GROUPS, C_STATE), 0.01),
        's5_log_dt': jax.random.uniform(next(ks), (N_CD, 2, C_GROUPS), F32, math.log(1e-3), math.log(1e-1)),
        's5_b_re': nrm((N_CD, 2, C_GROUPS, C_STATE, C_GROUP), (2 * C_GROUP) ** -0.5),
        's5_b_im': nrm((N_CD, 2, C_GROUPS, C_STATE, C_GROUP), (2 * C_GROUP) ** -0.5),
        's5_c_re': nrm((N_CD, 2, C_GROUPS, C_GROUP, C_STATE), C_STATE ** -0.5),
        's5_c_im': nrm((N_CD, 2, C_GROUPS, C_GROUP, C_STATE), C_STATE ** -0.5),
        's5_d': nrm((N_CD, C_WIDTH), 1.0),
        's5_glu_w': nrm((N_CD, C_WIDTH, C_WIDTH), C_WIDTH ** -0.5),
        's5_glu_b': nrm((N_CD, C_WIDTH), 0.02),
        'd_sink': nrm((N_CD, D_HEADS), 0.5),
        'final_g': 1.0 + nrm((D,), 0.02),
    }


def reference(x, c, ctx, c_ctx, mod_w, mod_b, norm_g, ffn_w1, ffn_w3, ffn_w2, ab_w_in, ab_b_gate,
              ab_w_out, a_gq, a_gk, b_head_g, cd_w_in, cd_w_out, s5_a_re, s5_a_im, s5_log_dt, s5_b_re,
              s5_b_im, s5_c_re, s5_c_im, s5_d, s5_glu_w, s5_glu_b, d_sink, final_g):
    b, t, d = x.shape
    rows = t // GRID_W
    cos_a, sin_a = axial_rope(rows, A_DH)
    cos_d, sin_d = axial_rope(rows, D_DH)
    h, hc = x, ctx
    for l in range(DEPTH):
        want_ctx = l < DEPTH - 1
        m = (jax.nn.silu(c) @ mod_w[l] + mod_b[l]).reshape(b, N_MOD, 1, d)
        mc = (jax.nn.silu(c_ctx) @ mod_w[l] + mod_b[l]).reshape(1, N_MOD, 1, d)
        h = h + 0.5 * m[:, 2] * swiglu(modulate(h, norm_g[l, 0], m[:, 0], m[:, 1]),
                                       ffn_w1[l, 0], ffn_w3[l, 0], ffn_w2[l, 0])
        hc = hc + 0.5 * mc[:, 2] * swiglu(modulate(hc, norm_g[l, 0], mc[:, 0], mc[:, 1]),
                                          ffn_w1[l, 0], ffn_w3[l, 0], ffn_w2[l, 0])
        hn = modulate(h, norm_g[l, 1], m[:, 3], m[:, 4])
        hcn = modulate(hc, norm_g[l, 1], mc[:, 3], mc[:, 4])
        i = l // 2
        if l % 2 == 0:
            y, yc = ab_mixer(hn, hcn, ab_w_in[i], ab_b_gate[i], ab_w_out[i], a_gq[i], a_gk[i], b_head_g[i],
                             cos_a, sin_a, want_ctx)
        else:
            y, yc = cd_mixer(hn, hcn, cd_w_in[i], cd_w_out[i], s5_a_re[i], s5_a_im[i], s5_log_dt[i],
                             s5_b_re[i], s5_b_im[i], s5_c_re[i], s5_c_im[i], s5_d[i], s5_glu_w[i],
                             s5_glu_b[i], d_sink[i], cos_d, sin_d, want_ctx)
        h = h + m[:, 5] * y
        h = h + 0.5 * m[:, 8] * swiglu(modulate(h, norm_g[l, 2], m[:, 6], m[:, 7]),
                                       ffn_w1[l, 1], ffn_w3[l, 1], ffn_w2[l, 1])
        if want_ctx:
            hc = hc + mc[:, 5] * yc
            hc = hc + 0.5 * mc[:, 8] * swiglu(modulate(hc, norm_g[l, 2], mc[:, 6], mc[:, 7]),
                                              ffn_w1[l, 1], ffn_w3[l, 1], ffn_w2[l, 1])
    return rmsnorm(h, final_g)
```

```python
import functools
import math

import jax
import jax.numpy as jnp
from jax import lax
from jax.experimental import pallas as pl
from jax.experimental.pallas import tpu as pltpu

F32 = jnp.float32
BF16 = jnp.bfloat16

EPS = 1e-6
ROPE_THETA = 10000.0
GRID_W = 64
N_MOD = 9
Q_BLOCK = 128

A_HEADS, A_KV, A_DH = 8, 2, 128
B_HEADS, B_DQK, B_DV, B_CHUNK = 4, 128, 256, 64
C_WIDTH, C_GROUP, C_STATE = 1024, 16, 64
C_GROUPS = C_WIDTH // C_GROUP
D_HEADS, D_KV, D_DH, D_WINDOW = 16, 2, 64, 128

AB_GATE_COL = 4608
AB_IN_PAD = 5120
S5_CHUNK = 8
S5_GB = 8

LANE = 128
VMEM_LIMIT = 56 * 1024 * 1024


def _cparams(sem):
    return pltpu.CompilerParams(dimension_semantics=sem, vmem_limit_bytes=VMEM_LIMIT)


def _rms(x):
    return x * lax.rsqrt(jnp.mean(x * x, axis=-1, keepdims=True) + EPS)


def _sigmoid(x):
    return 1.0 / (1.0 + jnp.exp(-x))


def _log_sigmoid(x):
    return jnp.minimum(x, 0.0) - jnp.log(1.0 + jnp.exp(-jnp.abs(x)))


def _dot(a, b):
    return jnp.dot(a, b, preferred_element_type=F32)


def _dot_nt(a, b):
    return lax.dot_general(a, b, (((1,), (1,)), ((), ())), preferred_element_type=F32)


def _dot_tn(a, b):
    return lax.dot_general(a, b, (((0,), (0,)), ((), ())), preferred_element_type=F32)


def _modvec_kernel(c_ref, w_ref, b_ref, o_ref):
    c = c_ref[...]
    s = (c * _sigmoid(c)).astype(BF16)
    o_ref[0] = _dot(s, w_ref[0].astype(BF16)) + b_ref[0]


def _modvec_call(cvec, mod_w, mod_b):
    L, D, N = mod_w.shape
    tn = math.gcd(D, 1024)
    return pl.pallas_call(
        _modvec_kernel,
        grid=(L, N // tn),
        in_specs=[
            pl.BlockSpec((8, D), lambda l, j: (0, 0)),
            pl.BlockSpec((1, D, tn), lambda l, j: (l, 0, j)),
            pl.BlockSpec((1, 1, tn), lambda l, j: (l, 0, j)),
        ],
        out_specs=pl.BlockSpec((1, 8, tn), lambda l, j: (l, 0, j)),
        out_shape=jax.ShapeDtypeStruct((L, 8, N), F32),
        compiler_params=_cparams(("parallel", "parallel")),
        name="modvec",
    )(cvec, mod_w, mod_b.reshape(L, 1, N))


def _ffn_kernel(h_ref, mod_ref, g_ref, w1_ref, w3_ref, w2_ref, fg_ref, o_ref, hm_ref, acc_ref, *, final_norm):
    j = pl.program_id(1)

    @pl.when(j == 0)
    def _():
        y = _rms(h_ref[...]) * g_ref[...]
        hm = y * (1.0 + mod_ref[0, 1:2, :]) + mod_ref[0, 0:1, :]
        hm_ref[...] = hm.astype(BF16)
        acc_ref[...] = jnp.zeros_like(acc_ref)

    hm = hm_ref[...]
    a = _dot(hm, w1_ref[...])
    b = _dot(hm, w3_ref[...])
    gg = (a * _sigmoid(a)) * b
    acc_ref[...] += _dot(gg.astype(BF16), w2_ref[...])

    @pl.when(j == pl.num_programs(1) - 1)
    def _():
        out = h_ref[...] + (0.5 * mod_ref[0, 2:3, :]) * acc_ref[...]
        if final_norm:
            out = _rms(out) * fg_ref[...]
        o_ref[...] = out


def _ffn_call(h, mod, g, w1, w3, w2, fg, *, n_rows, seg_rows, n_seg, tm, tf, final_norm):
    D = h.shape[1]
    FF = w1.shape[1]
    seg = lambda i, j: (jnp.minimum(i * tm // seg_rows, n_seg), 0, 0)
    return pl.pallas_call(
        functools.partial(_ffn_kernel, final_norm=final_norm),
        grid=(n_rows // tm, FF // tf),
        in_specs=[
            pl.BlockSpec((tm, D), lambda i, j: (i, 0)),
            pl.BlockSpec((1, 8, D), seg),
            pl.BlockSpec((1, D), lambda i, j: (0, 0)),
            pl.BlockSpec((D, tf), lambda i, j: (0, j)),
            pl.BlockSpec((D, tf), lambda i, j: (0, j)),
            pl.BlockSpec((tf, D), lambda i, j: (j, 0)),
            pl.BlockSpec((1, D), lambda i, j: (0, 0)),
        ],
        out_specs=pl.BlockSpec((tm, D), lambda i, j: (i, 0)),
        out_shape=jax.ShapeDtypeStruct((n_rows, D), F32),
        scratch_shapes=[pltpu.VMEM((tm, D), BF16), pltpu.VMEM((tm, D), F32)],
        compiler_params=_cparams(("parallel", "arbitrary")),
        name="ffn",
    )(h, mod, g, w1, w3, w2, fg)


def _inproj_kernel(h_ref, mod_ref, g_ref, w_ref, o_ref, hm_ref):
    @pl.when(pl.program_id(1) == 0)
    def _():
        y = _rms(h_ref[...]) * g_ref[...]
        hm_ref[...] = (y * (1.0 + mod_ref[0, 1:2, :]) + mod_ref[0, 0:1, :]).astype(BF16)

    o_ref[...] = _dot(hm_ref[...], w_ref[...])


def _inproj_call(h, mod, g, w, *, seg_rows, n_seg, tm, tn):
    R, D = h.shape
    N = w.shape[1]
    seg = lambda i, j: (jnp.minimum(i * tm // seg_rows, n_seg), 0, 0)
    return pl.pallas_call(
        _inproj_kernel,
        grid=(R // tm, N // tn),
        in_specs=[
            pl.BlockSpec((tm, D), lambda i, j: (i, 0)),
            pl.BlockSpec((1, 8, D), seg),
            pl.BlockSpec((1, D), lambda i, j: (0, 0)),
            pl.BlockSpec((D, tn), lambda i, j: (0, j)),
        ],
        out_specs=pl.BlockSpec((tm, tn), lambda i, j: (i, j)),
        out_shape=jax.ShapeDtypeStruct((R, N), F32),
        scratch_shapes=[pltpu.VMEM((tm, D), BF16)],
        compiler_params=_cparams(("parallel", "arbitrary")),
        name="inproj",
    )(h, mod, g, w)


def _outproj_kernel(xa_ref, xb_ref, wa_ref, wb_ref, h_ref, mod_ref, o_ref):
    y = _dot(xa_ref[...], wa_ref[...]) + _dot(xb_ref[...], wb_ref[...])
    o_ref[...] = h_ref[...] + mod_ref[0, 0:1, :] * y


def _outproj_call(xa, xb, wa, wb, h, mod, *, n_rows, seg_rows, n_seg, tm):
    D = h.shape[1]
    Ka, Kb = xa.shape[1], xb.shape[1]
    seg = lambda i: (jnp.minimum(i * tm // seg_rows, n_seg), 0, 0)
    return pl.pallas_call(
        _outproj_kernel,
        grid=(n_rows // tm,),
        in_specs=[
            pl.BlockSpec((tm, Ka), lambda i: (i, 0)),
            pl.BlockSpec((tm, Kb), lambda i: (i, 0)),
            pl.BlockSpec((Ka, D), lambda i: (0, 0)),
            pl.BlockSpec((Kb, D), lambda i: (0, 0)),
            pl.BlockSpec((tm, D), lambda i: (i, 0)),
            pl.BlockSpec((1, 8, D), seg),
        ],
        out_specs=pl.BlockSpec((tm, D), lambda i: (i, 0)),
        out_shape=jax.ShapeDtypeStruct((n_rows, D), F32),
        compiler_params=_cparams(("parallel",)),
        name="outproj",
    )(xa, xb, wa, wb, h, mod)


def _rope_tables(T, head_dim):
    half = head_dim // 2
    n_freq = head_dim // 4
    t = jnp.arange(T)
    r = (t // GRID_W).astype(F32)
    cidx = (t % GRID_W).astype(F32)
    inv = ROPE_THETA ** (-jnp.arange(n_freq, dtype=F32) / n_freq)
    ang = jnp.concatenate([r[:, None] * inv, cidx[:, None] * inv], axis=-1)
    cos, sin = jnp.cos(ang), jnp.sin(ang)
    cos_h = jnp.concatenate([cos, cos], axis=-1)
    sin_h = jnp.concatenate([-sin, sin], axis=-1)
    reps = LANE // head_dim
    return jnp.tile(cos_h, (1, reps)), jnp.tile(sin_h, (1, reps))


def _rope128(x, cos, sin):
    return x * cos + pltpu.roll(x, 64, 1) * sin


def _rope64(x, cos, sin):
    lane = lax.broadcasted_iota(jnp.int32, x.shape, 1)
    first = jnp.bitwise_and(lane, 63) < 32
    partner = jnp.where(first, pltpu.roll(x, 96, 1), pltpu.roll(x, 32, 1))
    return x * cos + partner * sin


def _attn_a_kernel(q_ref, k_ref, v_ref, kc_ref, vc_ref, cq_ref, sq_ref, ck_ref, sk_ref, gq_ref, gk_ref,
                   o_ref, kall, vall, *, tc, t, rk):
    grp = A_HEADS // A_KV

    @pl.when(pl.program_id(2) == 0)
    def _():
        gk = gk_ref[...]
        kall[0:tc, :] = (_rms(kc_ref[...]) * gk).astype(BF16)
        vall[0:tc, :] = vc_ref[...].astype(BF16)

        def body(c, carry):
            rows = pl.ds(pl.multiple_of(c * rk, rk), rk)
            kn = _rms(k_ref[rows, :]) * gk
            dst = pl.ds(pl.multiple_of(tc + c * rk, rk), rk)
            kall[dst, :] = _rope128(kn, ck_ref[rows, :], sk_ref[rows, :]).astype(BF16)
            vall[dst, :] = v_ref[rows, :].astype(BF16)
            return carry

        lax.fori_loop(0, t // rk, body, 0)

    q4 = q_ref[...]
    qs = jnp.concatenate([q4[:, h * A_DH:(h + 1) * A_DH] for h in range(grp)], axis=0)
    qn = _rms(qs) * gq_ref[...]
    cos = jnp.concatenate([cq_ref[...]] * grp, axis=0)
    sin = jnp.concatenate([sq_ref[...]] * grp, axis=0)
    qr = (_rope128(qn, cos, sin) * (A_DH ** -0.5)).astype(BF16)
    s = _dot_nt(qr, kall[...])
    p = jnp.exp(s - jnp.max(s, axis=-1, keepdims=True))
    l = jnp.sum(p, axis=-1, keepdims=True)
    o = _dot(p.astype(BF16), vall[...]) / l
    tq = q4.shape[0]
    for h in range(grp):
        o_ref[:, h * A_DH:(h + 1) * A_DH] = o[h * tq:(h + 1) * tq, :].astype(o_ref.dtype)


def _attn_a_call(z, cos, sin, gq, gk, *, B, T, Tc):
    tq = Q_BLOCK
    nq = T // tq
    grp_w = (A_HEADS // A_KV) * A_DH
    kcol = A_HEADS
    cbase = B * T // Tc
    rk = min(512, T)
    return pl.pallas_call(
        functools.partial(_attn_a_kernel, tc=Tc, t=T, rk=rk),
        grid=(B, A_KV, nq),
        in_specs=[
            pl.BlockSpec((tq, grp_w), lambda b, kv, i: (b * nq + i, kv)),
            pl.BlockSpec((T, A_DH), lambda b, kv, i: (b, kcol + kv)),
            pl.BlockSpec((T, A_DH), lambda b, kv, i: (b, kcol + A_KV + kv)),
            pl.BlockSpec((Tc, A_DH), lambda b, kv, i: (cbase + b, kcol + kv)),
            pl.BlockSpec((Tc, A_DH), lambda b, kv, i: (cbase + b, kcol + A_KV + kv)),
            pl.BlockSpec((tq, LANE), lambda b, kv, i: (i, 0)),
            pl.BlockSpec((tq, LANE), lambda b, kv, i: (i, 0)),
            pl.BlockSpec((T, LANE), lambda b, kv, i: (0, 0)),
            pl.BlockSpec((T, LANE), lambda b, kv, i: (0, 0)),
            pl.BlockSpec((1, A_DH), lambda b, kv, i: (0, 0)),
            pl.BlockSpec((1, A_DH), lambda b, kv, i: (0, 0)),
        ],
        out_specs=pl.BlockSpec((tq, grp_w), lambda b, kv, i: (b * nq + i, kv)),
        out_shape=jax.ShapeDtypeStruct((B * T, A_HEADS * A_DH), BF16),
        scratch_shapes=[pltpu.VMEM((Tc + T, A_DH), BF16), pltpu.VMEM((Tc + T, A_DH), BF16)],
        compiler_params=_cparams(("parallel", "parallel", "arbitrary")),
        name="attn_a",
    )(z, z, z, z, z, cos, sin, cos, sin, gq, gk)


def _attn_a_ctx_kernel(q_ref, kc_ref, vc_ref, gq_ref, gk_ref, o_ref):
    grp = A_HEADS // A_KV
    q4 = q_ref[...]
    tq = q4.shape[0]
    qs = jnp.concatenate([q4[:, h * A_DH:(h + 1) * A_DH] for h in range(grp)], axis=0)
    qn = (_rms(qs) * gq_ref[...] * (A_DH ** -0.5)).astype(BF16)
    kc = (_rms(kc_ref[...]) * gk_ref[...]).astype(BF16)
    s = _dot_nt(qn, kc)
    p = jnp.exp(s - jnp.max(s, axis=-1, keepdims=True))
    l = jnp.sum(p, axis=-1, keepdims=True)
    o = _dot(p.astype(BF16), vc_ref[...].astype(BF16)) / l
    for h in range(grp):
        o_ref[:, h * A_DH:(h + 1) * A_DH] = o[h * tq:(h + 1) * tq, :].astype(o_ref.dtype)


def _attn_a_ctx_call(z, gq, gk, *, B, T, Tc):
    grp_w = (A_HEADS // A_KV) * A_DH
    kcol = A_HEADS
    cbase = B * T // Tc
    return pl.pallas_call(
        _attn_a_ctx_kernel,
        grid=(B, A_KV),
        in_specs=[
            pl.BlockSpec((Tc, grp_w), lambda b, kv: (cbase + b, kv)),
            pl.BlockSpec((Tc, A_DH), lambda b, kv: (cbase + b, kcol + kv)),
            pl.BlockSpec((Tc, A_DH), lambda b, kv: (cbase + b, kcol + A_KV + kv)),
            pl.BlockSpec((1, A_DH), lambda b, kv: (0, 0)),
            pl.BlockSpec((1, A_DH), lambda b, kv: (0, 0)),
        ],
        out_specs=pl.BlockSpec((Tc, grp_w), lambda b, kv: (b, kv)),
        out_shape=jax.ShapeDtypeStruct((B * Tc, A_HEADS * A_DH), BF16),
        compiler_params=_cparams(("parallel", "parallel")),
        name="attn_a_ctx",
    )(z, z, z, gq, gk)


def _mlstm_unit(d, hd, q, k, v, gcol, grow, c_s, n_s, m_s, row, col):
    L = B_CHUNK
    u = d * B_HEADS + hd
    gi = 2 * d * B_HEADS + hd
    gf = gi + B_HEADS
    li_col = gcol[:, gi:gi + 1]
    lf_col = _log_sigmoid(gcol[:, gf:gf + 1])
    li_row = grow[gi:gi + 1, :]
    lf_row = _log_sigmoid(grow[gf:gf + 1, :])
    mask = (col <= row) if d == 0 else (col >= row)
    mask_t = (row <= col) if d == 0 else (row >= col)
    bcum_col = jnp.sum(jnp.where(mask, lf_row, 0.0), axis=1, keepdims=True)
    bcum_row = jnp.sum(jnp.where(mask_t, lf_col, 0.0), axis=0, keepdims=True)
    log_d = jnp.where(mask, bcum_col - bcum_row + li_row, -jnp.inf)
    m_prev = m_s[u, 0:1, 0:1]
    m_inter = bcum_col + m_prev
    m_t = jnp.maximum(m_inter, jnp.max(log_d, axis=1, keepdims=True))
    w_intra = jnp.exp(log_d - m_t)
    w_inter = jnp.exp(m_inter - m_t)
    qb = (q * (B_DQK ** -0.5)).astype(BF16)
    kb = k.astype(BF16)
    c_prev = c_s[u]
    n_prev = n_s[u, 0:1, :]
    s = _dot_nt(qb, kb) * w_intra
    num = _dot(s.astype(BF16), v.astype(BF16)) + w_inter * _dot_nt(qb, c_prev.astype(BF16))
    qf = qb.astype(F32)
    den = jnp.sum(s, axis=1, keepdims=True) + w_inter * jnp.sum(qf * n_prev, axis=1, keepdims=True)
    h = num / jnp.maximum(jnp.abs(den), jnp.exp(-m_t))
    b_last = jnp.sum(lf_row, axis=1, keepdims=True)
    log_w = b_last - bcum_col + li_col
    m_new = jnp.maximum(b_last + m_prev, jnp.max(log_w, axis=0, keepdims=True))
    w = jnp.exp(log_w - m_new)
    decay = jnp.exp(b_last + m_prev - m_new)
    c_s[u] = decay * c_prev + _dot_tn((w * v).astype(BF16), kb)
    n_s[u, 0:1, :] = decay * n_prev + jnp.sum(w * kb.astype(F32), axis=0, keepdims=True)
    m_s[u] = jnp.broadcast_to(m_new, m_s.shape[1:])
    return h


def _mlstm_kernel(qf_ref, kf_ref, vf0_ref, vf1_ref, gcf_ref, grf_ref,
                  qb_ref, kb_ref, vb0_ref, vb1_ref, gcb_ref, grb_ref,
                  bcol_ref, brow_ref, of_ref, ob_ref, c_s, n_s, m_s):
    L = B_CHUNK

    @pl.when(pl.program_id(1) == 0)
    def _():
        c_s[...] = jnp.zeros_like(c_s)
        n_s[...] = jnp.zeros_like(n_s)
        m_s[...] = jnp.zeros_like(m_s)

    row = lax.broadcasted_iota(jnp.int32, (L, L), 0)
    col = lax.broadcasted_iota(jnp.int32, (L, L), 1)
    for d, (q_ref, k_ref, v0_ref, v1_ref, gc_ref, gr_ref, o_ref) in enumerate((
            (qf_ref, kf_ref, vf0_ref, vf1_ref, gcf_ref, grf_ref, of_ref),
            (qb_ref, kb_ref, vb0_ref, vb1_ref, gcb_ref, grb_ref, ob_ref))):
        gcol = gc_ref[...] + bcol_ref[...]
        grow = gr_ref[0] + brow_ref[...]
        for hd in range(B_HEADS):
            q = q_ref[:, hd * B_DQK:(hd + 1) * B_DQK]
            k = k_ref[:, hd * B_DQK:(hd + 1) * B_DQK]
            v_ref = v0_ref if hd < 2 else v1_ref
            v = v_ref[:, (hd % 2) * B_DV:(hd % 2 + 1) * B_DV]
            h = _mlstm_unit(d, hd, q, k, v, gcol, grow, c_s, n_s, m_s, row, col)
            o_ref[0, :, hd * B_DV:(hd + 1) * B_DV] = h


def _mlstm_call(z, grow, bcol, brow, *, B, T, Tc):
    L = B_CHUNK
    ncl, ncc = T // L, Tc // L
    nc = ncl + ncc
    cbase = B * T // L
    R = z.shape[0]

    def blk(b, c, d):
        is_ctx = c < ncc
        jc = c if d == 0 else ncc - 1 - c
        jl = (c - ncc) if d == 0 else (ncl - 1 - (c - ncc))
        return jnp.where(is_ctx, cbase + b * ncc + jc, b * ncl + jl)

    def specs(d):
        return [
            pl.BlockSpec((L, 512), lambda b, c: (blk(b, c, d), 3)),
            pl.BlockSpec((L, 512), lambda b, c: (blk(b, c, d), 4)),
            pl.BlockSpec((L, 512), lambda b, c: (blk(b, c, d), 5)),
            pl.BlockSpec((L, 512), lambda b, c: (blk(b, c, d), 6)),
            pl.BlockSpec((L, LANE), lambda b, c: (blk(b, c, d), AB_GATE_COL // LANE)),
            pl.BlockSpec((1, 16, L), lambda b, c: (blk(b, c, d), 0, 0)),
        ]

    out_sds = jax.ShapeDtypeStruct((1, R, B_HEADS * B_DV), F32)
    return pl.pallas_call(
        _mlstm_kernel,
        grid=(B, nc),
        in_specs=specs(0) + specs(1) + [
            pl.BlockSpec((1, LANE), lambda b, c: (0, 0)),
            pl.BlockSpec((16, 1), lambda b, c: (0, 0)),
        ],
        out_specs=[
            pl.BlockSpec((1, L, B_HEADS * B_DV), lambda b, c: (0, blk(b, c, 0), 0)),
            pl.BlockSpec((1, L, B_HEADS * B_DV), lambda b, c: (0, blk(b, c, 1), 0)),
        ],
        out_shape=[out_sds, out_sds],
        scratch_shapes=[
            pltpu.VMEM((2 * B_HEADS, B_DV, B_DQK), F32),
            pltpu.VMEM((2 * B_HEADS, 8, B_DQK), F32),
            pltpu.VMEM((2 * B_HEADS, 8, LANE), F32),
        ],
        compiler_params=_cparams(("parallel", "arbitrary")),
        name="mlstm",
    )(z, z, z, z, z, grow, z, z, z, z, z, grow, bcol, brow)


def _mlstm_finish_kernel(hf_ref, hb_ref, o0_ref, o1_ref, hg_ref, y_ref):
    for hd in range(B_HEADS):
        sl = slice(hd * B_DV, (hd + 1) * B_DV)
        hh = hf_ref[0, :, sl] + hb_ref[0, :, sl]
        o_ref = o0_ref if hd < 2 else o1_ref
        o = o_ref[:, (hd % 2) * B_DV:(hd % 2 + 1) * B_DV]
        y_ref[:, sl] = (_sigmoid(o) * _rms(hh) * hg_ref[:, sl]).astype(y_ref.dtype)


def _mlstm_finish_call(hf, hb, z, head_g, *, tm):
    R = z.shape[0]
    W = B_HEADS * B_DV
    return pl.pallas_call(
        _mlstm_finish_kernel,
        grid=(R // tm,),
        in_specs=[
            pl.BlockSpec((1, tm, W), lambda i: (0, i, 0)),
            pl.BlockSpec((1, tm, W), lambda i: (0, i, 0)),
            pl.BlockSpec((tm, 512), lambda i: (i, 7)),
            pl.BlockSpec((tm, 512), lambda i: (i, 8)),
            pl.BlockSpec((1, W), lambda i: (0, 0)),
        ],
        out_specs=pl.BlockSpec((tm, W), lambda i: (i, 0)),
        out_shape=jax.ShapeDtypeStruct((R, W), BF16),
        compiler_params=_cparams(("parallel",)),
        name="mlstm_finish",
    )(hf, hb, z, z, head_g)


def _s5_params(a_re, a_im, log_dt, b_re, b_im, c_re, c_im, reverse):
    hp = lax.Precision.HIGHEST
    Lc = S5_CHUNK
    dt = jnp.exp(log_dt)[:, None]
    lam_re, lam_im = a_re * dt, a_im * dt
    mag = jnp.exp(lam_re)
    ab_re, ab_im = mag * jnp.cos(lam_im), mag * jnp.sin(lam_im)
    den = a_re * a_re + a_im * a_im
    nr, ni = ab_re - 1.0, ab_im
    k_re, k_im = (nr * a_re + ni * a_im) / den, (ni * a_re - nr * a_im) / den
    bb_re = k_re[..., None] * b_re - k_im[..., None] * b_im
    bb_im = k_re[..., None] * b_im + k_im[..., None] * b_re
    tau = jnp.arange(Lc + 1, dtype=F32)[:, None, None]
    pm = jnp.exp(tau * lam_re)
    pw_re, pw_im = pm * jnp.cos(tau * lam_im), pm * jnp.sin(tau * lam_im)
    cp_re = c_re[None] * pw_re[:, :, None, :] - c_im[None] * pw_im[:, :, None, :]
    cp_im = c_re[None] * pw_im[:, :, None, :] + c_im[None] * pw_re[:, :, None, :]
    kk = (jnp.einsum('tgop,gpi->tgoi', cp_re, bb_re, precision=hp)
          - jnp.einsum('tgop,gpi->tgoi', cp_im, bb_im, precision=hp))
    G = a_re.shape[0]
    i_idx = jnp.arange(Lc)[:, None]
    j_idx = jnp.arange(Lc)[None, :]
    lag = (i_idx - j_idx) if reverse else (j_idx - i_idx)
    valid = lag >= 0
    kt = jnp.where(valid[:, :, None, None, None], kk[jnp.clip(lag, 0, Lc)], 0.0)
    kt = kt.transpose(2, 0, 4, 1, 3).reshape(G, Lc * C_GROUP, Lc * C_GROUP)
    e = jnp.arange(Lc) if reverse else (Lc - 1 - jnp.arange(Lc))
    w_re = pw_re[e][:, :, :, None] * bb_re[None] - pw_im[e][:, :, :, None] * bb_im[None]
    w_im = pw_re[e][:, :, :, None] * bb_im[None] + pw_im[e][:, :, :, None] * bb_re[None]
    w_re = w_re.transpose(1, 0, 3, 2).reshape(G, Lc * C_GROUP, C_STATE)
    w_im = w_im.transpose(1, 0, 3, 2).reshape(G, Lc * C_GROUP, C_STATE)
    wp = jnp.concatenate([w_re, w_im], axis=-1)
    wps = jnp.concatenate([w_im, w_re], axis=-1)
    f = (Lc - jnp.arange(Lc)) if reverse else (jnp.arange(Lc) + 1)
    cj_re, cj_im = cp_re[f], cp_im[f]
    cp = jnp.concatenate([cj_re.transpose(1, 3, 0, 2).reshape(G, C_STATE, Lc * C_GROUP),
                          -cj_im.transpose(1, 3, 0, 2).reshape(G, C_STATE, Lc * C_GROUP)], axis=1)
    a8_re, a8_im = pw_re[Lc], pw_im[Lc]
    a1 = jnp.concatenate([a8_re, a8_re], axis=-1)[:, None, :]
    a2 = jnp.concatenate([-a8_im, a8_im], axis=-1)[:, None, :]
    return kt.astype(BF16), wp.astype(BF16), wps.astype(BF16), cp.astype(BF16), a1, a2


def _s5_kernel(u_ref, kt_ref, wp_ref, wps_ref, cp_ref, a1_ref, a2_ref, y_ref, w_s, ws_s, xp_s, *, nb, ncc, ncl):
    nc = ncc + ncl
    for d in range(2):
        for g in range(S5_GB):
            ug = u_ref[g]
            w_s[g] = _dot(ug, wp_ref[d, g])
            ws_s[g] = _dot(ug, wps_ref[d, g])
        a1 = [jnp.broadcast_to(a1_ref[d, g], (nb, LANE)) for g in range(S5_GB)]
        a2 = [jnp.broadcast_to(a2_ref[d, g], (nb, LANE)) for g in range(S5_GB)]

        def body(s, carry):
            if d == 0:
                c = s
            else:
                c = jnp.where(s < ncc, ncc - 1 - s, nc + ncc - 1 - s)
            rows = pl.ds(pl.multiple_of(c * nb, nb), nb)
            new = []
            for g in range(S5_GB):
                x, xs = carry[2 * g], carry[2 * g + 1]
                xp_s[g, rows, :] = x
                new.append(a1[g] * x + a2[g] * xs + w_s[g, rows, :])
                new.append(a1[g] * xs - a2[g] * x + ws_s[g, rows, :])
            return tuple(new)

        init = tuple(jnp.zeros((nb, LANE), F32) for _ in range(2 * S5_GB))
        lax.fori_loop(0, nc, body, init)
        for g in range(S5_GB):
            y = _dot(u_ref[g], kt_ref[d, g]) + _dot(xp_s[g].astype(BF16), cp_ref[d, g])
            if d == 0:
                y_ref[g] = y
            else:
                y_ref[g] += y


def _s5_call(ug, kt, wp, wps, cp, a1, a2, *, nb, ncc, ncl):
    G, R, _ = ug.shape
    wspec = pl.BlockSpec((2, S5_GB, LANE, LANE), lambda i: (0, i, 0, 0))
    aspec = pl.BlockSpec((2, S5_GB, 1, LANE), lambda i: (0, i, 0, 0))
    return pl.pallas_call(
        functools.partial(_s5_kernel, nb=nb, ncc=ncc, ncl=ncl),
        grid=(G // S5_GB,),
        in_specs=[pl.BlockSpec((S5_GB, R, LANE), lambda i: (i, 0, 0)), wspec, wspec, wspec, wspec, aspec, aspec],
        out_specs=pl.BlockSpec((S5_GB, R, LANE), lambda i: (i, 0, 0)),
        out_shape=jax.ShapeDtypeStruct((G, R, LANE), F32),
        scratch_shapes=[pltpu.VMEM((S5_GB, R, LANE), F32)] * 3,
        compiler_params=_cparams(("parallel",)),
        name="s5_scan",
    )(ug, kt, wp, wps, cp, a1, a2)


def _s5_finish_kernel(y_ref, u_ref, d_ref, w_ref, b_ref, o_ref):
    y = y_ref[...] + d_ref[...] * u_ref[...]
    g = 0.5 * y * (1.0 + jnp.tanh(math.sqrt(2.0 / math.pi) * (y + 0.044715 * (y * y * y))))
    o_ref[...] = (g * _sigmoid(_dot(g.astype(BF16), w_ref[...]) + b_ref[...])).astype(o_ref.dtype)


def _s5_finish_call(y, z, d_skip, glu_w, glu_b, *, n_rows, tm):
    W = C_WIDTH
    return pl.pallas_call(
        _s5_finish_kernel,
        grid=(n_rows // tm,),
        in_specs=[
            pl.BlockSpec((tm, W), lambda i: (i, 0)),
            pl.BlockSpec((tm, W), lambda i: (i, 0)),
            pl.BlockSpec((1, W), lambda i: (0, 0)),
            pl.BlockSpec((W, W), lambda i: (0, 0)),
            pl.BlockSpec((1, W), lambda i: (0, 0)),
        ],
        out_specs=pl.BlockSpec((tm, W), lambda i: (i, 0)),
        out_shape=jax.ShapeDtypeStruct((n_rows, W), BF16),
        compiler_params=_cparams(("parallel",)),
        name="s5_finish",
    )(y, z, d_skip, glu_w, glu_b)


def _attn_d_kernel(q_ref, k_ref, v_ref, kc_ref, vc_ref, cq_ref, sq_ref, ck_ref, sk_ref, sink_ref,
                   o_ref, kpe, kpo, vpe, vpo, kce, kco, vce, vco, *, t, tc, rk):
    W = D_WINDOW
    kv = pl.program_id(1)
    i = pl.program_id(2)
    lane = lax.broadcasted_iota(jnp.int32, (1, LANE), 1)
    lo = lane < D_DH

    def halves(x):
        sel = jnp.where(kv == 0, x, pltpu.roll(x, 64, 1))
        e = jnp.where(lo, sel, 0.0)
        return e.astype(BF16), pltpu.roll(e, 64, 1).astype(BF16)

    @pl.when(i == 0)
    def _():
        zpad = jnp.zeros((W, LANE), BF16)
        for ref in (kpe, kpo, vpe, vpo):
            ref[0:W, :] = zpad
            ref[W + t:W + t + W, :] = zpad
        kce[...], kco[...] = halves(kc_ref[...])
        vce[...], vco[...] = halves(vc_ref[...])

        def body(c, carry):
            rows = pl.ds(pl.multiple_of(c * rk, rk), rk)
            dst = pl.ds(pl.multiple_of(W + c * rk, rk), rk)
            kpe[dst, :], kpo[dst, :] = halves(_rope64(k_ref[rows, :], ck_ref[rows, :], sk_ref[rows, :]))
            vpe[dst, :], vpo[dst, :] = halves(v_ref[rows, :])
            return carry

        lax.fori_loop(0, t // rk, body, 0)

    tq = q_ref.shape[0]
    nslab = q_ref.shape[1] // LANE
    cos, sin = cq_ref[...], sq_ref[...]
    qs = jnp.concatenate(
        [_rope64(q_ref[:, j * LANE:(j + 1) * LANE], cos, sin) for j in range(nslab)], axis=0)
    qs = (qs * (D_DH ** -0.5)).astype(BF16)
    span = tq + 2 * W
    rows_k = pl.ds(pl.multiple_of(i * tq, tq), span)
    rr = jnp.bitwise_and(lax.broadcasted_iota(jnp.int32, (nslab * tq, span), 0), tq - 1)
    cc = lax.broadcasted_iota(jnp.int32, (nslab * tq, span), 1)
    kpos = i * tq - W + cc
    ok = (kpos >= 0) & (kpos < t) & (jnp.abs(rr + W - cc) <= W)
    out = None
    for par, (kp, vp, kcx, vcx) in enumerate(((kpe, vpe, kce, vce), (kpo, vpo, kco, vco))):
        s_loc = jnp.where(ok, _dot_nt(qs, kp[rows_k, :]), -jnp.inf)
        s_ctx = _dot_nt(qs, kcx[...])
        snk = jnp.concatenate(
            [jnp.broadcast_to(sink_ref[0, 0:1, 2 * j + par:2 * j + par + 1], (tq, 1)) for j in range(nslab)],
            axis=0)
        m = jnp.maximum(jnp.maximum(jnp.max(s_loc, axis=-1, keepdims=True),
                                    jnp.max(s_ctx, axis=-1, keepdims=True)), snk)
        p_loc = jnp.exp(s_loc - m)
        p_ctx = jnp.exp(s_ctx - m)
        l = (jnp.sum(p_loc, axis=-1, keepdims=True) + jnp.sum(p_ctx, axis=-1, keepdims=True)
             + jnp.exp(snk - m))
        o = (_dot(p_ctx.astype(BF16), vcx[...]) + _dot(p_loc.astype(BF16), vp[rows_k, :])) / l
        out = o if out is None else out + o
    for j in range(nslab):
        o_ref[:, j * LANE:(j + 1) * LANE] = out[j * tq:(j + 1) * tq, :].astype(o_ref.dtype)


def _attn_d_call(z, cos, sin, sink, *, B, T, Tc):
    tq = Q_BLOCK
    nq = T // tq
    grp_w = (D_HEADS // D_KV) * D_DH
    qcol = C_WIDTH // grp_w
    kcol = (C_WIDTH + D_HEADS * D_DH) // LANE
    cbase = B * T // Tc
    rk = min(512, T)
    pad = lambda n: pltpu.VMEM((n, LANE), BF16)
    return pl.pallas_call(
        functools.partial(_attn_d_kernel, t=T, tc=Tc, rk=rk),
        grid=(B, D_KV, nq),
        in_specs=[
            pl.BlockSpec((tq, grp_w), lambda b, kv, i: (b * nq + i, qcol + kv)),
            pl.BlockSpec((T, LANE), lambda b, kv, i: (b, kcol)),
            pl.BlockSpec((T, LANE), lambda b, kv, i: (b, kcol + 1)),
            pl.BlockSpec((Tc, LANE), lambda b, kv, i: (cbase + b, kcol)),
            pl.BlockSpec((Tc, LANE), lambda b, kv, i: (cbase + b, kcol + 1)),
            pl.BlockSpec((tq, LANE), lambda b, kv, i: (i, 0)),
            pl.BlockSpec((tq, LANE), lambda b, kv, i: (i, 0)),
            pl.BlockSpec((T, LANE), lambda b, kv, i: (0, 0)),
            pl.BlockSpec((T, LANE), lambda b, kv, i: (0, 0)),
            pl.BlockSpec((1, 1, LANE), lambda b, kv, i: (kv, 0, 0)),
        ],
        out_specs=pl.BlockSpec((tq, grp_w), lambda b, kv, i: (b * nq + i, kv)),
        out_shape=jax.ShapeDtypeStruct((B * T, D_HEADS * D_DH), BF16),
        scratch_shapes=[pad(T + 2 * D_WINDOW)] * 4 + [pad(Tc)] * 4,
        compiler_params=_cparams(("parallel", "parallel", "arbitrary")),
        name="attn_d",
    )(z, z, z, z, z, cos, sin, cos, sin, sink)


def _mod_rows(ml, idx):
    sel = ml[:, jnp.array(idx), :]
    return jnp.pad(sel, ((0, 0), (0, 8 - len(idx)), (0, 0)))


def kernel(x, c, ctx, c_ctx, mod_w, mod_b, norm_g, ffn_w1, ffn_w3, ffn_w2, ab_w_in, ab_b_gate, ab_w_out, a_gq, a_gk, b_head_g, cd_w_in, cd_w_out, s5_a_re, s5_a_im, s5_log_dt, s5_b_re, s5_b_im, s5_c_re, s5_c_im, s5_d, s5_glu_w, s5_glu_b, d_sink, final_g):
    B, T, D = x.shape
    Tc = ctx.shape[1]
    depth = mod_w.shape[0]
    n_lat, n_ctx = B * T, B * Tc
    n_all = n_lat + n_ctx
    tm = math.gcd(math.gcd(T, n_ctx), 512)
    tf = math.gcd(ffn_w1.shape[-1], 512)
    seg = dict(seg_rows=T, n_seg=B)

    h = jnp.concatenate([x.reshape(n_lat, D), ctx.reshape(n_ctx, D)], axis=0)
    cvec = jnp.zeros((8, D), F32).at[:B].set(c).at[B].set(c_ctx)
    mods = _modvec_call(cvec, mod_w, mod_b).reshape(depth, 8, N_MOD, D)[:, :B + 1]
    cos_a, sin_a = _rope_tables(T, A_DH)
    cos_d, sin_d = _rope_tables(T, D_DH)
    w1, w3, w2 = ffn_w1.astype(BF16), ffn_w3.astype(BF16), ffn_w2.astype(BF16)
    ones = jnp.ones((1, D), F32)

    for l in range(depth):
        last = l == depth - 1
        want_ctx = not last
        ml = mods[l]
        i = l // 2
        h = _ffn_call(h, _mod_rows(ml, (0, 1, 2)), norm_g[l, 0][None], w1[l, 0], w3[l, 0], w2[l, 0], ones,
                      n_rows=n_all, tm=tm, tf=tf, final_norm=False, **seg)
        mod_in = _mod_rows(ml, (3, 4))
        mod_out = _mod_rows(ml, (5,))
        if l % 2 == 0:
            w_in = jnp.pad(ab_w_in[i], ((0, 0), (0, AB_IN_PAD - ab_w_in.shape[-1]))).astype(BF16)
            z = _inproj_call(h, mod_in, norm_g[l, 1][None], w_in, tm=tm, tn=1024, **seg)
            ya = _attn_a_call(z, cos_a, sin_a, a_gq[i][None], a_gk[i][None], B=B, T=T, Tc=Tc)
            yac = _attn_a_ctx_call(z, a_gq[i][None], a_gk[i][None], B=B, T=T, Tc=Tc)
            ng = 4 * B_HEADS
            gates = z[:, AB_GATE_COL:AB_GATE_COL + ng]
            grow = gates.reshape(n_all // B_CHUNK, B_CHUNK, ng).transpose(0, 2, 1)
            bcol = jnp.pad(ab_b_gate[i], (0, LANE - ng))[None]
            brow = ab_b_gate[i][:, None]
            hf, hb = _mlstm_call(z, grow, bcol, brow, B=B, T=T, Tc=Tc)
            yb = _mlstm_finish_call(hf, hb, z, b_head_g[i][None], tm=tm)
            xa = jnp.concatenate([ya, yac], axis=0)
            w_out = ab_w_out[i].astype(BF16)
            ka = A_HEADS * A_DH
            n_rows = n_all
        else:
            w_in = cd_w_in[i].astype(BF16)
            z = _inproj_call(h, mod_in, norm_g[l, 1][None], w_in, tm=tm, tn=w_in.shape[1] // 3, **seg)
            Lc = S5_CHUNK
            ncl, ncc = T // Lc, Tc // Lc

            def fold(u, n):
                return (u.reshape(B, n, Lc, C_GROUPS, C_GROUP).transpose(3, 1, 0, 2, 4)
                        .reshape(C_GROUPS, n * B, Lc * C_GROUP))

            u_all = z[:, :C_WIDTH].astype(BF16)
            ug = jnp.concatenate([fold(u_all[n_lat:], ncc), fold(u_all[:n_lat], ncl)], axis=1)
            prm = [_s5_params(s5_a_re[i, d], s5_a_im[i, d], s5_log_dt[i, d], s5_b_re[i, d], s5_b_im[i, d],
                              s5_c_re[i, d], s5_c_im[i, d], reverse=(d == 1)) for d in range(2)]
            kt, wp, wps, cp, a1, a2 = [jnp.stack([prm[0][k], prm[1][k]]) for k in range(6)]
            yg = _s5_call(ug, kt, wp, wps, cp, a1, a2, nb=B, ncc=ncc, ncl=ncl)
            ys5 = (yg[:, ncc * B:].reshape(C_GROUPS, ncl, B, Lc, C_GROUP).transpose(2, 1, 3, 0, 4)
                   .reshape(n_lat, C_WIDTH))
            xa = _s5_finish_call(ys5, z, s5_d[i][None], s5_glu_w[i].astype(BF16), s5_glu_b[i][None],
                                 n_rows=n_lat, tm=tm)
            sink = jnp.pad(d_sink[i].reshape(D_KV, 1, D_HEADS // D_KV), ((0, 0), (0, 0), (0, LANE - D_HEADS // D_KV)))
            yb = _attn_d_call(z, cos_d, sin_d, sink, B=B, T=T, Tc=Tc)
            w_out = cd_w_out[i].astype(BF16)
            ka = C_WIDTH
            n_rows = n_all if want_ctx else n_lat
        if not want_ctx:
            n_rows = n_lat
        h = _outproj_call(xa, yb, w_out[:ka], w_out[ka:], h, mod_out, n_rows=n_rows, tm=tm, **seg)
        h = _ffn_call(h, _mod_rows(ml, (6, 7, 8)), norm_g[l, 2][None], w1[l, 1], w3[l, 1], w2[l, 1],
                      final_g[None] if last else ones, n_rows=n_rows, tm=tm, tf=tf, final_norm=last, **seg)
    return h[:n_lat].reshape(B, T, D)
```

```python
import functools
import math

import jax
import jax.numpy as jnp
from jax import lax
from jax.experimental import pallas as pl
from jax.experimental.pallas import tpu as pltpu

F32 = jnp.float32
BF16 = jnp.bfloat16

EPS = 1e-6
ROPE_THETA = 10000.0
GRID_W = 64
N_MOD = 9
Q_BLOCK = 128

A_HEADS, A_KV, A_DH = 8, 2, 128
B_HEADS, B_DQK, B_DV, B_CHUNK = 4, 128, 256, 64
C_WIDTH, C_GROUP, C_STATE = 1024, 16, 64
C_GROUPS = C_WIDTH // C_GROUP
D_HEADS, D_KV, D_DH, D_WINDOW = 16, 2, 64, 128

AB_GATE_COL = 4608
AB_IN_PAD = 5120
S5_CHUNK = 8
S5_GB = 8

LANE = 128
LOG2E = math.log2(math.e)
VMEM_LIMIT = 56 * 1024 * 1024


def _cparams(sem):
    return pltpu.CompilerParams(dimension_semantics=sem, vmem_limit_bytes=VMEM_LIMIT)


def _rms(x):
    return x * lax.rsqrt(jnp.mean(x * x, axis=-1, keepdims=True) + EPS)


def _sigmoid(x):
    return 1.0 / (1.0 + jnp.exp(-x))


def _log_sigmoid(x):
    return jnp.minimum(x, 0.0) - jnp.log(1.0 + jnp.exp(-jnp.abs(x)))


def _dot(a, b):
    return jnp.dot(a, b, preferred_element_type=F32)


def _dot_nt(a, b):
    return lax.dot_general(a, b, (((1,), (1,)), ((), ())), preferred_element_type=F32)


def _dot_tn(a, b):
    return lax.dot_general(a, b, (((0,), (0,)), ((), ())), preferred_element_type=F32)


def _modvec_kernel(c_ref, w_ref, b_ref, o_ref):
    c = c_ref[...]
    s = (c * _sigmoid(c)).astype(BF16)
    o_ref[0] = _dot(s, w_ref[0].astype(BF16)) + b_ref[0]


def _modvec_call(cvec, mod_w, mod_b):
    L, D, N = mod_w.shape
    tn = math.gcd(D, 1024)
    return pl.pallas_call(
        _modvec_kernel,
        grid=(L, N // tn),
        in_specs=[
            pl.BlockSpec((8, D), lambda l, j: (0, 0)),
            pl.BlockSpec((1, D, tn), lambda l, j: (l, 0, j)),
            pl.BlockSpec((1, 1, tn), lambda l, j: (l, 0, j)),
        ],
        out_specs=pl.BlockSpec((1, 8, tn), lambda l, j: (l, 0, j)),
        out_shape=jax.ShapeDtypeStruct((L, 8, N), F32),
        compiler_params=_cparams(("parallel", "parallel")),
        name="modvec",
    )(cvec, mod_w, mod_b.reshape(L, 1, N))


def _ffn_kernel(h_ref, hc_ref, mod_ref, g_ref, w1_ref, w3_ref, w2_ref, fg_ref, o_ref, hm_ref, acc_ref, *,
                final_norm, n_lat_tiles):
    j = pl.program_id(1)

    def load_h():
        if n_lat_tiles is None:
            return h_ref[...]
        return jnp.where(pl.program_id(0) < n_lat_tiles, h_ref[...], hc_ref[...])

    @pl.when(j == 0)
    def _():
        y = _rms(load_h()) * g_ref[...]
        hm = y * (1.0 + mod_ref[0, 1:2, :]) + mod_ref[0, 0:1, :]
        hm_ref[...] = hm.astype(BF16)
        acc_ref[...] = jnp.zeros_like(acc_ref)

    hm = hm_ref[...]
    a = _dot(hm, w1_ref[...])
    b = _dot(hm, w3_ref[...])
    gg = (a * _sigmoid(a)) * b
    acc_ref[...] += _dot(gg.astype(BF16), w2_ref[...])

    @pl.when(j == pl.num_programs(1) - 1)
    def _():
        out = load_h() + (0.5 * mod_ref[0, 2:3, :]) * acc_ref[...]
        if final_norm:
            out = _rms(out) * fg_ref[...]
        o_ref[...] = out


def _ffn_call(h, hc, mod, g, w1, w3, w2, fg, *, n_rows, seg_rows, n_seg, tm, tf, final_norm):
    D = h.shape[1]
    FF = w1.shape[1]
    seg = lambda i, j: (jnp.minimum(i * tm // seg_rows, n_seg), 0, 0)
    if hc is None:
        n_lat_tiles = None
        h_specs = [pl.BlockSpec((tm, D), lambda i, j: (i, 0)), pl.BlockSpec((tm, D), lambda i, j: (0, 0))]
        hc = h
    else:
        n_lat_tiles = h.shape[0] // tm
        h_specs = [pl.BlockSpec((tm, D), lambda i, j: (jnp.minimum(i, n_lat_tiles - 1), 0)),
                   pl.BlockSpec((tm, D), lambda i, j: (jnp.maximum(i - n_lat_tiles, 0), 0))]
    return pl.pallas_call(
        functools.partial(_ffn_kernel, final_norm=final_norm, n_lat_tiles=n_lat_tiles),
        grid=(n_rows // tm, FF // tf),
        in_specs=h_specs + [
            pl.BlockSpec((1, 8, D), seg),
            pl.BlockSpec((1, D), lambda i, j: (0, 0)),
            pl.BlockSpec((D, tf), lambda i, j: (0, j)),
            pl.BlockSpec((D, tf), lambda i, j: (0, j)),
            pl.BlockSpec((tf, D), lambda i, j: (j, 0)),
            pl.BlockSpec((1, D), lambda i, j: (0, 0)),
        ],
        out_specs=pl.BlockSpec((tm, D), lambda i, j: (i, 0)),
        out_shape=jax.ShapeDtypeStruct((n_rows, D), F32),
        scratch_shapes=[pltpu.VMEM((tm, D), BF16), pltpu.VMEM((tm, D), F32)],
        compiler_params=_cparams(("parallel", "arbitrary")),
        name="ffn",
    )(h, hc, mod, g, w1, w3, w2, fg)


def _inproj_kernel(h_ref, mod_ref, g_ref, w_ref, o_ref, hm_ref):
    @pl.when(pl.program_id(1) == 0)
    def _():
        y = _rms(h_ref[...]) * g_ref[...]
        hm_ref[...] = (y * (1.0 + mod_ref[0, 1:2, :]) + mod_ref[0, 0:1, :]).astype(BF16)

    o_ref[...] = _dot(hm_ref[...], w_ref[...])


def _inproj_call(h, mod, g, w, *, seg_rows, n_seg, tm, tn):
    R, D = h.shape
    N = w.shape[1]
    seg = lambda i, j: (jnp.minimum(i * tm // seg_rows, n_seg), 0, 0)
    return pl.pallas_call(
        _inproj_kernel,
        grid=(R // tm, N // tn),
        in_specs=[
            pl.BlockSpec((tm, D), lambda i, j: (i, 0)),
            pl.BlockSpec((1, 8, D), seg),
            pl.BlockSpec((1, D), lambda i, j: (0, 0)),
            pl.BlockSpec((D, tn), lambda i, j: (0, j)),
        ],
        out_specs=pl.BlockSpec((tm, tn), lambda i, j: (i, j)),
        out_shape=jax.ShapeDtypeStruct((R, N), F32),
        scratch_shapes=[pltpu.VMEM((tm, D), BF16)],
        compiler_params=_cparams(("parallel", "arbitrary")),
        name="inproj",
    )(h, mod, g, w)


def _outproj_kernel(xa_ref, xb_ref, wa_ref, wb_ref, h_ref, mod_ref, o_ref):
    y = _dot(xa_ref[...], wa_ref[...]) + _dot(xb_ref[...], wb_ref[...])
    o_ref[...] = h_ref[...] + mod_ref[0, 0:1, :] * y


def _outproj_call(xa, xb, wa, wb, h, mod, *, n_rows, seg_rows, n_seg, tm):
    D = h.shape[1]
    Ka, Kb = xa.shape[1], xb.shape[1]
    seg = lambda i: (jnp.minimum(i * tm // seg_rows, n_seg), 0, 0)
    return pl.pallas_call(
        _outproj_kernel,
        grid=(n_rows // tm,),
        in_specs=[
            pl.BlockSpec((tm, Ka), lambda i: (i, 0)),
            pl.BlockSpec((tm, Kb), lambda i: (i, 0)),
            pl.BlockSpec((Ka, D), lambda i: (0, 0)),
            pl.BlockSpec((Kb, D), lambda i: (0, 0)),
            pl.BlockSpec((tm, D), lambda i: (i, 0)),
            pl.BlockSpec((1, 8, D), seg),
        ],
        out_specs=pl.BlockSpec((tm, D), lambda i: (i, 0)),
        out_shape=jax.ShapeDtypeStruct((n_rows, D), F32),
        compiler_params=_cparams(("parallel",)),
        name="outproj",
    )(xa, xb, wa, wb, h, mod)


def _rope_tables(T, head_dim):
    half = head_dim // 2
    n_freq = head_dim // 4
    t = jnp.arange(T)
    r = (t // GRID_W).astype(F32)
    cidx = (t % GRID_W).astype(F32)
    inv = ROPE_THETA ** (-jnp.arange(n_freq, dtype=F32) / n_freq)
    ang = jnp.concatenate([r[:, None] * inv, cidx[:, None] * inv], axis=-1)
    cos, sin = jnp.cos(ang), jnp.sin(ang)
    cos_h = jnp.concatenate([cos, cos], axis=-1)
    sin_h = jnp.concatenate([-sin, sin], axis=-1)
    reps = LANE // head_dim
    return jnp.tile(cos_h, (1, reps)), jnp.tile(sin_h, (1, reps))


def _rope128(x, cos, sin):
    return x * cos + pltpu.roll(x, 64, 1) * sin


def _rope64(x, cos, sin):
    lane = lax.broadcasted_iota(jnp.int32, x.shape, 1)
    first = jnp.bitwise_and(lane, 63) < 32
    partner = jnp.where(first, pltpu.roll(x, 96, 1), pltpu.roll(x, 32, 1))
    return x * cos + partner * sin


def _attn_a_kernel(q_ref, k_ref, v_ref, kc_ref, vc_ref, cq_ref, sq_ref, ck_ref, sk_ref, gq_ref, gk_ref,
                   o_ref, kall, vall, *, tc, t, rk, kc):
    grp = A_HEADS // A_KV

    @pl.when(pl.program_id(2) == 0)
    def _():
        gk = gk_ref[...]
        kall[0:tc, :] = (_rms(kc_ref[...]) * gk).astype(BF16)
        vall[0:tc, :] = vc_ref[...].astype(BF16)

        def body(c, carry):
            rows = pl.ds(pl.multiple_of(c * rk, rk), rk)
            kn = _rms(k_ref[rows, :]) * gk
            dst = pl.ds(pl.multiple_of(tc + c * rk, math.gcd(tc, rk)), rk)
            kall[dst, :] = _rope128(kn, ck_ref[rows, :], sk_ref[rows, :]).astype(BF16)
            vall[dst, :] = v_ref[rows, :].astype(BF16)
            return carry

        lax.fori_loop(0, t // rk, body, 0)

    q4 = q_ref[...]
    qs = jnp.concatenate([q4[:, h * A_DH:(h + 1) * A_DH] for h in range(grp)], axis=0)
    qn = _rms(qs) * gq_ref[...]
    cos = jnp.concatenate([cq_ref[...]] * grp, axis=0)
    sin = jnp.concatenate([sq_ref[...]] * grp, axis=0)
    qr = (_rope128(qn, cos, sin) * (A_DH ** -0.5 * LOG2E)).astype(BF16)
    m = l = acc = None
    for k0, kn in [(0, tc)] + [(tc + c * kc, kc) for c in range(t // kc)]:
        s = _dot_nt(qr, kall[k0:k0 + kn, :])
        smax = jnp.max(s, axis=-1, keepdims=True)
        if m is None:
            m = smax
            p = jnp.exp2(s - m)
            l = jnp.sum(p, axis=-1, keepdims=True)
            acc = _dot(p.astype(BF16), vall[k0:k0 + kn, :])
        else:
            m_new = jnp.maximum(m, smax)
            alpha = jnp.exp2(m - m_new)
            p = jnp.exp2(s - m_new)
            l = alpha * l + jnp.sum(p, axis=-1, keepdims=True)
            acc = alpha * acc + _dot(p.astype(BF16), vall[k0:k0 + kn, :])
            m = m_new
    o = acc / l
    tq = q4.shape[0]
    for h in range(grp):
        o_ref[:, h * A_DH:(h + 1) * A_DH] = o[h * tq:(h + 1) * tq, :].astype(o_ref.dtype)


def _attn_a_call(z, cos, sin, gq, gk, *, B, T, Tc):
    tq = Q_BLOCK
    nq = T // tq
    grp_w = (A_HEADS // A_KV) * A_DH
    kcol = A_HEADS
    cbase = B * T // Tc
    rk = min(512, T)
    return pl.pallas_call(
        functools.partial(_attn_a_kernel, tc=Tc, t=T, rk=rk, kc=min(2048, T)),
        grid=(B, A_KV, nq),
        in_specs=[
            pl.BlockSpec((tq, grp_w), lambda b, kv, i: (b * nq + i, kv)),
            pl.BlockSpec((T, A_DH), lambda b, kv, i: (b, kcol + kv)),
            pl.BlockSpec((T, A_DH), lambda b, kv, i: (b, kcol + A_KV + kv)),
            pl.BlockSpec((Tc, A_DH), lambda b, kv, i: (cbase + b, kcol + kv)),
            pl.BlockSpec((Tc, A_DH), lambda b, kv, i: (cbase + b, kcol + A_KV + kv)),
            pl.BlockSpec((tq, LANE), lambda b, kv, i: (i, 0)),
            pl.BlockSpec((tq, LANE), lambda b, kv, i: (i, 0)),
            pl.BlockSpec((T, LANE), lambda b, kv, i: (0, 0)),
            pl.BlockSpec((T, LANE), lambda b, kv, i: (0, 0)),
            pl.BlockSpec((1, A_DH), lambda b, kv, i: (0, 0)),
            pl.BlockSpec((1, A_DH), lambda b, kv, i: (0, 0)),
        ],
        out_specs=pl.BlockSpec((tq, grp_w), lambda b, kv, i: (b * nq + i, kv)),
        out_shape=jax.ShapeDtypeStruct((B * T, A_HEADS * A_DH), BF16),
        scratch_shapes=[pltpu.VMEM((Tc + T, A_DH), BF16), pltpu.VMEM((Tc + T, A_DH), BF16)],
        compiler_params=_cparams(("parallel", "parallel", "arbitrary")),
        name="attn_a",
    )(z, z, z, z, z, cos, sin, cos, sin, gq, gk)


def _attn_a_ctx_kernel(q_ref, kc_ref, vc_ref, gq_ref, gk_ref, o_ref):
    grp = A_HEADS // A_KV
    q4 = q_ref[...]
    tq = q4.shape[0]
    qs = jnp.concatenate([q4[:, h * A_DH:(h + 1) * A_DH] for h in range(grp)], axis=0)
    qn = (_rms(qs) * gq_ref[...] * (A_DH ** -0.5)).astype(BF16)
    kc = (_rms(kc_ref[...]) * gk_ref[...]).astype(BF16)
    s = _dot_nt(qn, kc)
    p = jnp.exp(s - jnp.max(s, axis=-1, keepdims=True))
    l = jnp.sum(p, axis=-1, keepdims=True)
    o = _dot(p.astype(BF16), vc_ref[...].astype(BF16)) / l
    for h in range(grp):
        o_ref[:, h * A_DH:(h + 1) * A_DH] = o[h * tq:(h + 1) * tq, :].astype(o_ref.dtype)


def _attn_a_ctx_call(z, gq, gk, *, B, T, Tc):
    grp_w = (A_HEADS // A_KV) * A_DH
    kcol = A_HEADS
    cbase = B * T // Tc
    return pl.pallas_call(
        _attn_a_ctx_kernel,
        grid=(B, A_KV),
        in_specs=[
            pl.BlockSpec((Tc, grp_w), lambda b, kv: (cbase + b, kv)),
            pl.BlockSpec((Tc, A_DH), lambda b, kv: (cbase + b, kcol + kv)),
            pl.BlockSpec((Tc, A_DH), lambda b, kv: (cbase + b, kcol + A_KV + kv)),
            pl.BlockSpec((1, A_DH), lambda b, kv: (0, 0)),
            pl.BlockSpec((1, A_DH), lambda b, kv: (0, 0)),
        ],
        out_specs=pl.BlockSpec((Tc, grp_w), lambda b, kv: (b, kv)),
        out_shape=jax.ShapeDtypeStruct((B * Tc, A_HEADS * A_DH), BF16),
        compiler_params=_cparams(("parallel", "parallel")),
        name="attn_a_ctx",
    )(z, z, z, gq, gk)


MLSTM_CAUG = B_DV + 8


def _mlstm_local_kernel(q_ref, k_ref, v0_ref, v1_ref, gc_ref, gr_ref, bcol_ref, brow_ref,
                        nloc_ref, stat_ref, scal_ref, uaug_ref, *, cb):
    L = B_CHUNK
    row = lax.broadcasted_iota(jnp.int32, (L, L), 0)
    col = lax.broadcasted_iota(jnp.int32, (L, L), 1)
    gcol = (gc_ref[...] + bcol_ref[...]).reshape(cb, L, LANE)
    grow = gr_ref[...] + brow_ref[...][None]
    sub0 = lax.broadcasted_iota(jnp.int32, (cb, 8, B_DQK), 1) == 0
    for hd in range(B_HEADS):
        q = q_ref[:, hd * B_DQK:(hd + 1) * B_DQK].reshape(cb, L, B_DQK)
        k = k_ref[:, hd * B_DQK:(hd + 1) * B_DQK].reshape(cb, L, B_DQK)
        v_ref = v0_ref if hd < 2 else v1_ref
        v = v_ref[:, (hd % 2) * B_DV:(hd % 2 + 1) * B_DV].reshape(cb, L, B_DV)
        qb = (q * (B_DQK ** -0.5)).astype(BF16)
        kb = k.astype(BF16)
        vb = v.astype(BF16)
        qk = jnp.einsum('cld,csd->cls', qb, kb, preferred_element_type=F32)
        for d in range(2):
            gi = 2 * d * B_HEADS + hd
            gf = gi + B_HEADS
            li_col = gcol[:, :, gi:gi + 1]
            lf_col = _log_sigmoid(gcol[:, :, gf:gf + 1])
            li_row = grow[:, gi:gi + 1, :]
            lf_row = _log_sigmoid(grow[:, gf:gf + 1, :])
            mask = ((col <= row) if d == 0 else (col >= row))[None]
            mask_t = ((row <= col) if d == 0 else (row >= col))[None]
            bcum_col = jnp.sum(jnp.where(mask, lf_row, 0.0), axis=2, keepdims=True)
            bcum_row = jnp.sum(jnp.where(mask_t, lf_col, 0.0), axis=1, keepdims=True)
            log_d = jnp.where(mask, bcum_col - bcum_row + li_row, -jnp.inf)
            a = jnp.max(log_d, axis=2, keepdims=True)
            s = qk * jnp.exp(log_d - a)
            nloc = jnp.einsum('cls,csv->clv', s.astype(BF16), vb, preferred_element_type=F32)
            dloc = jnp.sum(s, axis=2, keepdims=True)
            b_last = jnp.sum(lf_row, axis=2, keepdims=True)
            log_w = b_last - bcum_col + li_col
            b_loc = jnp.max(log_w, axis=1, keepdims=True)
            w = jnp.exp(log_w - b_loc)
            u = jnp.einsum('csv,csd->cvd', (w * v).astype(BF16), kb, preferred_element_type=F32)
            nu = jnp.sum(w * kb.astype(F32), axis=1, keepdims=True)
            nloc_ref[d, :, hd * B_DV:(hd + 1) * B_DV] = nloc.reshape(cb * L, B_DV)
            stat_ref[d, :, hd:hd + 1] = (a - bcum_col).reshape(cb * L, 1)
            stat_ref[d, :, B_HEADS + hd:B_HEADS + hd + 1] = dloc.reshape(cb * L, 1)
            stat_ref[d, :, 2 * B_HEADS + hd:2 * B_HEADS + hd + 1] = bcum_col.reshape(cb * L, 1)
            scal_ref[d, :, hd:hd + 1, :] = jnp.broadcast_to(b_last, (cb, 1, LANE))
            scal_ref[d, :, B_HEADS + hd:B_HEADS + hd + 1, :] = jnp.broadcast_to(b_loc, (cb, 1, LANE))
            uaug_ref[d, :, hd, 0:B_DV, :] = u
            uaug_ref[d, :, hd, B_DV:MLSTM_CAUG, :] = jnp.where(sub0, nu, 0.0)


def _mlstm_local_call(z, grow, bcol, brow, *, cb):
    L = B_CHUNK
    R = z.shape[0]
    nchunk = R // L
    W = B_HEADS * B_DV
    return pl.pallas_call(
        functools.partial(_mlstm_local_kernel, cb=cb),
        grid=(nchunk // cb,),
        in_specs=[
            pl.BlockSpec((cb * L, 512), lambda i: (i, 3)),
            pl.BlockSpec((cb * L, 512), lambda i: (i, 4)),
            pl.BlockSpec((cb * L, 512), lambda i: (i, 5)),
            pl.BlockSpec((cb * L, 512), lambda i: (i, 6)),
            pl.BlockSpec((cb * L, LANE), lambda i: (i, AB_GATE_COL // LANE)),
            pl.BlockSpec((cb, 16, L), lambda i: (i, 0, 0)),
            pl.BlockSpec((1, LANE), lambda i: (0, 0)),
            pl.BlockSpec((16, 1), lambda i: (0, 0)),
        ],
        out_specs=[
            pl.BlockSpec((2, cb * L, W), lambda i: (0, i, 0)),
            pl.BlockSpec((2, cb * L, 16), lambda i: (0, i, 0)),
            pl.BlockSpec((2, cb, 8, LANE), lambda i: (0, i, 0, 0)),
            pl.BlockSpec((2, cb, B_HEADS, MLSTM_CAUG, B_DQK), lambda i: (0, i, 0, 0, 0)),
        ],
        out_shape=[
            jax.ShapeDtypeStruct((2, R, W), F32),
            jax.ShapeDtypeStruct((2, R, 16), F32),
            jax.ShapeDtypeStruct((2, nchunk, 8, LANE), F32),
            jax.ShapeDtypeStruct((2, nchunk, B_HEADS, MLSTM_CAUG, B_DQK), F32),
        ],
        compiler_params=_cparams(("parallel",)),
        name="mlstm_local",
    )(z, z, z, z, z, grow, bcol, brow)


def _mlstm_scan_kernel(qf_ref, nf_ref, stf_ref, scf_ref, uf_ref, qb_ref, nb_ref, stb_ref, scb_ref, ub_ref,
                       of_ref, ob_ref, c_s, m_s):
    @pl.when(pl.program_id(1) == 0)
    def _():
        c_s[...] = jnp.zeros_like(c_s)
        m_s[...] = jnp.zeros_like(m_s)

    L = B_CHUNK
    units = []
    for d, (q_ref, n_ref, st_ref, sc_ref, u_ref, o_ref) in enumerate((
            (qf_ref, nf_ref, stf_ref, scf_ref, uf_ref, of_ref),
            (qb_ref, nb_ref, stb_ref, scb_ref, ub_ref, ob_ref))):
        st = st_ref[0]
        for hd in range(B_HEADS):
            u = d * B_HEADS + hd
            units.append((u, hd, q_ref, n_ref, st, sc_ref, u_ref, o_ref, c_s[u], m_s[u, 0:1, :]))
    new_state = []
    for u, hd, q_ref, n_ref, st, sc_ref, u_ref, o_ref, c_prev, m_row in units:
        rep = lambda j: jnp.broadcast_to(st[:, j:j + 1], (L, LANE))
        g, dloc, bcum = rep(hd), rep(B_HEADS + hd), rep(2 * B_HEADS + hd)
        m = jnp.broadcast_to(m_row, (L, LANE))
        e_intra = jnp.exp(jnp.minimum(g - m, 0.0))
        w_inter = jnp.exp(jnp.minimum(m - g, 0.0))
        floor = jnp.exp(-(bcum + jnp.maximum(m, g)))
        qb = (q_ref[:, hd * B_DQK:(hd + 1) * B_DQK] * (B_DQK ** -0.5)).astype(BF16)
        inter = _dot_nt(qb, c_prev.astype(BF16))
        den = e_intra * dloc + w_inter * inter[:, B_DV:B_DV + LANE]
        r = 1.0 / jnp.maximum(jnp.abs(den), floor)
        for j in range(B_DV // LANE):
            c0 = hd * B_DV + j * LANE
            num = e_intra * n_ref[0, :, c0:c0 + LANE] + w_inter * inter[:, j * LANE:(j + 1) * LANE]
            o_ref[0, :, c0:c0 + LANE] = num * r
        b_last = sc_ref[0, 0, hd:hd + 1, :]
        b_loc = sc_ref[0, 0, B_HEADS + hd:B_HEADS + hd + 1, :]
        m_new = jnp.maximum(b_last + m_row, b_loc)
        decay = jnp.exp(b_last + m_row - m_new)
        scl = jnp.exp(b_loc - m_new)
        c_new = decay * c_prev[0:B_DV] + scl * u_ref[0, 0, hd, 0:B_DV, :]
        n_new = decay * c_prev[B_DV:] + scl * jnp.broadcast_to(u_ref[0, 0, hd, B_DV:B_DV + 1, :], (LANE, B_DQK))
        new_state.append((u, c_new, n_new, m_new))
    for u, c_new, n_new, m_new in new_state:
        c_s[u, 0:B_DV, :] = c_new
        c_s[u, B_DV:, :] = n_new
        m_s[u] = jnp.broadcast_to(m_new, m_s.shape[1:])


def _mlstm_scan_call(z, nloc, stat, scal, uaug, *, B, T, Tc):
    L = B_CHUNK
    ncl, ncc = T // L, Tc // L
    nc = ncl + ncc
    cbase = B * T // L
    R = z.shape[0]
    W = B_HEADS * B_DV

    def blk(b, c, d):
        is_ctx = c < ncc
        jc = c if d == 0 else ncc - 1 - c
        jl = (c - ncc) if d == 0 else (ncl - 1 - (c - ncc))
        return jnp.where(is_ctx, cbase + b * ncc + jc, b * ncl + jl)

    def specs(d):
        return [
            pl.BlockSpec((L, 512), lambda b, c: (blk(b, c, d), 3)),
            pl.BlockSpec((1, L, W), lambda b, c: (d, blk(b, c, d), 0)),
            pl.BlockSpec((1, L, 16), lambda b, c: (d, blk(b, c, d), 0)),
            pl.BlockSpec((1, 1, 8, LANE), lambda b, c: (d, blk(b, c, d), 0, 0)),
            pl.BlockSpec((1, 1, B_HEADS, MLSTM_CAUG, B_DQK), lambda b, c: (d, blk(b, c, d), 0, 0, 0)),
        ]

    out_sds = jax.ShapeDtypeStruct((1, R, W), F32)
    return pl.pallas_call(
        _mlstm_scan_kernel,
        grid=(B, nc),
        in_specs=specs(0) + specs(1),
        out_specs=[
            pl.BlockSpec((1, L, W), lambda b, c: (0, blk(b, c, 0), 0)),
            pl.BlockSpec((1, L, W), lambda b, c: (0, blk(b, c, 1), 0)),
        ],
        out_shape=[out_sds, out_sds],
        scratch_shapes=[
            pltpu.VMEM((2 * B_HEADS, B_DV + LANE, B_DQK), F32),
            pltpu.VMEM((2 * B_HEADS, 8, LANE), F32),
        ],
        compiler_params=_cparams(("parallel", "arbitrary")),
        name="mlstm_scan",
    )(z, nloc, stat, scal, uaug, z, nloc, stat, scal, uaug)


def _mlstm_finish_kernel(hf_ref, hb_ref, o0_ref, o1_ref, hg_ref, y_ref):
    for hd in range(B_HEADS):
        sl = slice(hd * B_DV, (hd + 1) * B_DV)
        hh = hf_ref[0, :, sl] + hb_ref[0, :, sl]
        o_ref = o0_ref if hd < 2 else o1_ref
        o = o_ref[:, (hd % 2) * B_DV:(hd % 2 + 1) * B_DV]
        y_ref[:, sl] = (_sigmoid(o) * _rms(hh) * hg_ref[:, sl]).astype(y_ref.dtype)


def _mlstm_finish_call(hf, hb, z, head_g, *, tm):
    R = z.shape[0]
    W = B_HEADS * B_DV
    return pl.pallas_call(
        _mlstm_finish_kernel,
        grid=(R // tm,),
        in_specs=[
            pl.BlockSpec((1, tm, W), lambda i: (0, i, 0)),
            pl.BlockSpec((1, tm, W), lambda i: (0, i, 0)),
            pl.BlockSpec((tm, 512), lambda i: (i, 7)),
            pl.BlockSpec((tm, 512), lambda i: (i, 8)),
            pl.BlockSpec((1, W), lambda i: (0, 0)),
        ],
        out_specs=pl.BlockSpec((tm, W), lambda i: (i, 0)),
        out_shape=jax.ShapeDtypeStruct((R, W), BF16),
        compiler_params=_cparams(("parallel",)),
        name="mlstm_finish",
    )(hf, hb, z, z, head_g)


def _s5_params(a_re, a_im, log_dt, b_re, b_im, c_re, c_im, reverse):
    hp = lax.Precision.HIGHEST
    Lc = S5_CHUNK
    dt = jnp.exp(log_dt)[:, None]
    lam_re, lam_im = a_re * dt, a_im * dt
    mag = jnp.exp(lam_re)
    ab_re, ab_im = mag * jnp.cos(lam_im), mag * jnp.sin(lam_im)
    den = a_re * a_re + a_im * a_im
    nr, ni = ab_re - 1.0, ab_im
    k_re, k_im = (nr * a_re + ni * a_im) / den, (ni * a_re - nr * a_im) / den
    bb_re = k_re[..., None] * b_re - k_im[..., None] * b_im
    bb_im = k_re[..., None] * b_im + k_im[..., None] * b_re
    tau = jnp.arange(Lc + 1, dtype=F32)[:, None, None]
    pm = jnp.exp(tau * lam_re)
    pw_re, pw_im = pm * jnp.cos(tau * lam_im), pm * jnp.sin(tau * lam_im)
    cp_re = c_re[None] * pw_re[:, :, None, :] - c_im[None] * pw_im[:, :, None, :]
    cp_im = c_re[None] * pw_im[:, :, None, :] + c_im[None] * pw_re[:, :, None, :]
    kk = (jnp.einsum('tgop,gpi->tgoi', cp_re, bb_re, precision=hp)
          - jnp.einsum('tgop,gpi->tgoi', cp_im, bb_im, precision=hp))
    G = a_re.shape[0]
    i_idx = jnp.arange(Lc)[:, None]
    j_idx = jnp.arange(Lc)[None, :]
    lag = (i_idx - j_idx) if reverse else (j_idx - i_idx)
    valid = lag >= 0
    kt = jnp.where(valid[:, :, None, None, None], kk[jnp.clip(lag, 0, Lc)], 0.0)
    kt = kt.transpose(2, 0, 4, 1, 3).reshape(G, Lc * C_GROUP, Lc * C_GROUP)
    e = jnp.arange(Lc) if reverse else (Lc - 1 - jnp.arange(Lc))
    w_re = pw_re[e][:, :, :, None] * bb_re[None] - pw_im[e][:, :, :, None] * bb_im[None]
    w_im = pw_re[e][:, :, :, None] * bb_im[None] + pw_im[e][:, :, :, None] * bb_re[None]
    w_re = w_re.transpose(1, 0, 3, 2).reshape(G, Lc * C_GROUP, C_STATE)
    w_im = w_im.transpose(1, 0, 3, 2).reshape(G, Lc * C_GROUP, C_STATE)
    wp = jnp.concatenate([w_re, w_im], axis=-1)
    wps = jnp.concatenate([w_im, w_re], axis=-1)
    f = (Lc - jnp.arange(Lc)) if reverse else (jnp.arange(Lc) + 1)
    cj_re, cj_im = cp_re[f], cp_im[f]
    cp = jnp.concatenate([cj_re.transpose(1, 3, 0, 2).reshape(G, C_STATE, Lc * C_GROUP),
                          -cj_im.transpose(1, 3, 0, 2).reshape(G, C_STATE, Lc * C_GROUP)], axis=1)
    a8_re, a8_im = pw_re[Lc], pw_im[Lc]
    a1 = jnp.concatenate([a8_re, a8_re], axis=-1)[:, None, :]
    a2 = jnp.concatenate([-a8_im, a8_im], axis=-1)[:, None, :]
    return kt.astype(BF16), wp.astype(BF16), wps.astype(BF16), cp.astype(BF16), a1, a2


def _s5_kernel(u_ref, kt_ref, wp_ref, wps_ref, cp_ref, a1_ref, a2_ref, y_ref, w_s, ws_s, xp_s, *, nb, ncc, ncl):
    nc = ncc + ncl
    for d in range(2):
        for g in range(S5_GB):
            ug = u_ref[g]
            w_s[g] = _dot(ug, wp_ref[d, g])
            ws_s[g] = _dot(ug, wps_ref[d, g])
        a1 = [jnp.broadcast_to(a1_ref[d, g], (nb, LANE)) for g in range(S5_GB)]
        a2 = [jnp.broadcast_to(a2_ref[d, g], (nb, LANE)) for g in range(S5_GB)]

        def body(s, carry):
            if d == 0:
                c = s
            else:
                c = jnp.where(s < ncc, ncc - 1 - s, nc + ncc - 1 - s)
            rows = pl.ds(pl.multiple_of(c * nb, nb), nb)
            new = []
            for g in range(S5_GB):
                x, xs = carry[2 * g], carry[2 * g + 1]
                xp_s[g, rows, :] = x
                new.append(a1[g] * x + a2[g] * xs + w_s[g, rows, :])
                new.append(a1[g] * xs - a2[g] * x + ws_s[g, rows, :])
            return tuple(new)

        init = tuple(jnp.zeros((nb, LANE), F32) for _ in range(2 * S5_GB))
        lax.fori_loop(0, nc, body, init)
        for g in range(S5_GB):
            y = _dot(u_ref[g], kt_ref[d, g]) + _dot(xp_s[g].astype(BF16), cp_ref[d, g])
            if d == 0:
                y_ref[g] = y
            else:
                y_ref[g] += y


def _s5_call(ug, kt, wp, wps, cp, a1, a2, *, nb, ncc, ncl):
    G, R, _ = ug.shape
    wspec = pl.BlockSpec((2, S5_GB, LANE, LANE), lambda i: (0, i, 0, 0))
    aspec = pl.BlockSpec((2, S5_GB, 1, LANE), lambda i: (0, i, 0, 0))
    return pl.pallas_call(
        functools.partial(_s5_kernel, nb=nb, ncc=ncc, ncl=ncl),
        grid=(G // S5_GB,),
        in_specs=[pl.BlockSpec((S5_GB, R, LANE), lambda i: (i, 0, 0)), wspec, wspec, wspec, wspec, aspec, aspec],
        out_specs=pl.BlockSpec((S5_GB, R, LANE), lambda i: (i, 0, 0)),
        out_shape=jax.ShapeDtypeStruct((G, R, LANE), F32),
        scratch_shapes=[pltpu.VMEM((S5_GB, R, LANE), F32)] * 3,
        compiler_params=_cparams(("parallel",)),
        name="s5_scan",
    )(ug, kt, wp, wps, cp, a1, a2)


def _s5_finish_kernel(y_ref, u_ref, d_ref, w_ref, b_ref, o_ref):
    y = y_ref[...] + d_ref[...] * u_ref[...]
    g = 0.5 * y * (1.0 + jnp.tanh(math.sqrt(2.0 / math.pi) * (y + 0.044715 * (y * y * y))))
    o_ref[...] = (g * _sigmoid(_dot(g.astype(BF16), w_ref[...]) + b_ref[...])).astype(o_ref.dtype)


def _s5_finish_call(y, z, d_skip, glu_w, glu_b, *, n_rows, tm):
    W = C_WIDTH
    return pl.pallas_call(
        _s5_finish_kernel,
        grid=(n_rows // tm,),
        in_specs=[
            pl.BlockSpec((tm, W), lambda i: (i, 0)),
            pl.BlockSpec((tm, W), lambda i: (i, 0)),
            pl.BlockSpec((1, W), lambda i: (0, 0)),
            pl.BlockSpec((W, W), lambda i: (0, 0)),
            pl.BlockSpec((1, W), lambda i: (0, 0)),
        ],
        out_specs=pl.BlockSpec((tm, W), lambda i: (i, 0)),
        out_shape=jax.ShapeDtypeStruct((n_rows, W), BF16),
        compiler_params=_cparams(("parallel",)),
        name="s5_finish",
    )(y, z, d_skip, glu_w, glu_b)


def _ones_lane(par):
    return D_DH if par == 0 else 0


def _attn_d_kernel(q_ref, k_ref, v_ref, kc_ref, vc_ref, cq_ref, sq_ref, ck_ref, sk_ref, sink_ref,
                   o_ref, kpe, kpo, vpe, vpo, kce, kco, vce, vco, band, *, t, tc, rk):
    W = D_WINDOW
    kv = pl.program_id(1)
    i = pl.program_id(2)
    lane = lax.broadcasted_iota(jnp.int32, (1, LANE), 1)
    lo = lane < D_DH

    def halves(x, ones=False):
        sel = jnp.where(kv == 0, x, pltpu.roll(x, 64, 1))
        e = jnp.where(lo, sel, 0.0)
        o = pltpu.roll(e, 64, 1)
        if ones:
            e = jnp.where(lane == _ones_lane(0), 1.0, e)
            o = jnp.where(lane == _ones_lane(1), 1.0, o)
        return e.astype(BF16), o.astype(BF16)

    @pl.when(i == 0)
    def _():
        zpad = jnp.zeros((W, LANE), BF16)
        for ref in (kpe, kpo, vpe, vpo):
            ref[0:W, :] = zpad
            ref[W + t:W + t + W, :] = zpad
        kce[...], kco[...] = halves(kc_ref[...])
        vce[...], vco[...] = halves(vc_ref[...], ones=True)

        def body(c, carry):
            rows = pl.ds(pl.multiple_of(c * rk, rk), rk)
            dst = pl.ds(pl.multiple_of(W + c * rk, math.gcd(W, rk)), rk)
            kpe[dst, :], kpo[dst, :] = halves(_rope64(k_ref[rows, :], ck_ref[rows, :], sk_ref[rows, :]))
            vpe[dst, :], vpo[dst, :] = halves(v_ref[rows, :], ones=True)
            return carry

        lax.fori_loop(0, t // rk, body, 0)

    tq = q_ref.shape[0]
    nslab = q_ref.shape[1] // LANE
    span = tq + 2 * W

    @pl.when(i == 0)
    def _():
        rr = jnp.bitwise_and(lax.broadcasted_iota(jnp.int32, (nslab * tq, span), 0), tq - 1)
        cc = lax.broadcasted_iota(jnp.int32, (nslab * tq, span), 1)
        band[...] = jnp.where(jnp.abs(rr + W - cc) <= W, 0.0, -jnp.inf)

    cos, sin = cq_ref[...], sq_ref[...]
    qs = jnp.concatenate(
        [_rope64(q_ref[:, j * LANE:(j + 1) * LANE], cos, sin) for j in range(nslab)], axis=0)
    qs = (qs * (D_DH ** -0.5 * LOG2E)).astype(BF16)
    rows_k = pl.ds(pl.multiple_of(i * tq, tq), span)
    kpos = i * tq - W + lax.broadcasted_iota(jnp.int32, (1, span), 1)
    bias = band[...] + jnp.where((kpos >= 0) & (kpos < t), 0.0, -jnp.inf)
    out = None
    lane_o = lax.broadcasted_iota(jnp.int32, (1, LANE), 1)
    for par, (kp, vp, kcx, vcx) in enumerate(((kpe, vpe, kce, vce), (kpo, vpo, kco, vco))):
        s = jnp.concatenate([_dot_nt(qs, kcx[...]), _dot_nt(qs, kp[rows_k, :]) + bias], axis=1)
        snk = LOG2E * jnp.concatenate(
            [jnp.broadcast_to(sink_ref[0, 0:1, 2 * j + par:2 * j + par + 1], (tq, 1)) for j in range(nslab)],
            axis=0)
        m = jnp.maximum(jnp.max(s, axis=-1, keepdims=True), snk)
        p = jnp.exp2(s - m).astype(BF16)
        o = _dot(p[:, 0:tc], vcx[...]) + _dot(p[:, tc:], vp[rows_k, :])
        one = _ones_lane(par)
        l = o[:, one:one + 1] + jnp.exp2(snk - m)
        own = (lane_o < D_DH) if par == 0 else (lane_o >= D_DH)
        o = jnp.where(own, o, 0.0) / l
        out = o if out is None else out + o
    for j in range(nslab):
        o_ref[:, j * LANE:(j + 1) * LANE] = out[j * tq:(j + 1) * tq, :].astype(o_ref.dtype)


def _attn_d_call(z, cos, sin, sink, *, B, T, Tc):
    tq = min(2 * Q_BLOCK, T)
    nq = T // tq
    grp_w = (D_HEADS // D_KV) * D_DH
    qcol = C_WIDTH // grp_w
    kcol = (C_WIDTH + D_HEADS * D_DH) // LANE
    cbase = B * T // Tc
    rk = min(512, T)
    pad = lambda n: pltpu.VMEM((n, LANE), BF16)
    return pl.pallas_call(
        functools.partial(_attn_d_kernel, t=T, tc=Tc, rk=rk),
        grid=(B, D_KV, nq),
        in_specs=[
            pl.BlockSpec((tq, grp_w), lambda b, kv, i: (b * nq + i, qcol + kv)),
            pl.BlockSpec((T, LANE), lambda b, kv, i: (b, kcol)),
            pl.BlockSpec((T, LANE), lambda b, kv, i: (b, kcol + 1)),
            pl.BlockSpec((Tc, LANE), lambda b, kv, i: (cbase + b, kcol)),
            pl.BlockSpec((Tc, LANE), lambda b, kv, i: (cbase + b, kcol + 1)),
            pl.BlockSpec((tq, LANE), lambda b, kv, i: (i, 0)),
            pl.BlockSpec((tq, LANE), lambda b, kv, i: (i, 0)),
            pl.BlockSpec((T, LANE), lambda b, kv, i: (0, 0)),
            pl.BlockSpec((T, LANE), lambda b, kv, i: (0, 0)),
            pl.BlockSpec((1, 1, LANE), lambda b, kv, i: (kv, 0, 0)),
        ],
        out_specs=pl.BlockSpec((tq, grp_w), lambda b, kv, i: (b * nq + i, kv)),
        out_shape=jax.ShapeDtypeStruct((B * T, D_HEADS * D_DH), BF16),
        scratch_shapes=[pad(T + 2 * D_WINDOW)] * 4 + [pad(Tc)] * 4
        + [pltpu.VMEM((grp_w // LANE * tq, tq + 2 * D_WINDOW), F32)],
        compiler_params=_cparams(("parallel", "parallel", "arbitrary")),
        name="attn_d",
    )(z, z, z, z, z, cos, sin, cos, sin, sink)


def _mod_rows(ml, idx):
    sel = ml[:, jnp.array(idx), :]
    return jnp.pad(sel, ((0, 0), (0, 8 - len(idx)), (0, 0)))


def kernel(x, c, ctx, c_ctx, mod_w, mod_b, norm_g, ffn_w1, ffn_w3, ffn_w2, ab_w_in, ab_b_gate, ab_w_out, a_gq, a_gk, b_head_g, cd_w_in, cd_w_out, s5_a_re, s5_a_im, s5_log_dt, s5_b_re, s5_b_im, s5_c_re, s5_c_im, s5_d, s5_glu_w, s5_glu_b, d_sink, final_g):
    B, T, D = x.shape
    Tc = ctx.shape[1]
    depth = mod_w.shape[0]
    n_lat, n_ctx = B * T, B * Tc
    n_all = n_lat + n_ctx
    tm = math.gcd(math.gcd(T, n_ctx), 512)
    tf = math.gcd(ffn_w1.shape[-1], 512)
    seg = dict(seg_rows=T, n_seg=B)

    h, hc0 = x.reshape(n_lat, D), ctx.reshape(n_ctx, D)
    cvec = jnp.zeros((8, D), F32).at[:B].set(c).at[B].set(c_ctx)
    mods = _modvec_call(cvec, mod_w, mod_b).reshape(depth, 8, N_MOD, D)[:, :B + 1]
    cos_a, sin_a = _rope_tables(T, A_DH)
    cos_d, sin_d = _rope_tables(T, D_DH)
    w1, w3, w2 = ffn_w1.astype(BF16), ffn_w3.astype(BF16), ffn_w2.astype(BF16)
    ones = jnp.ones((1, D), F32)

    for l in range(depth):
        last = l == depth - 1
        want_ctx = not last
        ml = mods[l]
        i = l // 2
        h = _ffn_call(h, hc0 if l == 0 else None, _mod_rows(ml, (0, 1, 2)), norm_g[l, 0][None],
                      w1[l, 0], w3[l, 0], w2[l, 0], ones,
                      n_rows=n_all, tm=tm, tf=tf, final_norm=False, **seg)
        mod_in = _mod_rows(ml, (3, 4))
        mod_out = _mod_rows(ml, (5,))
        if l % 2 == 0:
            w_in = jnp.pad(ab_w_in[i], ((0, 0), (0, AB_IN_PAD - ab_w_in.shape[-1]))).astype(BF16)
            z = _inproj_call(h, mod_in, norm_g[l, 1][None], w_in, tm=tm, tn=1024, **seg)
            ya = _attn_a_call(z, cos_a, sin_a, a_gq[i][None], a_gk[i][None], B=B, T=T, Tc=Tc)
            yac = _attn_a_ctx_call(z, a_gq[i][None], a_gk[i][None], B=B, T=T, Tc=Tc)
            ng = 4 * B_HEADS
            gates = z[:, AB_GATE_COL:AB_GATE_COL + ng]
            grow = gates.reshape(n_all // B_CHUNK, B_CHUNK, ng).transpose(0, 2, 1)
            bcol = jnp.pad(ab_b_gate[i], (0, LANE - ng))[None]
            brow = ab_b_gate[i][:, None]
            nloc, stat, scal, uaug = _mlstm_local_call(z, grow, bcol, brow, cb=math.gcd(n_all // B_CHUNK, 8))
            hf, hb = _mlstm_scan_call(z, nloc, stat, scal, uaug, B=B, T=T, Tc=Tc)
            yb = _mlstm_finish_call(hf, hb, z, b_head_g[i][None], tm=tm)
            xa = jnp.concatenate([ya, yac], axis=0)
            w_out = ab_w_out[i].astype(BF16)
            ka = A_HEADS * A_DH
            n_rows = n_all
        else:
            w_in = cd_w_in[i].astype(BF16)
            z = _inproj_call(h, mod_in, norm_g[l, 1][None], w_in, tm=tm, tn=w_in.shape[1] // 3, **seg)
            Lc = S5_CHUNK
            ncl, ncc = T // Lc, Tc // Lc

            def fold(u, n):
                return (u.reshape(B, n, Lc, C_GROUPS, C_GROUP).transpose(3, 1, 0, 2, 4)
                        .reshape(C_GROUPS, n * B, Lc * C_GROUP))

            u_all = z[:, :C_WIDTH].astype(BF16)
            ug = jnp.concatenate([fold(u_all[n_lat:], ncc), fold(u_all[:n_lat], ncl)], axis=1)
            prm = [_s5_params(s5_a_re[i, d], s5_a_im[i, d], s5_log_dt[i, d], s5_b_re[i, d], s5_b_im[i, d],
                              s5_c_re[i, d], s5_c_im[i, d], reverse=(d == 1)) for d in range(2)]
            kt, wp, wps, cp, a1, a2 = [jnp.stack([prm[0][k], prm[1][k]]) for k in range(6)]
            yg = _s5_call(ug, kt, wp, wps, cp, a1, a2, nb=B, ncc=ncc, ncl=ncl)
            ys5 = (yg[:, ncc * B:].reshape(C_GROUPS, ncl, B, Lc, C_GROUP).transpose(2, 1, 3, 0, 4)
                   .reshape(n_lat, C_WIDTH))
            xa = _s5_finish_call(ys5, z, s5_d[i][None], s5_glu_w[i].astype(BF16), s5_glu_b[i][None],
                                 n_rows=n_lat, tm=tm)
            sink = jnp.pad(d_sink[i].reshape(D_KV, 1, D_HEADS // D_KV), ((0, 0), (0, 0), (0, LANE - D_HEADS // D_KV)))
            yb = _attn_d_call(z, cos_d, sin_d, sink, B=B, T=T, Tc=Tc)
            w_out = cd_w_out[i].astype(BF16)
            ka = C_WIDTH
            n_rows = n_all if want_ctx else n_lat
        if not want_ctx:
            n_rows = n_lat
        h = _outproj_call(xa, yb, w_out[:ka], w_out[ka:], h, mod_out, n_rows=n_rows, tm=tm, **seg)
        h = _ffn_call(h, None, _mod_rows(ml, (6, 7, 8)), norm_g[l, 2][None], w1[l, 1], w3[l, 1], w2[l, 1],
                      final_g[None] if last else ones, n_rows=n_rows, tm=tm, tf=tf, final_norm=last, **seg)
    return h[:n_lat].reshape(B, T, D)
```

```python
import functools
import math

import jax
import jax.numpy as jnp
from jax import lax
from jax.experimental import pallas as pl
from jax.experimental.pallas import tpu as pltpu

F32 = jnp.float32
BF16 = jnp.bfloat16

EPS = 1e-6
ROPE_THETA = 10000.0
GRID_W = 64
N_MOD = 9
Q_BLOCK = 128

A_HEADS, A_KV, A_DH = 8, 2, 128
B_HEADS, B_DQK, B_DV, B_CHUNK = 4, 128, 256, 64
C_WIDTH, C_GROUP, C_STATE = 1024, 16, 64
C_GROUPS = C_WIDTH // C_GROUP
D_HEADS, D_KV, D_DH, D_WINDOW = 16, 2, 64, 128

AB_GATE_COL = 4608
AB_IN_PAD = 5120
S5_CHUNK = 8
S5_GB = 8

LANE = 128
LOG2E = math.log2(math.e)
VMEM_LIMIT = 56 * 1024 * 1024


def _cparams(sem):
    return pltpu.CompilerParams(dimension_semantics=sem, vmem_limit_bytes=VMEM_LIMIT)


def _rms(x):
    return x * lax.rsqrt(jnp.mean(x * x, axis=-1, keepdims=True) + EPS)


def _sigmoid(x):
    return 1.0 / (1.0 + jnp.exp(-x))


def _log_sigmoid(x):
    return jnp.minimum(x, 0.0) - jnp.log(1.0 + jnp.exp(-jnp.abs(x)))


def _dot(a, b):
    return jnp.dot(a, b, preferred_element_type=F32)


def _dot_nt(a, b):
    return lax.dot_general(a, b, (((1,), (1,)), ((), ())), preferred_element_type=F32)


def _dot_tn(a, b):
    return lax.dot_general(a, b, (((0,), (0,)), ((), ())), preferred_element_type=F32)


def _modvec_kernel(c_ref, w_ref, b_ref, o_ref):
    c = c_ref[...]
    s = (c * _sigmoid(c)).astype(BF16)
    o_ref[0] = _dot(s, w_ref[0].astype(BF16)) + b_ref[0]


def _modvec_call(cvec, mod_w, mod_b):
    L, D, N = mod_w.shape
    tn = math.gcd(D, 1024)
    return pl.pallas_call(
        _modvec_kernel,
        grid=(L, N // tn),
        in_specs=[
            pl.BlockSpec((8, D), lambda l, j: (0, 0)),
            pl.BlockSpec((1, D, tn), lambda l, j: (l, 0, j)),
            pl.BlockSpec((1, 1, tn), lambda l, j: (l, 0, j)),
        ],
        out_specs=pl.BlockSpec((1, 8, tn), lambda l, j: (l, 0, j)),
        out_shape=jax.ShapeDtypeStruct((L, 8, N), F32),
        compiler_params=_cparams(("parallel", "parallel")),
        name="modvec",
    )(cvec, mod_w, mod_b.reshape(L, 1, N))


def _ffn_kernel(h_ref, hc_ref, mod_ref, g_ref, w1_ref, w3_ref, w2_ref, fg_ref, o_ref, hm_ref, acc_ref, *,
                final_norm, n_lat_tiles):
    j = pl.program_id(1)

    def load_h():
        if n_lat_tiles is None:
            return h_ref[...]
        return jnp.where(pl.program_id(0) < n_lat_tiles, h_ref[...], hc_ref[...])

    @pl.when(j == 0)
    def _():
        y = _rms(load_h()) * g_ref[...]
        hm = y * (1.0 + mod_ref[0, 1:2, :]) + mod_ref[0, 0:1, :]
        hm_ref[...] = hm.astype(BF16)
        acc_ref[...] = jnp.zeros_like(acc_ref)

    hm = hm_ref[...]
    a = _dot(hm, w1_ref[...])
    b = _dot(hm, w3_ref[...])
    gg = (a * _sigmoid(a)) * b
    acc_ref[...] += _dot(gg.astype(BF16), w2_ref[...])

    @pl.when(j == pl.num_programs(1) - 1)
    def _():
        out = load_h() + (0.5 * mod_ref[0, 2:3, :]) * acc_ref[...]
        if final_norm:
            out = _rms(out) * fg_ref[...]
        o_ref[...] = out


def _ffn_pipe_kernel(h_ref, hn_ref, mod_ref, modn_ref, g_ref, w1_ref, w3_ref, w2_ref, fg_ref, o_ref,
                     hm_a, hm_b, acc_ref, *, final_norm, rc):
    i, j = pl.program_id(0), pl.program_id(1)

    @pl.when((i == 0) & (j == 0))
    def _():
        hm_a[...] = _modulate(h_ref[...], g_ref[...], mod_ref[0])

    @pl.when(j == 0)
    def _():
        acc_ref[...] = jnp.zeros_like(acc_ref)

    def body(hm_cur, hm_next):
        _next_tile_chunk(hn_ref, modn_ref, g_ref, hm_next, j, rc)
        hm = hm_cur[...]
        a = _dot(hm, w1_ref[...])
        b = _dot(hm, w3_ref[...])
        gg = (a * _sigmoid(a)) * b
        acc_ref[...] += _dot(gg.astype(BF16), w2_ref[...])

    pl.when(i % 2 == 0)(lambda: body(hm_a, hm_b))
    pl.when(i % 2 == 1)(lambda: body(hm_b, hm_a))

    @pl.when(j == pl.num_programs(1) - 1)
    def _():
        out = h_ref[...] + (0.5 * mod_ref[0, 2:3, :]) * acc_ref[...]
        if final_norm:
            out = _rms(out) * fg_ref[...]
        o_ref[...] = out


def _ffn_pipe_call(h, mod, g, w1, w3, w2, fg, *, n_rows, seg_rows, n_seg, tm, tf, final_norm):
    D = h.shape[1]
    FF = w1.shape[1]
    ni, nj = n_rows // tm, FF // tf
    nxt = lambda i: jnp.minimum(i + 1, ni - 1)
    seg = lambda i: jnp.minimum(i * tm // seg_rows, n_seg)
    return pl.pallas_call(
        functools.partial(_ffn_pipe_kernel, final_norm=final_norm, rc=_chunk_rows(tm, nj)),
        grid=(ni, nj),
        in_specs=[
            pl.BlockSpec((tm, D), lambda i, j: (i, 0)),
            pl.BlockSpec((tm, D), lambda i, j: (nxt(i), 0)),
            pl.BlockSpec((1, 8, D), lambda i, j: (seg(i), 0, 0)),
            pl.BlockSpec((1, 8, D), lambda i, j: (seg(nxt(i)), 0, 0)),
            pl.BlockSpec((1, D), lambda i, j: (0, 0)),
            pl.BlockSpec((D, tf), lambda i, j: (0, j)),
            pl.BlockSpec((D, tf), lambda i, j: (0, j)),
            pl.BlockSpec((tf, D), lambda i, j: (j, 0)),
            pl.BlockSpec((1, D), lambda i, j: (0, 0)),
        ],
        out_specs=pl.BlockSpec((tm, D), lambda i, j: (i, 0)),
        out_shape=jax.ShapeDtypeStruct((n_rows, D), F32),
        scratch_shapes=[pltpu.VMEM((tm, D), BF16), pltpu.VMEM((tm, D), BF16), pltpu.VMEM((tm, D), F32)],
        compiler_params=_cparams(("arbitrary", "arbitrary")),
        name="ffn_pipe",
    )(h, h, mod, mod, g, w1, w3, w2, fg)


def _ffn_call(h, hc, mod, g, w1, w3, w2, fg, *, n_rows, seg_rows, n_seg, tm, tf, final_norm):
    D = h.shape[1]
    FF = w1.shape[1]
    seg = lambda i, j: (jnp.minimum(i * tm // seg_rows, n_seg), 0, 0)
    if hc is None:
        return _ffn_pipe_call(h, mod, g, w1, w3, w2, fg, n_rows=n_rows, seg_rows=seg_rows, n_seg=n_seg,
                              tm=tm, tf=tf, final_norm=final_norm)
    else:
        n_lat_tiles = h.shape[0] // tm
        h_specs = [pl.BlockSpec((tm, D), lambda i, j: (jnp.minimum(i, n_lat_tiles - 1), 0)),
                   pl.BlockSpec((tm, D), lambda i, j: (jnp.maximum(i - n_lat_tiles, 0), 0))]
    return pl.pallas_call(
        functools.partial(_ffn_kernel, final_norm=final_norm, n_lat_tiles=n_lat_tiles),
        grid=(n_rows // tm, FF // tf),
        in_specs=h_specs + [
            pl.BlockSpec((1, 8, D), seg),
            pl.BlockSpec((1, D), lambda i, j: (0, 0)),
            pl.BlockSpec((D, tf), lambda i, j: (0, j)),
            pl.BlockSpec((D, tf), lambda i, j: (0, j)),
            pl.BlockSpec((tf, D), lambda i, j: (j, 0)),
            pl.BlockSpec((1, D), lambda i, j: (0, 0)),
        ],
        out_specs=pl.BlockSpec((tm, D), lambda i, j: (i, 0)),
        out_shape=jax.ShapeDtypeStruct((n_rows, D), F32),
        scratch_shapes=[pltpu.VMEM((tm, D), BF16), pltpu.VMEM((tm, D), F32)],
        compiler_params=_cparams(("parallel", "arbitrary")),
        name="ffn",
    )(h, hc, mod, g, w1, w3, w2, fg)


def _modulate(h, g, mod):
    return (_rms(h) * g * (1.0 + mod[1:2, :]) + mod[0:1, :]).astype(BF16)


def _chunk_rows(tm, nj):
    per_step = -(-tm // nj)
    return min(tm, (per_step + 15) // 16 * 16)


def _next_tile_chunk(hn_ref, modn_ref, g_ref, hm_next, j, rc):
    tm = hn_ref.shape[0]
    rows = pl.ds(pl.multiple_of(jnp.minimum(j * rc, tm - rc), 16), rc)
    hm_next[rows, :] = _modulate(hn_ref[rows, :], g_ref[...], modn_ref[0])


def _inproj_kernel(h_ref, hn_ref, mod_ref, modn_ref, g_ref, w_ref, o_ref, hm_a, hm_b, *, rc):
    i, j = pl.program_id(0), pl.program_id(1)

    @pl.when((i == 0) & (j == 0))
    def _():
        hm_a[...] = _modulate(h_ref[...], g_ref[...], mod_ref[0])

    def body(hm_cur, hm_next):
        _next_tile_chunk(hn_ref, modn_ref, g_ref, hm_next, j, rc)
        o_ref[...] = _dot(hm_cur[...], w_ref[...])

    pl.when(i % 2 == 0)(lambda: body(hm_a, hm_b))
    pl.when(i % 2 == 1)(lambda: body(hm_b, hm_a))


def _inproj_call(h, mod, g, w, *, seg_rows, n_seg, tm, tn):
    R, D = h.shape
    N = w.shape[1]
    ni, nj = R // tm, N // tn
    nxt = lambda i: jnp.minimum(i + 1, ni - 1)
    seg = lambda i: jnp.minimum(i * tm // seg_rows, n_seg)
    return pl.pallas_call(
        functools.partial(_inproj_kernel, rc=_chunk_rows(tm, nj)),
        grid=(ni, nj),
        in_specs=[
            pl.BlockSpec((tm, D), lambda i, j: (i, 0)),
            pl.BlockSpec((tm, D), lambda i, j: (nxt(i), 0)),
            pl.BlockSpec((1, 8, D), lambda i, j: (seg(i), 0, 0)),
            pl.BlockSpec((1, 8, D), lambda i, j: (seg(nxt(i)), 0, 0)),
            pl.BlockSpec((1, D), lambda i, j: (0, 0)),
            pl.BlockSpec((D, tn), lambda i, j: (0, j)),
        ],
        out_specs=pl.BlockSpec((tm, tn), lambda i, j: (i, j)),
        out_shape=jax.ShapeDtypeStruct((R, N), F32),
        scratch_shapes=[pltpu.VMEM((tm, D), BF16)] * 2,
        compiler_params=_cparams(("arbitrary", "arbitrary")),
        name="inproj",
    )(h, h, mod, mod, g, w)


def _outproj_kernel(xa_ref, xb_ref, wa_ref, wb_ref, h_ref, mod_ref, o_ref):
    y = _dot(xa_ref[...], wa_ref[...]) + _dot(xb_ref[...], wb_ref[...])
    o_ref[...] = h_ref[...] + mod_ref[0, 0:1, :] * y


def _outproj_call(xa, xb, wa, wb, h, mod, *, n_rows, seg_rows, n_seg, tm):
    D = h.shape[1]
    Ka, Kb = xa.shape[1], xb.shape[1]
    seg = lambda i: (jnp.minimum(i * tm // seg_rows, n_seg), 0, 0)
    return pl.pallas_call(
        _outproj_kernel,
        grid=(n_rows // tm,),
        in_specs=[
            pl.BlockSpec((tm, Ka), lambda i: (i, 0)),
            pl.BlockSpec((tm, Kb), lambda i: (i, 0)),
            pl.BlockSpec((Ka, D), lambda i: (0, 0)),
            pl.BlockSpec((Kb, D), lambda i: (0, 0)),
            pl.BlockSpec((tm, D), lambda i: (i, 0)),
            pl.BlockSpec((1, 8, D), seg),
        ],
        out_specs=pl.BlockSpec((tm, D), lambda i: (i, 0)),
        out_shape=jax.ShapeDtypeStruct((n_rows, D), F32),
        compiler_params=_cparams(("parallel",)),
        name="outproj",
    )(xa, xb, wa, wb, h, mod)


def _rope_tables(T, head_dim):
    half = head_dim // 2
    n_freq = head_dim // 4
    t = jnp.arange(T)
    r = (t // GRID_W).astype(F32)
    cidx = (t % GRID_W).astype(F32)
    inv = ROPE_THETA ** (-jnp.arange(n_freq, dtype=F32) / n_freq)
    ang = jnp.concatenate([r[:, None] * inv, cidx[:, None] * inv], axis=-1)
    cos, sin = jnp.cos(ang), jnp.sin(ang)
    cos_h = jnp.concatenate([cos, cos], axis=-1)
    sin_h = jnp.concatenate([-sin, sin], axis=-1)
    reps = LANE // head_dim
    return jnp.tile(cos_h, (1, reps)), jnp.tile(sin_h, (1, reps))


def _rope128(x, cos, sin):
    return x * cos + pltpu.roll(x, 64, 1) * sin


def _rope64(x, cos, sin):
    lane = lax.broadcasted_iota(jnp.int32, x.shape, 1)
    first = jnp.bitwise_and(lane, 63) < 32
    partner = jnp.where(first, pltpu.roll(x, 96, 1), pltpu.roll(x, 32, 1))
    return x * cos + partner * sin


def _attn_a_kernel(q_ref, k_ref, v_ref, kc_ref, vc_ref, cq_ref, sq_ref, ck_ref, sk_ref, gq_ref, gk_ref,
                   o_ref, kall, vall, *, tc, t, rk, kc):
    grp = A_HEADS // A_KV

    @pl.when(pl.program_id(2) == 0)
    def _():
        gk = gk_ref[...]
        kall[0:tc, :] = (_rms(kc_ref[...]) * gk).astype(BF16)
        vall[0:tc, :] = vc_ref[...].astype(BF16)

        def body(c, carry):
            rows = pl.ds(pl.multiple_of(c * rk, rk), rk)
            kn = _rms(k_ref[rows, :]) * gk
            dst = pl.ds(pl.multiple_of(tc + c * rk, math.gcd(tc, rk)), rk)
            kall[dst, :] = _rope128(kn, ck_ref[rows, :], sk_ref[rows, :]).astype(BF16)
            vall[dst, :] = v_ref[rows, :].astype(BF16)
            return carry

        lax.fori_loop(0, t // rk, body, 0)

    q4 = q_ref[...]
    qs = jnp.concatenate([q4[:, h * A_DH:(h + 1) * A_DH] for h in range(grp)], axis=0)
    qn = _rms(qs) * gq_ref[...]
    cos = jnp.concatenate([cq_ref[...]] * grp, axis=0)
    sin = jnp.concatenate([sq_ref[...]] * grp, axis=0)
    qr = (_rope128(qn, cos, sin) * (A_DH ** -0.5 * LOG2E)).astype(BF16)
    m = l = acc = None
    for k0, kn in [(0, tc)] + [(tc + c * kc, kc) for c in range(t // kc)]:
        s = _dot_nt(qr, kall[k0:k0 + kn, :])
        smax = jnp.max(s, axis=-1, keepdims=True)
        if m is None:
            m = smax
            p = jnp.exp2(s - m)
            l = jnp.sum(p, axis=-1, keepdims=True)
            acc = _dot(p.astype(BF16), vall[k0:k0 + kn, :])
        else:
            m_new = jnp.maximum(m, smax)
            alpha = jnp.exp2(m - m_new)
            p = jnp.exp2(s - m_new)
            l = alpha * l + jnp.sum(p, axis=-1, keepdims=True)
            acc = alpha * acc + _dot(p.astype(BF16), vall[k0:k0 + kn, :])
            m = m_new
    o = acc / l
    tq = q4.shape[0]
    for h in range(grp):
        o_ref[:, h * A_DH:(h + 1) * A_DH] = o[h * tq:(h + 1) * tq, :].astype(o_ref.dtype)


def _attn_a_call(z, cos, sin, gq, gk, *, B, T, Tc):
    tq = Q_BLOCK
    nq = T // tq
    grp_w = (A_HEADS // A_KV) * A_DH
    kcol = A_HEADS
    cbase = B * T // Tc
    rk = min(512, T)
    return pl.pallas_call(
        functools.partial(_attn_a_kernel, tc=Tc, t=T, rk=rk, kc=min(2048, T)),
        grid=(B, A_KV, nq),
        in_specs=[
            pl.BlockSpec((tq, grp_w), lambda b, kv, i: (b * nq + i, kv)),
            pl.BlockSpec((T, A_DH), lambda b, kv, i: (b, kcol + kv)),
            pl.BlockSpec((T, A_DH), lambda b, kv, i: (b, kcol + A_KV + kv)),
            pl.BlockSpec((Tc, A_DH), lambda b, kv, i: (cbase + b, kcol + kv)),
            pl.BlockSpec((Tc, A_DH), lambda b, kv, i: (cbase + b, kcol + A_KV + kv)),
            pl.BlockSpec((tq, LANE), lambda b, kv, i: (i, 0)),
            pl.BlockSpec((tq, LANE), lambda b, kv, i: (i, 0)),
            pl.BlockSpec((T, LANE), lambda b, kv, i: (0, 0)),
            pl.BlockSpec((T, LANE), lambda b, kv, i: (0, 0)),
            pl.BlockSpec((1, A_DH), lambda b, kv, i: (0, 0)),
            pl.BlockSpec((1, A_DH), lambda b, kv, i: (0, 0)),
        ],
        out_specs=pl.BlockSpec((tq, grp_w), lambda b, kv, i: (b * nq + i, kv)),
        out_shape=jax.ShapeDtypeStruct((B * T, A_HEADS * A_DH), BF16),
        scratch_shapes=[pltpu.VMEM((Tc + T, A_DH), BF16), pltpu.VMEM((Tc + T, A_DH), BF16)],
        compiler_params=_cparams(("parallel", "parallel", "arbitrary")),
        name="attn_a",
    )(z, z, z, z, z, cos, sin, cos, sin, gq, gk)


def _attn_a_ctx_kernel(q_ref, kc_ref, vc_ref, gq_ref, gk_ref, o_ref):
    grp = A_HEADS // A_KV
    q4 = q_ref[...]
    tq = q4.shape[0]
    qs = jnp.concatenate([q4[:, h * A_DH:(h + 1) * A_DH] for h in range(grp)], axis=0)
    qn = (_rms(qs) * gq_ref[...] * (A_DH ** -0.5)).astype(BF16)
    kc = (_rms(kc_ref[...]) * gk_ref[...]).astype(BF16)
    s = _dot_nt(qn, kc)
    p = jnp.exp(s - jnp.max(s, axis=-1, keepdims=True))
    l = jnp.sum(p, axis=-1, keepdims=True)
    o = _dot(p.astype(BF16), vc_ref[...].astype(BF16)) / l
    for h in range(grp):
        o_ref[:, h * A_DH:(h + 1) * A_DH] = o[h * tq:(h + 1) * tq, :].astype(o_ref.dtype)


def _attn_a_ctx_call(z, gq, gk, *, B, T, Tc):
    grp_w = (A_HEADS // A_KV) * A_DH
    kcol = A_HEADS
    cbase = B * T // Tc
    return pl.pallas_call(
        _attn_a_ctx_kernel,
        grid=(B, A_KV),
        in_specs=[
            pl.BlockSpec((Tc, grp_w), lambda b, kv: (cbase + b, kv)),
            pl.BlockSpec((Tc, A_DH), lambda b, kv: (cbase + b, kcol + kv)),
            pl.BlockSpec((Tc, A_DH), lambda b, kv: (cbase + b, kcol + A_KV + kv)),
            pl.BlockSpec((1, A_DH), lambda b, kv: (0, 0)),
            pl.BlockSpec((1, A_DH), lambda b, kv: (0, 0)),
        ],
        out_specs=pl.BlockSpec((Tc, grp_w), lambda b, kv: (b, kv)),
        out_shape=jax.ShapeDtypeStruct((B * Tc, A_HEADS * A_DH), BF16),
        compiler_params=_cparams(("parallel", "parallel")),
        name="attn_a_ctx",
    )(z, z, z, gq, gk)


MLSTM_CAUG = B_DV + 8


def _mlstm_local_kernel(q_ref, k_ref, v0_ref, v1_ref, gc_ref, gr_ref, bcol_ref, brow_ref,
                        nloc_ref, stat_ref, scal_ref, uaug_ref, *, cb):
    L = B_CHUNK
    row = lax.broadcasted_iota(jnp.int32, (L, L), 0)
    col = lax.broadcasted_iota(jnp.int32, (L, L), 1)
    gcol = (gc_ref[...] + bcol_ref[...]).reshape(cb, L, LANE)
    grow = gr_ref[...] + brow_ref[...][None]
    sub0 = lax.broadcasted_iota(jnp.int32, (cb, 8, B_DQK), 1) == 0
    n_stat = 3 * B_HEADS
    stat_ref[:, :, n_stat:] = jnp.zeros((2, cb * L, stat_ref.shape[2] - n_stat), F32)
    for hd in range(B_HEADS):
        q = q_ref[:, hd * B_DQK:(hd + 1) * B_DQK].reshape(cb, L, B_DQK)
        k = k_ref[:, hd * B_DQK:(hd + 1) * B_DQK].reshape(cb, L, B_DQK)
        v_ref = v0_ref if hd < 2 else v1_ref
        v = v_ref[:, (hd % 2) * B_DV:(hd % 2 + 1) * B_DV].reshape(cb, L, B_DV)
        qb = (q * (B_DQK ** -0.5)).astype(BF16)
        kb = k.astype(BF16)
        vb = v.astype(BF16)
        qk = jnp.einsum('cld,csd->cls', qb, kb, preferred_element_type=F32)
        for d in range(2):
            gi = 2 * d * B_HEADS + hd
            gf = gi + B_HEADS
            li_col = gcol[:, :, gi:gi + 1]
            lf_col = _log_sigmoid(gcol[:, :, gf:gf + 1])
            li_row = grow[:, gi:gi + 1, :]
            lf_row = _log_sigmoid(grow[:, gf:gf + 1, :])
            mask = ((col <= row) if d == 0 else (col >= row))[None]
            mask_t = ((row <= col) if d == 0 else (row >= col))[None]
            bcum_col = jnp.sum(jnp.where(mask, lf_row, 0.0), axis=2, keepdims=True)
            bcum_row = jnp.sum(jnp.where(mask_t, lf_col, 0.0), axis=1, keepdims=True)
            log_d = jnp.where(mask, bcum_col - bcum_row + li_row, -jnp.inf)
            a = jnp.max(log_d, axis=2, keepdims=True)
            s = qk * jnp.exp(log_d - a)
            nloc = jnp.einsum('cls,csv->clv', s.astype(BF16), vb, preferred_element_type=F32)
            dloc = jnp.sum(s, axis=2, keepdims=True)
            b_last = jnp.sum(lf_row, axis=2, keepdims=True)
            log_w = b_last - bcum_col + li_col
            b_loc = jnp.max(log_w, axis=1, keepdims=True)
            w = jnp.exp(log_w - b_loc)
            u = jnp.einsum('csv,csd->cvd', (w * v).astype(BF16), kb, preferred_element_type=F32)
            nu = jnp.sum(w * kb.astype(F32), axis=1, keepdims=True)
            nloc_ref[d, :, hd * B_DV:(hd + 1) * B_DV] = nloc.reshape(cb * L, B_DV)
            stat_ref[d, :, hd:hd + 1] = (a - bcum_col).reshape(cb * L, 1)
            stat_ref[d, :, B_HEADS + hd:B_HEADS + hd + 1] = dloc.reshape(cb * L, 1)
            stat_ref[d, :, 2 * B_HEADS + hd:2 * B_HEADS + hd + 1] = bcum_col.reshape(cb * L, 1)
            scal_ref[d, :, hd:hd + 1, :] = jnp.broadcast_to(b_last, (cb, 1, LANE))
            scal_ref[d, :, B_HEADS + hd:B_HEADS + hd + 1, :] = jnp.broadcast_to(b_loc, (cb, 1, LANE))
            uaug_ref[d, :, hd, 0:B_DV, :] = u
            uaug_ref[d, :, hd, B_DV:MLSTM_CAUG, :] = jnp.where(sub0, nu, 0.0)


def _mlstm_local_call(z, grow, bcol, brow, *, cb):
    L = B_CHUNK
    R = z.shape[0]
    nchunk = R // L
    W = B_HEADS * B_DV
    return pl.pallas_call(
        functools.partial(_mlstm_local_kernel, cb=cb),
        grid=(nchunk // cb,),
        in_specs=[
            pl.BlockSpec((cb * L, 512), lambda i: (i, 3)),
            pl.BlockSpec((cb * L, 512), lambda i: (i, 4)),
            pl.BlockSpec((cb * L, 512), lambda i: (i, 5)),
            pl.BlockSpec((cb * L, 512), lambda i: (i, 6)),
            pl.BlockSpec((cb * L, LANE), lambda i: (i, AB_GATE_COL // LANE)),
            pl.BlockSpec((cb, 16, L), lambda i: (i, 0, 0)),
            pl.BlockSpec((1, LANE), lambda i: (0, 0)),
            pl.BlockSpec((16, 1), lambda i: (0, 0)),
        ],
        out_specs=[
            pl.BlockSpec((2, cb * L, W), lambda i: (0, i, 0)),
            pl.BlockSpec((2, cb * L, 16), lambda i: (0, i, 0)),
            pl.BlockSpec((2, cb, 8, LANE), lambda i: (0, i, 0, 0)),
            pl.BlockSpec((2, cb, B_HEADS, MLSTM_CAUG, B_DQK), lambda i: (0, i, 0, 0, 0)),
        ],
        out_shape=[
            jax.ShapeDtypeStruct((2, R, W), F32),
            jax.ShapeDtypeStruct((2, R, 16), F32),
            jax.ShapeDtypeStruct((2, nchunk, 8, LANE), F32),
            jax.ShapeDtypeStruct((2, nchunk, B_HEADS, MLSTM_CAUG, B_DQK), F32),
        ],
        compiler_params=_cparams(("parallel",)),
        name="mlstm_local",
    )(z, z, z, z, z, grow, bcol, brow)


def _mlstm_scan_kernel(qf_ref, nf_ref, stf_ref, scf_ref, uf_ref, qb_ref, nb_ref, stb_ref, scb_ref, ub_ref,
                       of_ref, ob_ref, c_s, m_s):
    @pl.when(pl.program_id(1) == 0)
    def _():
        c_s[...] = jnp.zeros_like(c_s)
        m_s[...] = jnp.zeros_like(m_s)

    L = B_CHUNK
    units = []
    for d, (q_ref, n_ref, st_ref, sc_ref, u_ref, o_ref) in enumerate((
            (qf_ref, nf_ref, stf_ref, scf_ref, uf_ref, of_ref),
            (qb_ref, nb_ref, stb_ref, scb_ref, ub_ref, ob_ref))):
        st = st_ref[0]
        for hd in range(B_HEADS):
            u = d * B_HEADS + hd
            units.append((u, hd, q_ref, n_ref, st, sc_ref, u_ref, o_ref, c_s[u], m_s[u, 0:1, :]))
    new_state = []
    for u, hd, q_ref, n_ref, st, sc_ref, u_ref, o_ref, c_prev, m_row in units:
        rep = lambda j: jnp.broadcast_to(st[:, j:j + 1], (L, LANE))
        g, dloc, bcum = rep(hd), rep(B_HEADS + hd), rep(2 * B_HEADS + hd)
        m = jnp.broadcast_to(m_row, (L, LANE))
        e_intra = jnp.exp(jnp.minimum(g - m, 0.0))
        w_inter = jnp.exp(jnp.minimum(m - g, 0.0))
        floor = jnp.exp(-(bcum + jnp.maximum(m, g)))
        qb = (q_ref[:, hd * B_DQK:(hd + 1) * B_DQK] * (B_DQK ** -0.5)).astype(BF16)
        inter = _dot_nt(qb, c_prev.astype(BF16))
        den = e_intra * dloc + w_inter * inter[:, B_DV:B_DV + LANE]
        r = 1.0 / jnp.maximum(jnp.abs(den), floor)
        for j in range(B_DV // LANE):
            c0 = hd * B_DV + j * LANE
            num = e_intra * n_ref[0, :, c0:c0 + LANE] + w_inter * inter[:, j * LANE:(j + 1) * LANE]
            o_ref[0, :, c0:c0 + LANE] = num * r
        b_last = sc_ref[0, 0, hd:hd + 1, :]
        b_loc = sc_ref[0, 0, B_HEADS + hd:B_HEADS + hd + 1, :]
        m_new = jnp.maximum(b_last + m_row, b_loc)
        decay = jnp.exp(b_last + m_row - m_new)
        scl = jnp.exp(b_loc - m_new)
        c_new = decay * c_prev[0:B_DV] + scl * u_ref[0, 0, hd, 0:B_DV, :]
        n_new = decay * c_prev[B_DV:] + scl * jnp.broadcast_to(u_ref[0, 0, hd, B_DV:B_DV + 1, :], (LANE, B_DQK))
        new_state.append((u, c_new, n_new, m_new))
    for u, c_new, n_new, m_new in new_state:
        c_s[u, 0:B_DV, :] = c_new
        c_s[u, B_DV:, :] = n_new
        m_s[u] = jnp.broadcast_to(m_new, m_s.shape[1:])


def _mlstm_scan_call(z, nloc, stat, scal, uaug, *, B, T, Tc):
    L = B_CHUNK
    ncl, ncc = T // L, Tc // L
    nc = ncl + ncc
    cbase = B * T // L
    R = z.shape[0]
    W = B_HEADS * B_DV

    def blk(b, c, d):
        is_ctx = c < ncc
        jc = c if d == 0 else ncc - 1 - c
        jl = (c - ncc) if d == 0 else (ncl - 1 - (c - ncc))
        return jnp.where(is_ctx, cbase + b * ncc + jc, b * ncl + jl)

    def specs(d):
        return [
            pl.BlockSpec((L, 512), lambda b, c: (blk(b, c, d), 3)),
            pl.BlockSpec((1, L, W), lambda b, c: (d, blk(b, c, d), 0)),
            pl.BlockSpec((1, L, 16), lambda b, c: (d, blk(b, c, d), 0)),
            pl.BlockSpec((1, 1, 8, LANE), lambda b, c: (d, blk(b, c, d), 0, 0)),
            pl.BlockSpec((1, 1, B_HEADS, MLSTM_CAUG, B_DQK), lambda b, c: (d, blk(b, c, d), 0, 0, 0)),
        ]

    out_sds = jax.ShapeDtypeStruct((1, R, W), F32)
    return pl.pallas_call(
        _mlstm_scan_kernel,
        grid=(B, nc),
        in_specs=specs(0) + specs(1),
        out_specs=[
            pl.BlockSpec((1, L, W), lambda b, c: (0, blk(b, c, 0), 0)),
            pl.BlockSpec((1, L, W), lambda b, c: (0, blk(b, c, 1), 0)),
        ],
        out_shape=[out_sds, out_sds],
        scratch_shapes=[
            pltpu.VMEM((2 * B_HEADS, B_DV + LANE, B_DQK), F32),
            pltpu.VMEM((2 * B_HEADS, 8, LANE), F32),
        ],
        compiler_params=_cparams(("parallel", "arbitrary")),
        name="mlstm_scan",
    )(z, nloc, stat, scal, uaug, z, nloc, stat, scal, uaug)


def _mlstm_finish_kernel(hf_ref, hb_ref, o0_ref, o1_ref, hg_ref, y_ref):
    for hd in range(B_HEADS):
        sl = slice(hd * B_DV, (hd + 1) * B_DV)
        hh = hf_ref[0, :, sl] + hb_ref[0, :, sl]
        o_ref = o0_ref if hd < 2 else o1_ref
        o = o_ref[:, (hd % 2) * B_DV:(hd % 2 + 1) * B_DV]
        y_ref[:, sl] = (_sigmoid(o) * _rms(hh) * hg_ref[:, sl]).astype(y_ref.dtype)


def _mlstm_finish_call(hf, hb, z, head_g, *, tm):
    R = z.shape[0]
    W = B_HEADS * B_DV
    return pl.pallas_call(
        _mlstm_finish_kernel,
        grid=(R // tm,),
        in_specs=[
            pl.BlockSpec((1, tm, W), lambda i: (0, i, 0)),
            pl.BlockSpec((1, tm, W), lambda i: (0, i, 0)),
            pl.BlockSpec((tm, 512), lambda i: (i, 7)),
            pl.BlockSpec((tm, 512), lambda i: (i, 8)),
            pl.BlockSpec((1, W), lambda i: (0, 0)),
        ],
        out_specs=pl.BlockSpec((tm, W), lambda i: (i, 0)),
        out_shape=jax.ShapeDtypeStruct((R, W), BF16),
        compiler_params=_cparams(("parallel",)),
        name="mlstm_finish",
    )(hf, hb, z, z, head_g)


def _s5_params(a_re, a_im, log_dt, b_re, b_im, c_re, c_im, reverse):
    hp = lax.Precision.HIGHEST
    Lc = S5_CHUNK
    dt = jnp.exp(log_dt)[:, None]
    lam_re, lam_im = a_re * dt, a_im * dt
    mag = jnp.exp(lam_re)
    ab_re, ab_im = mag * jnp.cos(lam_im), mag * jnp.sin(lam_im)
    den = a_re * a_re + a_im * a_im
    nr, ni = ab_re - 1.0, ab_im
    k_re, k_im = (nr * a_re + ni * a_im) / den, (ni * a_re - nr * a_im) / den
    bb_re = k_re[..., None] * b_re - k_im[..., None] * b_im
    bb_im = k_re[..., None] * b_im + k_im[..., None] * b_re
    tau = jnp.arange(Lc + 1, dtype=F32)[:, None, None]
    pm = jnp.exp(tau * lam_re)
    pw_re, pw_im = pm * jnp.cos(tau * lam_im), pm * jnp.sin(tau * lam_im)
    cp_re = c_re[None] * pw_re[:, :, None, :] - c_im[None] * pw_im[:, :, None, :]
    cp_im = c_re[None] * pw_im[:, :, None, :] + c_im[None] * pw_re[:, :, None, :]
    kk = (jnp.einsum('tgop,gpi->tgoi', cp_re, bb_re, precision=hp)
          - jnp.einsum('tgop,gpi->tgoi', cp_im, bb_im, precision=hp))
    G = a_re.shape[0]
    i_idx = jnp.arange(Lc)[:, None]
    j_idx = jnp.arange(Lc)[None, :]
    lag = (i_idx - j_idx) if reverse else (j_idx - i_idx)
    valid = lag >= 0
    kt = jnp.where(valid[:, :, None, None, None], kk[jnp.clip(lag, 0, Lc)], 0.0)
    kt = kt.transpose(2, 0, 4, 1, 3).reshape(G, Lc * C_GROUP, Lc * C_GROUP)
    e = jnp.arange(Lc) if reverse else (Lc - 1 - jnp.arange(Lc))
    w_re = pw_re[e][:, :, :, None] * bb_re[None] - pw_im[e][:, :, :, None] * bb_im[None]
    w_im = pw_re[e][:, :, :, None] * bb_im[None] + pw_im[e][:, :, :, None] * bb_re[None]
    w_re = w_re.transpose(1, 0, 3, 2).reshape(G, Lc * C_GROUP, C_STATE)
    w_im = w_im.transpose(1, 0, 3, 2).reshape(G, Lc * C_GROUP, C_STATE)
    even = (jnp.arange(G) % 2 == 0)[:, None, None]
    zc = jnp.zeros_like(w_re)
    half_cols = lambda m: jnp.where(even, jnp.concatenate([m, zc], axis=-1), jnp.concatenate([zc, m], axis=-1))
    wre, wim = half_cols(w_re), half_cols(w_im)
    f = (Lc - jnp.arange(Lc)) if reverse else (jnp.arange(Lc) + 1)
    cj_re = cp_re[f].transpose(1, 3, 0, 2).reshape(G, C_STATE, Lc * C_GROUP)
    cj_im = -cp_im[f].transpose(1, 3, 0, 2).reshape(G, C_STATE, Lc * C_GROUP)
    zr = jnp.zeros_like(cj_re)
    half_rows = lambda m: jnp.where(even, jnp.concatenate([m, zr], axis=1), jnp.concatenate([zr, m], axis=1))
    cre, cim = half_rows(cj_re), half_rows(cj_im)
    pair = lambda v: v.reshape(G // 2, 1, 2 * C_STATE)
    are, aim = pair(pw_re[Lc]), pair(pw_im[Lc])
    wcat = jnp.concatenate([wre, wim], axis=-1)
    ccat = jnp.concatenate([cre, cim], axis=1)
    return kt.astype(BF16), wcat.astype(BF16), ccat.astype(BF16), are, aim


def _granule_transpose(xs):
    xs = list(xs)
    gran = jnp.right_shift(lax.broadcasted_iota(jnp.int32, (1, LANE), 1), C_GROUP.bit_length() - 1)
    for bit in (1, 2, 4):
        hi = jnp.bitwise_and(gran, bit) != 0
        for a in range(S5_GB):
            if a & bit:
                continue
            p = a | bit
            xa, xp = xs[a], xs[p]
            xs[a] = jnp.where(hi, pltpu.roll(xp, bit * C_GROUP, 1), xa)
            xs[p] = jnp.where(hi, xp, pltpu.roll(xa, LANE - bit * C_GROUP, 1))
    return xs


def _s5_kernel(ul_ref, uc_ref, kt_ref, wcat_ref, ccat_ref, are_ref, aim_ref, y_ref,
               ug_s, wr_s, wi_s, xr_s, xi_s, y_s, *, nb, ncc, ncl):
    Lc = S5_CHUNK
    lat0 = nb * ncc
    for src, row0, n in ((uc_ref, 0, nb * ncc), (ul_ref, lat0, nb * ncl)):
        folded = _granule_transpose([src[pl.ds(i, n, stride=Lc), :] for i in range(Lc)])
        for g in range(S5_GB):
            ug_s[g, row0:row0 + n, :] = folded[g].astype(BF16)

    npair = S5_GB // 2
    for d in range(2):
        for k in range(npair):
            u0, u1 = ug_s[2 * k], ug_s[2 * k + 1]
            w = _dot(u0, wcat_ref[d, 2 * k]) + _dot(u1, wcat_ref[d, 2 * k + 1])
            wr_s[k] = w[:, 0:LANE]
            wi_s[k] = w[:, LANE:]
        are = [jnp.broadcast_to(are_ref[d, k], (nb, LANE)) for k in range(npair)]
        aim = [jnp.broadcast_to(aim_ref[d, k], (nb, LANE)) for k in range(npair)]

        def make_body(base, n):
            def body(s, carry):
                c = s if d == 0 else n - 1 - s
                rows = pl.ds(base + c, nb, stride=n)
                new = []
                for k in range(npair):
                    xr, xi = carry[2 * k], carry[2 * k + 1]
                    xr_s[k, rows, :] = xr
                    xi_s[k, rows, :] = xi
                    new.append(are[k] * xr - aim[k] * xi + wr_s[k, rows, :])
                    new.append(are[k] * xi + aim[k] * xr + wi_s[k, rows, :])
                return tuple(new)
            return body

        state = tuple(jnp.zeros((nb, LANE), F32) for _ in range(2 * npair))
        state = lax.fori_loop(0, ncc, make_body(0, ncc), state, unroll=4)
        lax.fori_loop(0, ncl, make_body(lat0, ncl), state, unroll=4)
        for g in range(S5_GB):
            y = (_dot(ug_s[g, lat0:, :], kt_ref[d, g])
                 + _dot(jnp.concatenate([xr_s[g // 2, lat0:, :], xi_s[g // 2, lat0:, :]], axis=1).astype(BF16),
                        ccat_ref[d, g]))
            if d == 0:
                y_s[g] = y
            else:
                y_s[g] += y

    unfolded = _granule_transpose([y_s[g] for g in range(S5_GB)])
    for j in range(Lc):
        y_ref[pl.ds(j, nb * ncl, stride=Lc), :] = unfolded[j]


def _s5_call(z, kt, wcat, ccat, are, aim, *, B, T, Tc, nb):
    Lc = S5_CHUNK
    ncl, ncc = T // Lc, Tc // Lc
    nr = nb * (ncl + ncc)
    cbase = B * T // (nb * Tc)
    wspec = pl.BlockSpec((2, S5_GB, LANE, LANE), lambda i, h: (0, i, 0, 0))
    aspec = pl.BlockSpec((2, S5_GB // 2, 1, LANE), lambda i, h: (0, i, 0, 0))
    big = lambda dt: pltpu.VMEM((S5_GB, nr, LANE), dt)
    half = pltpu.VMEM((S5_GB // 2, nr, LANE), F32)
    return pl.pallas_call(
        functools.partial(_s5_kernel, nb=nb, ncc=ncc, ncl=ncl),
        grid=(C_GROUPS // S5_GB, B // nb),
        in_specs=[
            pl.BlockSpec((nb * T, LANE), lambda i, h: (h, i)),
            pl.BlockSpec((nb * Tc, LANE), lambda i, h: (cbase + h, i)),
            wspec,
            pl.BlockSpec((2, S5_GB, LANE, 2 * LANE), lambda i, h: (0, i, 0, 0)),
            pl.BlockSpec((2, S5_GB, 2 * LANE, LANE), lambda i, h: (0, i, 0, 0)),
            aspec, aspec],
        out_specs=pl.BlockSpec((nb * T, LANE), lambda i, h: (h, i)),
        out_shape=jax.ShapeDtypeStruct((B * T, C_WIDTH), F32),
        scratch_shapes=[big(BF16), half, half, half, half,
                        pltpu.VMEM((S5_GB, nb * ncl, LANE), F32)],
        compiler_params=_cparams(("parallel", "parallel")),
        name="s5_scan",
    )(z, z, kt, wcat, ccat, are, aim)


def _s5_finish_kernel(y_ref, u_ref, d_ref, w_ref, b_ref, o_ref):
    y = y_ref[...] + d_ref[...] * u_ref[...]
    g = 0.5 * y * (1.0 + jnp.tanh(math.sqrt(2.0 / math.pi) * (y + 0.044715 * (y * y * y))))
    o_ref[...] = (g * _sigmoid(_dot(g.astype(BF16), w_ref[...]) + b_ref[...])).astype(o_ref.dtype)


def _s5_finish_call(y, z, d_skip, glu_w, glu_b, *, n_rows, tm):
    W = C_WIDTH
    return pl.pallas_call(
        _s5_finish_kernel,
        grid=(n_rows // tm,),
        in_specs=[
            pl.BlockSpec((tm, W), lambda i: (i, 0)),
            pl.BlockSpec((tm, W), lambda i: (i, 0)),
            pl.BlockSpec((1, W), lambda i: (0, 0)),
            pl.BlockSpec((W, W), lambda i: (0, 0)),
            pl.BlockSpec((1, W), lambda i: (0, 0)),
        ],
        out_specs=pl.BlockSpec((tm, W), lambda i: (i, 0)),
        out_shape=jax.ShapeDtypeStruct((n_rows, W), BF16),
        compiler_params=_cparams(("parallel",)),
        name="s5_finish",
    )(y, z, d_skip, glu_w, glu_b)


def _ones_lane(par):
    return D_DH if par == 0 else 0


def _attn_d_kernel(q_ref, k_ref, v_ref, kc_ref, vc_ref, cq_ref, sq_ref, ck_ref, sk_ref, sink_ref,
                   o_ref, kpe, kpo, vpe, vpo, kce, kco, vce, vco, band, *, t, tc, rk):
    W = D_WINDOW
    kv = pl.program_id(1)
    i = pl.program_id(2)
    lane = lax.broadcasted_iota(jnp.int32, (1, LANE), 1)
    lo = lane < D_DH

    def halves(x, ones=False):
        sel = jnp.where(kv == 0, x, pltpu.roll(x, 64, 1))
        e = jnp.where(lo, sel, 0.0)
        o = pltpu.roll(e, 64, 1)
        if ones:
            e = jnp.where(lane == _ones_lane(0), 1.0, e)
            o = jnp.where(lane == _ones_lane(1), 1.0, o)
        return e.astype(BF16), o.astype(BF16)

    @pl.when(i == 0)
    def _():
        zpad = jnp.zeros((W, LANE), BF16)
        for ref in (kpe, kpo, vpe, vpo):
            ref[0:W, :] = zpad
            ref[W + t:W + t + W, :] = zpad
        kce[...], kco[...] = halves(kc_ref[...])
        vce[...], vco[...] = halves(vc_ref[...], ones=True)

        def body(c, carry):
            rows = pl.ds(pl.multiple_of(c * rk, rk), rk)
            dst = pl.ds(pl.multiple_of(W + c * rk, math.gcd(W, rk)), rk)
            kpe[dst, :], kpo[dst, :] = halves(_rope64(k_ref[rows, :], ck_ref[rows, :], sk_ref[rows, :]))
            vpe[dst, :], vpo[dst, :] = halves(v_ref[rows, :], ones=True)
            return carry

        lax.fori_loop(0, t // rk, body, 0)

    tq = q_ref.shape[0]
    nslab = q_ref.shape[1] // LANE
    span = tq + 2 * W

    @pl.when(i == 0)
    def _():
        rr = jnp.bitwise_and(lax.broadcasted_iota(jnp.int32, (nslab * tq, span), 0), tq - 1)
        cc = lax.broadcasted_iota(jnp.int32, (nslab * tq, span), 1)
        band[...] = jnp.where(jnp.abs(rr + W - cc) <= W, 0.0, -jnp.inf)

    cos, sin = cq_ref[...], sq_ref[...]
    qs = jnp.concatenate(
        [_rope64(q_ref[:, j * LANE:(j + 1) * LANE], cos, sin) for j in range(nslab)], axis=0)
    qs = (qs * (D_DH ** -0.5 * LOG2E)).astype(BF16)
    rows_k = pl.ds(pl.multiple_of(i * tq, tq), span)
    kpos = i * tq - W + lax.broadcasted_iota(jnp.int32, (1, span), 1)
    bias = band[...] + jnp.where((kpos >= 0) & (kpos < t), 0.0, -jnp.inf)
    out = None
    lane_o = lax.broadcasted_iota(jnp.int32, (1, LANE), 1)
    for par, (kp, vp, kcx, vcx) in enumerate(((kpe, vpe, kce, vce), (kpo, vpo, kco, vco))):
        s = jnp.concatenate([_dot_nt(qs, kcx[...]), _dot_nt(qs, kp[rows_k, :]) + bias], axis=1)
        snk = LOG2E * jnp.concatenate(
            [jnp.broadcast_to(sink_ref[0, 0:1, 2 * j + par:2 * j + par + 1], (tq, 1)) for j in range(nslab)],
            axis=0)
        m = jnp.maximum(jnp.max(s, axis=-1, keepdims=True), snk)
        p = jnp.exp2(s - m).astype(BF16)
        o = _dot(p[:, 0:tc], vcx[...]) + _dot(p[:, tc:], vp[rows_k, :])
        one = _ones_lane(par)
        l = o[:, one:one + 1] + jnp.exp2(snk - m)
        own = (lane_o < D_DH) if par == 0 else (lane_o >= D_DH)
        o = jnp.where(own, o, 0.0) / l
        out = o if out is None else out + o
    for j in range(nslab):
        o_ref[:, j * LANE:(j + 1) * LANE] = out[j * tq:(j + 1) * tq, :].astype(o_ref.dtype)


def _attn_d_call(z, cos, sin, sink, *, B, T, Tc):
    tq = min(2 * Q_BLOCK, T)
    nq = T // tq
    grp_w = (D_HEADS // D_KV) * D_DH
    qcol = C_WIDTH // grp_w
    kcol = (C_WIDTH + D_HEADS * D_DH) // LANE
    cbase = B * T // Tc
    rk = min(512, T)
    pad = lambda n: pltpu.VMEM((n, LANE), BF16)
    return pl.pallas_call(
        functools.partial(_attn_d_kernel, t=T, tc=Tc, rk=rk),
        grid=(B, D_KV, nq),
        in_specs=[
            pl.BlockSpec((tq, grp_w), lambda b, kv, i: (b * nq + i, qcol + kv)),
            pl.BlockSpec((T, LANE), lambda b, kv, i: (b, kcol)),
            pl.BlockSpec((T, LANE), lambda b, kv, i: (b, kcol + 1)),
            pl.BlockSpec((Tc, LANE), lambda b, kv, i: (cbase + b, kcol)),
            pl.BlockSpec((Tc, LANE), lambda b, kv, i: (cbase + b, kcol + 1)),
            pl.BlockSpec((tq, LANE), lambda b, kv, i: (i, 0)),
            pl.BlockSpec((tq, LANE), lambda b, kv, i: (i, 0)),
            pl.BlockSpec((T, LANE), lambda b, kv, i: (0, 0)),
            pl.BlockSpec((T, LANE), lambda b, kv, i: (0, 0)),
            pl.BlockSpec((1, 1, LANE), lambda b, kv, i: (kv, 0, 0)),
        ],
        out_specs=pl.BlockSpec((tq, grp_w), lambda b, kv, i: (b * nq + i, kv)),
        out_shape=jax.ShapeDtypeStruct((B * T, D_HEADS * D_DH), BF16),
        scratch_shapes=[pad(T + 2 * D_WINDOW)] * 4 + [pad(Tc)] * 4
        + [pltpu.VMEM((grp_w // LANE * tq, tq + 2 * D_WINDOW), F32)],
        compiler_params=_cparams(("parallel", "parallel", "arbitrary")),
        name="attn_d",
    )(z, z, z, z, z, cos, sin, cos, sin, sink)


def _mod_rows(ml, idx):
    sel = ml[:, jnp.array(idx), :]
    return jnp.pad(sel, ((0, 0), (0, 8 - len(idx)), (0, 0)))


def kernel(x, c, ctx, c_ctx, mod_w, mod_b, norm_g, ffn_w1, ffn_w3, ffn_w2, ab_w_in, ab_b_gate, ab_w_out, a_gq, a_gk, b_head_g, cd_w_in, cd_w_out, s5_a_re, s5_a_im, s5_log_dt, s5_b_re, s5_b_im, s5_c_re, s5_c_im, s5_d, s5_glu_w, s5_glu_b, d_sink, final_g):
    B, T, D = x.shape
    Tc = ctx.shape[1]
    depth = mod_w.shape[0]
    n_lat, n_ctx = B * T, B * Tc
    n_all = n_lat + n_ctx
    tm = math.gcd(math.gcd(T, n_ctx), 512)
    tf = math.gcd(ffn_w1.shape[-1], 512)
    seg = dict(seg_rows=T, n_seg=B)

    h, hc0 = x.reshape(n_lat, D), ctx.reshape(n_ctx, D)
    cvec = jnp.zeros((8, D), F32).at[:B].set(c).at[B].set(c_ctx)
    mods = _modvec_call(cvec, mod_w, mod_b).reshape(depth, 8, N_MOD, D)[:, :B + 1]
    cos_a, sin_a = _rope_tables(T, A_DH)
    cos_d, sin_d = _rope_tables(T, D_DH)
    w1, w3, w2 = ffn_w1.astype(BF16), ffn_w3.astype(BF16), ffn_w2.astype(BF16)
    ones = jnp.ones((1, D), F32)

    for l in range(depth):
        last = l == depth - 1
        want_ctx = not last
        ml = mods[l]
        i = l // 2
        h = _ffn_call(h, hc0 if l == 0 else None, _mod_rows(ml, (0, 1, 2)), norm_g[l, 0][None],
                      w1[l, 0], w3[l, 0], w2[l, 0], ones,
                      n_rows=n_all, tm=tm, tf=tf, final_norm=False, **seg)
        mod_in = _mod_rows(ml, (3, 4))
        mod_out = _mod_rows(ml, (5,))
        if l % 2 == 0:
            w_in = jnp.pad(ab_w_in[i], ((0, 0), (0, AB_IN_PAD - ab_w_in.shape[-1]))).astype(BF16)
            z = _inproj_call(h, mod_in, norm_g[l, 1][None], w_in, tm=tm, tn=1024, **seg)
            ya = _attn_a_call(z, cos_a, sin_a, a_gq[i][None], a_gk[i][None], B=B, T=T, Tc=Tc)
            yac = _attn_a_ctx_call(z, a_gq[i][None], a_gk[i][None], B=B, T=T, Tc=Tc)
            ng = 4 * B_HEADS
            gates = z[:, AB_GATE_COL:AB_GATE_COL + ng]
            grow = gates.reshape(n_all // B_CHUNK, B_CHUNK, ng).transpose(0, 2, 1)
            bcol = jnp.pad(ab_b_gate[i], (0, LANE - ng))[None]
            brow = ab_b_gate[i][:, None]
            nloc, stat, scal, uaug = _mlstm_local_call(z, grow, bcol, brow, cb=math.gcd(n_all // B_CHUNK, 8))
            hf, hb = _mlstm_scan_call(z, nloc, stat, scal, uaug, B=B, T=T, Tc=Tc)
            yb = _mlstm_finish_call(hf, hb, z, b_head_g[i][None], tm=tm)
            xa = jnp.concatenate([ya, yac], axis=0)
            w_out = ab_w_out[i].astype(BF16)
            ka = A_HEADS * A_DH
            n_rows = n_all
        else:
            w_in = cd_w_in[i].astype(BF16)
            z = _inproj_call(h, mod_in, norm_g[l, 1][None], w_in, tm=tm, tn=w_in.shape[1] // 3, **seg)
            prm = [_s5_params(s5_a_re[i, d], s5_a_im[i, d], s5_log_dt[i, d], s5_b_re[i, d], s5_b_im[i, d],
                              s5_c_re[i, d], s5_c_im[i, d], reverse=(d == 1)) for d in range(2)]
            s5w = [jnp.stack([prm[0][k], prm[1][k]]) for k in range(len(prm[0]))]
            ys5 = _s5_call(z, *s5w, B=B, T=T, Tc=Tc, nb=math.gcd(B, 2))
            xa = _s5_finish_call(ys5, z, s5_d[i][None], s5_glu_w[i].astype(BF16), s5_glu_b[i][None],
                                 n_rows=n_lat, tm=tm)
            sink = jnp.pad(d_sink[i].reshape(D_KV, 1, D_HEADS // D_KV), ((0, 0), (0, 0), (0, LANE - D_HEADS // D_KV)))
            yb = _attn_d_call(z, cos_d, sin_d, sink, B=B, T=T, Tc=Tc)
            w_out = cd_w_out[i].astype(BF16)
            ka = C_WIDTH
            n_rows = n_all if want_ctx else n_lat
        if not want_ctx:
            n_rows = n_lat
        h = _outproj_call(xa, yb, w_out[:ka], w_out[ka:], h, mod_out, n_rows=n_rows, tm=tm, **seg)
        h = _ffn_call(h, None, _mod_rows(ml, (6, 7, 8)), norm_g[l, 2][None], w1[l, 1], w3[l, 1], w2[l, 1],
                      final_g[None] if last else ones, n_rows=n_rows, tm=tm, tf=tf, final_norm=last, **seg)
    return h[:n_lat].reshape(B, T, D)
```

```python
import functools
import math

import jax
import jax.numpy as jnp
from jax import lax
from jax.experimental import pallas as pl
from jax.experimental.pallas import tpu as pltpu

F32 = jnp.float32
BF16 = jnp.bfloat16

EPS = 1e-6
ROPE_THETA = 10000.0
GRID_W = 64
N_MOD = 9
Q_BLOCK = 128

A_HEADS, A_KV, A_DH = 8, 2, 128
B_HEADS, B_DQK, B_DV, B_CHUNK = 4, 128, 256, 64
C_WIDTH, C_GROUP, C_STATE = 1024, 16, 64
C_GROUPS = C_WIDTH // C_GROUP
D_HEADS, D_KV, D_DH, D_WINDOW = 16, 2, 64, 128

AB_GATE_COL = 4608
AB_IN_PAD = 5120
S5_CHUNK = 8
S5_GB = 8

LANE = 128
LOG2E = math.log2(math.e)
VMEM_LIMIT = 56 * 1024 * 1024


def _cparams(sem):
    return pltpu.CompilerParams(dimension_semantics=sem, vmem_limit_bytes=VMEM_LIMIT)


def _rms(x):
    return x * lax.rsqrt(jnp.mean(x * x, axis=-1, keepdims=True) + EPS)


def _sigmoid(x):
    return 1.0 / (1.0 + jnp.exp(-x))


def _log_sigmoid(x):
    return jnp.minimum(x, 0.0) - jnp.log(1.0 + jnp.exp(-jnp.abs(x)))


def _dot(a, b):
    return jnp.dot(a, b, preferred_element_type=F32)


def _dot_nt(a, b):
    return lax.dot_general(a, b, (((1,), (1,)), ((), ())), preferred_element_type=F32)


def _dot_tn(a, b):
    return lax.dot_general(a, b, (((0,), (0,)), ((), ())), preferred_element_type=F32)


def _modvec_kernel(c_ref, w_ref, b_ref, o_ref):
    c = c_ref[...]
    s = (c * _sigmoid(c)).astype(BF16)
    o_ref[0] = _dot(s, w_ref[0].astype(BF16)) + b_ref[0]


def _modvec_call(cvec, mod_w, mod_b):
    L, D, N = mod_w.shape
    tn = math.gcd(D, 1024)
    return pl.pallas_call(
        _modvec_kernel,
        grid=(L, N // tn),
        in_specs=[
            pl.BlockSpec((8, D), lambda l, j: (0, 0)),
            pl.BlockSpec((1, D, tn), lambda l, j: (l, 0, j)),
            pl.BlockSpec((1, 1, tn), lambda l, j: (l, 0, j)),
        ],
        out_specs=pl.BlockSpec((1, 8, tn), lambda l, j: (l, 0, j)),
        out_shape=jax.ShapeDtypeStruct((L, 8, N), F32),
        compiler_params=_cparams(("parallel", "parallel")),
        name="modvec",
    )(cvec, mod_w, mod_b.reshape(L, 1, N))


def _modulate(h, g, mod):
    return (_rms(h) * g * (1.0 + mod[1:2, :]) + mod[0:1, :]).astype(BF16)


def _ffn_kernel(h_ref, hc_ref, mod_ref, g_ref, w1_ref, w3_ref, w2_ref, fg_ref, o_ref, hm_ref, acc_ref, *,
                final_norm, n_lat_tiles):
    j = pl.program_id(1)
    tm, d = o_ref.shape

    def load_h():
        if n_lat_tiles is None:
            return h_ref[...]
        return jnp.where(pl.program_id(0) < n_lat_tiles, h_ref[...], hc_ref[...].reshape(tm, d))

    @pl.when(j == 0)
    def _():
        hm_ref[...] = _modulate(load_h(), g_ref[...], mod_ref[0])
        acc_ref[...] = jnp.zeros_like(acc_ref)

    hm = hm_ref[...]
    a = _dot(hm, w1_ref[...])
    b = _dot(hm, w3_ref[...])
    gg = (a * _sigmoid(a)) * b
    acc_ref[...] += _dot(gg.astype(BF16), w2_ref[...])

    @pl.when(j == pl.num_programs(1) - 1)
    def _():
        out = load_h() + (0.5 * mod_ref[0, 2:3, :]) * acc_ref[...]
        if final_norm:
            out = _rms(out) * fg_ref[...]
        o_ref[...] = out


def _ffn_call(h, hc, mod, g, w1, w3, w2, fg, *, n_rows, seg_rows, n_seg, tm, tf, final_norm, out_3d=False):
    D = h.shape[-1]
    FF = w1.shape[1]
    tpb = seg_rows // tm
    seg = lambda i, j: (jnp.minimum(i * tm // seg_rows, n_seg), 0, 0)
    if hc is None:
        n_lat_tiles = None
        h_specs = [pl.BlockSpec((tm, D), lambda i, j: (i, 0)), pl.BlockSpec((tm, D), lambda i, j: (0, 0))]
        hc = h
    else:
        tc = hc.shape[1]
        assert tm % tc == 0 and h.shape[1] == seg_rows
        n_lat_tiles = h.shape[0] * tpb
        lat = lambda i: jnp.minimum(i, n_lat_tiles - 1)
        h_specs = [pl.BlockSpec((None, tm, D), lambda i, j: (lat(i) // tpb, lat(i) % tpb, 0)),
                   pl.BlockSpec((tm // tc, tc, D), lambda i, j: (jnp.maximum(i - n_lat_tiles, 0), 0, 0))]
    if out_3d:
        assert n_rows % seg_rows == 0
        out_spec = pl.BlockSpec((None, tm, D), lambda i, j: (i // tpb, i % tpb, 0))
        out_shape = jax.ShapeDtypeStruct((n_rows // seg_rows, seg_rows, D), F32)
    else:
        out_spec = pl.BlockSpec((tm, D), lambda i, j: (i, 0))
        out_shape = jax.ShapeDtypeStruct((n_rows, D), F32)
    return pl.pallas_call(
        functools.partial(_ffn_kernel, final_norm=final_norm, n_lat_tiles=n_lat_tiles),
        grid=(n_rows // tm, FF // tf),
        in_specs=h_specs + [
            pl.BlockSpec((1, 8, D), seg),
            pl.BlockSpec((1, D), lambda i, j: (0, 0)),
            pl.BlockSpec((D, tf), lambda i, j: (0, j)),
            pl.BlockSpec((D, tf), lambda i, j: (0, j)),
            pl.BlockSpec((tf, D), lambda i, j: (j, 0)),
            pl.BlockSpec((1, D), lambda i, j: (0, 0)),
        ],
        out_specs=out_spec,
        out_shape=out_shape,
        scratch_shapes=[pltpu.VMEM((tm, D), BF16), pltpu.VMEM((tm, D), F32)],
        compiler_params=_cparams(("parallel", "arbitrary")),
        name="ffn",
    )(h, hc, mod, g, w1, w3, w2, fg)


def _inproj_kernel(h_ref, mod_ref, g_ref, w_ref, o_ref, hm_ref):
    @pl.when(pl.program_id(1) == 0)
    def _():
        hm_ref[...] = _modulate(h_ref[...], g_ref[...], mod_ref[0])

    o_ref[...] = _dot(hm_ref[...], w_ref[...])


def _inproj_call(h, mod, g, w, *, seg_rows, n_seg, tm, tn):
    R, D = h.shape
    N = w.shape[1]
    seg = lambda i, j: (jnp.minimum(i * tm // seg_rows, n_seg), 0, 0)
    return pl.pallas_call(
        _inproj_kernel,
        grid=(R // tm, N // tn),
        in_specs=[
            pl.BlockSpec((tm, D), lambda i, j: (i, 0)),
            pl.BlockSpec((1, 8, D), seg),
            pl.BlockSpec((1, D), lambda i, j: (0, 0)),
            pl.BlockSpec((D, tn), lambda i, j: (0, j)),
        ],
        out_specs=pl.BlockSpec((tm, tn), lambda i, j: (i, j)),
        out_shape=jax.ShapeDtypeStruct((R, N), F32),
        scratch_shapes=[pltpu.VMEM((tm, D), BF16)],
        compiler_params=_cparams(("parallel", "arbitrary")),
        name="inproj",
    )(h, mod, g, w)


def _outproj_kernel(xa_ref, xb_ref, wa_ref, wb_ref, h_ref, mod_ref, o_ref):
    y = _dot(xa_ref[...], wa_ref[...]) + _dot(xb_ref[...], wb_ref[...])
    o_ref[...] = h_ref[...] + mod_ref[0, 0:1, :] * y


def _outproj_call(xa, xb, wa, wb, h, mod, *, n_rows, seg_rows, n_seg, tm):
    D = h.shape[1]
    Ka, Kb = xa.shape[1], xb.shape[1]
    seg = lambda i: (jnp.minimum(i * tm // seg_rows, n_seg), 0, 0)
    return pl.pallas_call(
        _outproj_kernel,
        grid=(n_rows // tm,),
        in_specs=[
            pl.BlockSpec((tm, Ka), lambda i: (i, 0)),
            pl.BlockSpec((tm, Kb), lambda i: (i, 0)),
            pl.BlockSpec((Ka, D), lambda i: (0, 0)),
            pl.BlockSpec((Kb, D), lambda i: (0, 0)),
            pl.BlockSpec((tm, D), lambda i: (i, 0)),
            pl.BlockSpec((1, 8, D), seg),
        ],
        out_specs=pl.BlockSpec((tm, D), lambda i: (i, 0)),
        out_shape=jax.ShapeDtypeStruct((n_rows, D), F32),
        compiler_params=_cparams(("parallel",)),
        name="outproj",
    )(xa, xb, wa, wb, h, mod)


def _rope_tables(T, head_dim):
    half = head_dim // 2
    n_freq = head_dim // 4
    t = jnp.arange(T)
    r = (t // GRID_W).astype(F32)
    cidx = (t % GRID_W).astype(F32)
    inv = ROPE_THETA ** (-jnp.arange(n_freq, dtype=F32) / n_freq)
    ang = jnp.concatenate([r[:, None] * inv, cidx[:, None] * inv], axis=-1)
    cos, sin = jnp.cos(ang), jnp.sin(ang)
    cos_h = jnp.concatenate([cos, cos], axis=-1)
    sin_h = jnp.concatenate([-sin, sin], axis=-1)
    reps = LANE // head_dim
    return jnp.tile(cos_h, (1, reps)), jnp.tile(sin_h, (1, reps))


def _rope128(x, cos, sin):
    return x * cos + pltpu.roll(x, 64, 1) * sin


def _rope64(x, cos, sin):
    lane = lax.broadcasted_iota(jnp.int32, x.shape, 1)
    first = jnp.bitwise_and(lane, 63) < 32
    partner = jnp.where(first, pltpu.roll(x, 96, 1), pltpu.roll(x, 32, 1))
    return x * cos + partner * sin


def _attn_a_kernel(q_ref, k_ref, v_ref, kc_ref, vc_ref, cq_ref, sq_ref, ck_ref, sk_ref, gq_ref, gk_ref,
                   o_ref, kall, vall, *, tc, t, rk, kc):
    grp = A_HEADS // A_KV

    @pl.when(pl.program_id(2) == 0)
    def _():
        gk = gk_ref[...]
        kall[0:tc, :] = (_rms(kc_ref[...]) * gk).astype(BF16)
        vall[0:tc, 0:A_DH] = vc_ref[...].astype(BF16)
        vall[:, A_DH:] = jnp.ones((tc + t, A_DH), BF16)

        def body(c, carry):
            rows = pl.ds(pl.multiple_of(c * rk, rk), rk)
            kn = _rms(k_ref[rows, :]) * gk
            dst = pl.ds(pl.multiple_of(tc + c * rk, math.gcd(tc, rk)), rk)
            kall[dst, :] = _rope128(kn, ck_ref[rows, :], sk_ref[rows, :]).astype(BF16)
            vall[dst, 0:A_DH] = v_ref[rows, :].astype(BF16)
            return carry

        lax.fori_loop(0, t // rk, body, 0)

    q4 = q_ref[...]
    qs = jnp.concatenate([q4[:, h * A_DH:(h + 1) * A_DH] for h in range(grp)], axis=0)
    qn = _rms(qs) * gq_ref[...]
    cos = jnp.concatenate([cq_ref[...]] * grp, axis=0)
    sin = jnp.concatenate([sq_ref[...]] * grp, axis=0)
    qr = (_rope128(qn, cos, sin) * (A_DH ** -0.5 * LOG2E)).astype(BF16)
    m = l = acc = None
    for k0, kn in [(0, tc)] + [(tc + c * kc, kc) for c in range(t // kc)]:
        s = _dot_nt(qr, kall[k0:k0 + kn, :])
        smax = jnp.max(s, axis=-1, keepdims=True)
        if m is None:
            m = smax
            acc = _dot(jnp.exp2(s - m).astype(BF16), vall[k0:k0 + kn, :])
        else:
            m_new = jnp.maximum(m, smax)
            acc = jnp.exp2(m - m_new) * acc + _dot(jnp.exp2(s - m_new).astype(BF16), vall[k0:k0 + kn, :])
            m = m_new
    o = acc[:, 0:A_DH] / acc[:, A_DH:]
    tq = q4.shape[0]
    for h in range(grp):
        o_ref[:, h * A_DH:(h + 1) * A_DH] = o[h * tq:(h + 1) * tq, :].astype(o_ref.dtype)


def _attn_a_call(z, cos, sin, gq, gk, *, B, T, Tc):
    tq = min(2 * Q_BLOCK, T)
    nq = T // tq
    grp_w = (A_HEADS // A_KV) * A_DH
    kcol = A_HEADS
    cbase = B * T // Tc
    rk = min(512, T)
    return pl.pallas_call(
        functools.partial(_attn_a_kernel, tc=Tc, t=T, rk=rk, kc=min(256, T)),
        grid=(B, A_KV, nq),
        in_specs=[
            pl.BlockSpec((tq, grp_w), lambda b, kv, i: (b * nq + i, kv)),
            pl.BlockSpec((T, A_DH), lambda b, kv, i: (b, kcol + kv)),
            pl.BlockSpec((T, A_DH), lambda b, kv, i: (b, kcol + A_KV + kv)),
            pl.BlockSpec((Tc, A_DH), lambda b, kv, i: (cbase + b, kcol + kv)),
            pl.BlockSpec((Tc, A_DH), lambda b, kv, i: (cbase + b, kcol + A_KV + kv)),
            pl.BlockSpec((tq, LANE), lambda b, kv, i: (i, 0)),
            pl.BlockSpec((tq, LANE), lambda b, kv, i: (i, 0)),
            pl.BlockSpec((T, LANE), lambda b, kv, i: (0, 0)),
            pl.BlockSpec((T, LANE), lambda b, kv, i: (0, 0)),
            pl.BlockSpec((1, A_DH), lambda b, kv, i: (0, 0)),
            pl.BlockSpec((1, A_DH), lambda b, kv, i: (0, 0)),
        ],
        out_specs=pl.BlockSpec((tq, grp_w), lambda b, kv, i: (b * nq + i, kv)),
        out_shape=jax.ShapeDtypeStruct((B * T, A_HEADS * A_DH), BF16),
        scratch_shapes=[pltpu.VMEM((Tc + T, A_DH), BF16), pltpu.VMEM((Tc + T, 2 * A_DH), BF16)],
        compiler_params=_cparams(("parallel", "parallel", "arbitrary")),
        name="attn_a",
    )(z, z, z, z, z, cos, sin, cos, sin, gq, gk)


def _attn_a_ctx_kernel(q_ref, kc_ref, vc_ref, gq_ref, gk_ref, o_ref):
    grp = A_HEADS // A_KV
    q4 = q_ref[...]
    tq = q4.shape[0]
    qs = jnp.concatenate([q4[:, h * A_DH:(h + 1) * A_DH] for h in range(grp)], axis=0)
    qn = (_rms(qs) * gq_ref[...] * (A_DH ** -0.5)).astype(BF16)
    kc = (_rms(kc_ref[...]) * gk_ref[...]).astype(BF16)
    s = _dot_nt(qn, kc)
    p = jnp.exp(s - jnp.max(s, axis=-1, keepdims=True))
    l = jnp.sum(p, axis=-1, keepdims=True)
    o = _dot(p.astype(BF16), vc_ref[...].astype(BF16)) / l
    for h in range(grp):
        o_ref[:, h * A_DH:(h + 1) * A_DH] = o[h * tq:(h + 1) * tq, :].astype(o_ref.dtype)


def _attn_a_ctx_call(z, gq, gk, *, B, T, Tc):
    grp_w = (A_HEADS // A_KV) * A_DH
    kcol = A_HEADS
    cbase = B * T // Tc
    return pl.pallas_call(
        _attn_a_ctx_kernel,
        grid=(B, A_KV),
        in_specs=[
            pl.BlockSpec((Tc, grp_w), lambda b, kv: (cbase + b, kv)),
            pl.BlockSpec((Tc, A_DH), lambda b, kv: (cbase + b, kcol + kv)),
            pl.BlockSpec((Tc, A_DH), lambda b, kv: (cbase + b, kcol + A_KV + kv)),
            pl.BlockSpec((1, A_DH), lambda b, kv: (0, 0)),
            pl.BlockSpec((1, A_DH), lambda b, kv: (0, 0)),
        ],
        out_specs=pl.BlockSpec((Tc, grp_w), lambda b, kv: (b, kv)),
        out_shape=jax.ShapeDtypeStruct((B * Tc, A_HEADS * A_DH), BF16),
        compiler_params=_cparams(("parallel", "parallel")),
        name="attn_a_ctx",
    )(z, z, z, gq, gk)


MLSTM_CAUG = B_DV + 8


def _mlstm_local_kernel(q_ref, k_ref, v0_ref, v1_ref, gc_ref, gr_ref, bcol_ref, brow_ref,
                        nloc_ref, stat_ref, scal_ref, uaug_ref, *, cb):
    L = B_CHUNK
    row = lax.broadcasted_iota(jnp.int32, (L, L), 0)
    col = lax.broadcasted_iota(jnp.int32, (L, L), 1)
    gcol = (gc_ref[...] + bcol_ref[...]).reshape(cb, L, LANE)
    grow = gr_ref[...] + brow_ref[...][None]
    sub0 = lax.broadcasted_iota(jnp.int32, (cb, 8, B_DQK), 1) == 0
    n_stat = 3 * B_HEADS
    stat_ref[:, :, n_stat:] = jnp.zeros((2, cb * L, stat_ref.shape[2] - n_stat), F32)
    for hd in range(B_HEADS):
        q = q_ref[:, hd * B_DQK:(hd + 1) * B_DQK].reshape(cb, L, B_DQK)
        k = k_ref[:, hd * B_DQK:(hd + 1) * B_DQK].reshape(cb, L, B_DQK)
        v_ref = v0_ref if hd < 2 else v1_ref
        v = v_ref[:, (hd % 2) * B_DV:(hd % 2 + 1) * B_DV].reshape(cb, L, B_DV)
        qb = (q * (B_DQK ** -0.5)).astype(BF16)
        kb = k.astype(BF16)
        vb = v.astype(BF16)
        qk = jnp.einsum('cld,csd->cls', qb, kb, preferred_element_type=F32)
        for d in range(2):
            gi = 2 * d * B_HEADS + hd
            gf = gi + B_HEADS
            li_col = gcol[:, :, gi:gi + 1]
            lf_col = _log_sigmoid(gcol[:, :, gf:gf + 1])
            li_row = grow[:, gi:gi + 1, :]
            lf_row = _log_sigmoid(grow[:, gf:gf + 1, :])
            mask = ((col <= row) if d == 0 else (col >= row))[None]
            mask_t = ((row <= col) if d == 0 else (row >= col))[None]
            bcum_col = jnp.sum(jnp.where(mask, lf_row, 0.0), axis=2, keepdims=True)
            bcum_row = jnp.sum(jnp.where(mask_t, lf_col, 0.0), axis=1, keepdims=True)
            log_d = jnp.where(mask, bcum_col - bcum_row + li_row, -jnp.inf)
            a = jnp.max(log_d, axis=2, keepdims=True)
            s = qk * jnp.exp(log_d - a)
            nloc = jnp.einsum('cls,csv->clv', s.astype(BF16), vb, preferred_element_type=F32)
            dloc = jnp.sum(s, axis=2, keepdims=True)
            b_last = jnp.sum(lf_row, axis=2, keepdims=True)
            log_w = b_last - bcum_col + li_col
            b_loc = jnp.max(log_w, axis=1, keepdims=True)
            w = jnp.exp(log_w - b_loc)
            u = jnp.einsum('csv,csd->cvd', (w * v).astype(BF16), kb, preferred_element_type=F32)
            nu = jnp.sum(w * kb.astype(F32), axis=1, keepdims=True)
            nloc_ref[d, :, hd * B_DV:(hd + 1) * B_DV] = nloc.reshape(cb * L, B_DV)
            stat_ref[d, :, hd:hd + 1] = (a - bcum_col).reshape(cb * L, 1)
            stat_ref[d, :, B_HEADS + hd:B_HEADS + hd + 1] = dloc.reshape(cb * L, 1)
            stat_ref[d, :, 2 * B_HEADS + hd:2 * B_HEADS + hd + 1] = bcum_col.reshape(cb * L, 1)
            scal_ref[d, :, hd:hd + 1, :] = jnp.broadcast_to(b_last, (cb, 1, LANE))
            scal_ref[d, :, B_HEADS + hd:B_HEADS + hd + 1, :] = jnp.broadcast_to(b_loc, (cb, 1, LANE))
            uaug_ref[d, :, hd, 0:B_DV, :] = u
            uaug_ref[d, :, hd, B_DV:MLSTM_CAUG, :] = jnp.where(sub0, nu, 0.0)


def _mlstm_local_call(z, grow, bcol, brow, *, cb):
    L = B_CHUNK
    R = z.shape[0]
    nchunk = R // L
    W = B_HEADS * B_DV
    return pl.pallas_call(
        functools.partial(_mlstm_local_kernel, cb=cb),
        grid=(nchunk // cb,),
        in_specs=[
            pl.BlockSpec((cb * L, 512), lambda i: (i, 3)),
            pl.BlockSpec((cb * L, 512), lambda i: (i, 4)),
            pl.BlockSpec((cb * L, 512), lambda i: (i, 5)),
            pl.BlockSpec((cb * L, 512), lambda i: (i, 6)),
            pl.BlockSpec((cb * L, LANE), lambda i: (i, AB_GATE_COL // LANE)),
            pl.BlockSpec((cb, 16, L), lambda i: (i, 0, 0)),
            pl.BlockSpec((1, LANE), lambda i: (0, 0)),
            pl.BlockSpec((16, 1), lambda i: (0, 0)),
        ],
        out_specs=[
            pl.BlockSpec((2, cb * L, W), lambda i: (0, i, 0)),
            pl.BlockSpec((2, cb * L, 16), lambda i: (0, i, 0)),
            pl.BlockSpec((2, cb, 8, LANE), lambda i: (0, i, 0, 0)),
            pl.BlockSpec((2, cb, B_HEADS, MLSTM_CAUG, B_DQK), lambda i: (0, i, 0, 0, 0)),
        ],
        out_shape=[
            jax.ShapeDtypeStruct((2, R, W), F32),
            jax.ShapeDtypeStruct((2, R, 16), F32),
            jax.ShapeDtypeStruct((2, nchunk, 8, LANE), F32),
            jax.ShapeDtypeStruct((2, nchunk, B_HEADS, MLSTM_CAUG, B_DQK), F32),
        ],
        compiler_params=_cparams(("parallel",)),
        name="mlstm_local",
    )(z, z, z, z, z, grow, bcol, brow)


def _mlstm_scan_kernel(qf_ref, nf_ref, stf_ref, scf_ref, uf_ref, qb_ref, nb_ref, stb_ref, scb_ref, ub_ref,
                       of_ref, ob_ref, c_s, m_s):
    @pl.when(pl.program_id(1) == 0)
    def _():
        c_s[...] = jnp.zeros_like(c_s)
        m_s[...] = jnp.zeros_like(m_s)

    L = B_CHUNK
    units = []
    for d, (q_ref, n_ref, st_ref, sc_ref, u_ref, o_ref) in enumerate((
            (qf_ref, nf_ref, stf_ref, scf_ref, uf_ref, of_ref),
            (qb_ref, nb_ref, stb_ref, scb_ref, ub_ref, ob_ref))):
        st = st_ref[0]
        for hd in range(B_HEADS):
            u = d * B_HEADS + hd
            units.append((u, hd, q_ref, n_ref, st, sc_ref, u_ref, o_ref, c_s[u], m_s[u, 0:1, :]))
    new_state = []
    for u, hd, q_ref, n_ref, st, sc_ref, u_ref, o_ref, c_prev, m_row in units:
        rep = lambda j: jnp.broadcast_to(st[:, j:j + 1], (L, LANE))
        g, dloc, bcum = rep(hd), rep(B_HEADS + hd), rep(2 * B_HEADS + hd)
        m = jnp.broadcast_to(m_row, (L, LANE))
        e_intra = jnp.exp(jnp.minimum(g - m, 0.0))
        w_inter = jnp.exp(jnp.minimum(m - g, 0.0))
        floor = jnp.exp(-(bcum + jnp.maximum(m, g)))
        qb = (q_ref[:, hd * B_DQK:(hd + 1) * B_DQK] * (B_DQK ** -0.5)).astype(BF16)
        inter = _dot_nt(qb, c_prev.astype(BF16))
        den = e_intra * dloc + w_inter * inter[:, B_DV:B_DV + LANE]
        r = 1.0 / jnp.maximum(jnp.abs(den), floor)
        for j in range(B_DV // LANE):
            c0 = hd * B_DV + j * LANE
            num = e_intra * n_ref[0, :, c0:c0 + LANE] + w_inter * inter[:, j * LANE:(j + 1) * LANE]
            o_ref[0, :, c0:c0 + LANE] = num * r
        b_last = sc_ref[0, 0, hd:hd + 1, :]
        b_loc = sc_ref[0, 0, B_HEADS + hd:B_HEADS + hd + 1, :]
        m_new = jnp.maximum(b_last + m_row, b_loc)
        decay = jnp.exp(b_last + m_row - m_new)
        scl = jnp.exp(b_loc - m_new)
        c_new = decay * c_prev[0:B_DV] + scl * u_ref[0, 0, hd, 0:B_DV, :]
        n_new = decay * c_prev[B_DV:] + scl * jnp.broadcast_to(u_ref[0, 0, hd, B_DV:B_DV + 1, :], (LANE, B_DQK))
        new_state.append((u, c_new, n_new, m_new))
    for u, c_new, n_new, m_new in new_state:
        c_s[u, 0:B_DV, :] = c_new
        c_s[u, B_DV:, :] = n_new
        m_s[u] = jnp.broadcast_to(m_new, m_s.shape[1:])


def _mlstm_scan_call(z, nloc, stat, scal, uaug, *, B, T, Tc):
    L = B_CHUNK
    ncl, ncc = T // L, Tc // L
    nc = ncl + ncc
    cbase = B * T // L
    R = z.shape[0]
    W = B_HEADS * B_DV

    def blk(b, c, d):
        is_ctx = c < ncc
        jc = c if d == 0 else ncc - 1 - c
        jl = (c - ncc) if d == 0 else (ncl - 1 - (c - ncc))
        return jnp.where(is_ctx, cbase + b * ncc + jc, b * ncl + jl)

    def specs(d):
        return [
            pl.BlockSpec((L, 512), lambda b, c: (blk(b, c, d), 3)),
            pl.BlockSpec((1, L, W), lambda b, c: (d, blk(b, c, d), 0)),
            pl.BlockSpec((1, L, 16), lambda b, c: (d, blk(b, c, d), 0)),
            pl.BlockSpec((1, 1, 8, LANE), lambda b, c: (d, blk(b, c, d), 0, 0)),
            pl.BlockSpec((1, 1, B_HEADS, MLSTM_CAUG, B_DQK), lambda b, c: (d, blk(b, c, d), 0, 0, 0)),
        ]

    out_sds = jax.ShapeDtypeStruct((1, R, W), F32)
    return pl.pallas_call(
        _mlstm_scan_kernel,
        grid=(B, nc),
        in_specs=specs(0) + specs(1),
        out_specs=[
            pl.BlockSpec((1, L, W), lambda b, c: (0, blk(b, c, 0), 0)),
            pl.BlockSpec((1, L, W), lambda b, c: (0, blk(b, c, 1), 0)),
        ],
        out_shape=[out_sds, out_sds],
        scratch_shapes=[
            pltpu.VMEM((2 * B_HEADS, B_DV + LANE, B_DQK), F32),
            pltpu.VMEM((2 * B_HEADS, 8, LANE), F32),
        ],
        compiler_params=_cparams(("parallel", "arbitrary")),
        name="mlstm_scan",
    )(z, nloc, stat, scal, uaug, z, nloc, stat, scal, uaug)


def _mlstm_finish_kernel(hf_ref, hb_ref, o0_ref, o1_ref, hg_ref, y_ref):
    for hd in range(B_HEADS):
        sl = slice(hd * B_DV, (hd + 1) * B_DV)
        hh = hf_ref[0, :, sl] + hb_ref[0, :, sl]
        o_ref = o0_ref if hd < 2 else o1_ref
        o = o_ref[:, (hd % 2) * B_DV:(hd % 2 + 1) * B_DV]
        y_ref[:, sl] = (_sigmoid(o) * _rms(hh) * hg_ref[:, sl]).astype(y_ref.dtype)


def _mlstm_finish_call(hf, hb, z, head_g, *, tm):
    R = z.shape[0]
    W = B_HEADS * B_DV
    return pl.pallas_call(
        _mlstm_finish_kernel,
        grid=(R // tm,),
        in_specs=[
            pl.BlockSpec((1, tm, W), lambda i: (0, i, 0)),
            pl.BlockSpec((1, tm, W), lambda i: (0, i, 0)),
            pl.BlockSpec((tm, 512), lambda i: (i, 7)),
            pl.BlockSpec((tm, 512), lambda i: (i, 8)),
            pl.BlockSpec((1, W), lambda i: (0, 0)),
        ],
        out_specs=pl.BlockSpec((tm, W), lambda i: (i, 0)),
        out_shape=jax.ShapeDtypeStruct((R, W), BF16),
        compiler_params=_cparams(("parallel",)),
        name="mlstm_finish",
    )(hf, hb, z, z, head_g)


def _s5_params(a_re, a_im, log_dt, b_re, b_im, c_re, c_im, reverse):
    hp = lax.Precision.HIGHEST
    Lc = S5_CHUNK
    dt = jnp.exp(log_dt)[:, None]
    lam_re, lam_im = a_re * dt, a_im * dt
    mag = jnp.exp(lam_re)
    ab_re, ab_im = mag * jnp.cos(lam_im), mag * jnp.sin(lam_im)
    den = a_re * a_re + a_im * a_im
    nr, ni = ab_re - 1.0, ab_im
    k_re, k_im = (nr * a_re + ni * a_im) / den, (ni * a_re - nr * a_im) / den
    bb_re = k_re[..., None] * b_re - k_im[..., None] * b_im
    bb_im = k_re[..., None] * b_im + k_im[..., None] * b_re
    tau = jnp.arange(Lc + 1, dtype=F32)[:, None, None]
    pm = jnp.exp(tau * lam_re)
    pw_re, pw_im = pm * jnp.cos(tau * lam_im), pm * jnp.sin(tau * lam_im)
    cp_re = c_re[None] * pw_re[:, :, None, :] - c_im[None] * pw_im[:, :, None, :]
    cp_im = c_re[None] * pw_im[:, :, None, :] + c_im[None] * pw_re[:, :, None, :]
    kk = (jnp.einsum('tgop,gpi->tgoi', cp_re, bb_re, precision=hp)
          - jnp.einsum('tgop,gpi->tgoi', cp_im, bb_im, precision=hp))
    G = a_re.shape[0]
    i_idx = jnp.arange(Lc)[:, None]
    j_idx = jnp.arange(Lc)[None, :]
    lag = (i_idx - j_idx) if reverse else (j_idx - i_idx)
    valid = lag >= 0
    kt = jnp.where(valid[:, :, None, None, None], kk[jnp.clip(lag, 0, Lc)], 0.0)
    kt = kt.transpose(2, 0, 4, 1, 3).reshape(G, Lc * C_GROUP, Lc * C_GROUP)
    e = jnp.arange(Lc) if reverse else (Lc - 1 - jnp.arange(Lc))
    w_re = pw_re[e][:, :, :, None] * bb_re[None] - pw_im[e][:, :, :, None] * bb_im[None]
    w_im = pw_re[e][:, :, :, None] * bb_im[None] + pw_im[e][:, :, :, None] * bb_re[None]
    w_re = w_re.transpose(1, 0, 3, 2).reshape(G, Lc * C_GROUP, C_STATE)
    w_im = w_im.transpose(1, 0, 3, 2).reshape(G, Lc * C_GROUP, C_STATE)
    even = (jnp.arange(G) % 2 == 0)[:, None, None]
    zc = jnp.zeros_like(w_re)
    half_cols = lambda m: jnp.where(even, jnp.concatenate([m, zc], axis=-1), jnp.concatenate([zc, m], axis=-1))
    wre, wim = half_cols(w_re), half_cols(w_im)
    f = (Lc - jnp.arange(Lc)) if reverse else (jnp.arange(Lc) + 1)
    cj_re = cp_re[f].transpose(1, 3, 0, 2).reshape(G, C_STATE, Lc * C_GROUP)
    cj_im = -cp_im[f].transpose(1, 3, 0, 2).reshape(G, C_STATE, Lc * C_GROUP)
    zr = jnp.zeros_like(cj_re)
    half_rows = lambda m: jnp.where(even, jnp.concatenate([m, zr], axis=1), jnp.concatenate([zr, m], axis=1))
    cre, cim = half_rows(cj_re), half_rows(cj_im)
    pair = lambda v: v.reshape(G // 2, 1, 2 * C_STATE)
    are, aim = pair(pw_re[Lc]), pair(pw_im[Lc])
    wcat = jnp.concatenate([wre, wim], axis=-1)
    ccat = jnp.concatenate([cre, cim], axis=1)
    return kt.astype(BF16), wcat.astype(BF16), ccat.astype(BF16), are, aim


def _granule_transpose(xs):
    xs = list(xs)
    gran = jnp.right_shift(lax.broadcasted_iota(jnp.int32, (1, LANE), 1), C_GROUP.bit_length() - 1)
    for bit in (1, 2, 4):
        hi = jnp.bitwise_and(gran, bit) != 0
        for a in range(S5_GB):
            if a & bit:
                continue
            p = a | bit
            xa, xp = xs[a], xs[p]
            xs[a] = jnp.where(hi, pltpu.roll(xp, bit * C_GROUP, 1), xa)
            xs[p] = jnp.where(hi, xp, pltpu.roll(xa, LANE - bit * C_GROUP, 1))
    return xs


def _s5_kernel(ul_ref, uc_ref, kt_ref, wcat_ref, ccat_ref, are_ref, aim_ref, y_ref,
               ug_s, wr_s, wi_s, xr_s, xi_s, y_s, *, nb, ncc, ncl):
    Lc = S5_CHUNK
    lat0 = nb * ncc
    for src, row0, n in ((uc_ref, 0, nb * ncc), (ul_ref, lat0, nb * ncl)):
        folded = _granule_transpose([src[pl.ds(i, n, stride=Lc), :] for i in range(Lc)])
        for g in range(S5_GB):
            ug_s[g, row0:row0 + n, :] = folded[g].astype(BF16)

    npair = S5_GB // 2
    for d in range(2):
        for k in range(npair):
            u0, u1 = ug_s[2 * k], ug_s[2 * k + 1]
            w = _dot(u0, wcat_ref[d, 2 * k]) + _dot(u1, wcat_ref[d, 2 * k + 1])
            wr_s[k] = w[:, 0:LANE]
            wi_s[k] = w[:, LANE:]
        are = [jnp.broadcast_to(are_ref[d, k], (nb, LANE)) for k in range(npair)]
        aim = [jnp.broadcast_to(aim_ref[d, k], (nb, LANE)) for k in range(npair)]

        def make_body(base, n):
            def body(s, carry):
                c = s if d == 0 else n - 1 - s
                rows = pl.ds(base + c, nb, stride=n)
                new = []
                for k in range(npair):
                    xr, xi = carry[2 * k], carry[2 * k + 1]
                    xr_s[k, rows, :] = xr
                    xi_s[k, rows, :] = xi
                    new.append(are[k] * xr - aim[k] * xi + wr_s[k, rows, :])
                    new.append(are[k] * xi + aim[k] * xr + wi_s[k, rows, :])
                return tuple(new)
            return body

        state = tuple(jnp.zeros((nb, LANE), F32) for _ in range(2 * npair))
        state = lax.fori_loop(0, ncc, make_body(0, ncc), state, unroll=4)
        lax.fori_loop(0, ncl, make_body(lat0, ncl), state, unroll=4)
        for g in range(S5_GB):
            y = (_dot(ug_s[g, lat0:, :], kt_ref[d, g])
                 + _dot(jnp.concatenate([xr_s[g // 2, lat0:, :], xi_s[g // 2, lat0:, :]], axis=1).astype(BF16),
                        ccat_ref[d, g]))
            if d == 0:
                y_s[g] = y
            else:
                y_s[g] += y

    unfolded = _granule_transpose([y_s[g] for g in range(S5_GB)])
    for j in range(Lc):
        y_ref[pl.ds(j, nb * ncl, stride=Lc), :] = unfolded[j]


def _s5_call(z, kt, wcat, ccat, are, aim, *, B, T, Tc, nb):
    Lc = S5_CHUNK
    ncl, ncc = T // Lc, Tc // Lc
    nr = nb * (ncl + ncc)
    cbase = B * T // (nb * Tc)
    wspec = pl.BlockSpec((2, S5_GB, LANE, LANE), lambda i, h: (0, i, 0, 0))
    aspec = pl.BlockSpec((2, S5_GB // 2, 1, LANE), lambda i, h: (0, i, 0, 0))
    big = lambda dt: pltpu.VMEM((S5_GB, nr, LANE), dt)
    half = pltpu.VMEM((S5_GB // 2, nr, LANE), F32)
    return pl.pallas_call(
        functools.partial(_s5_kernel, nb=nb, ncc=ncc, ncl=ncl),
        grid=(C_GROUPS // S5_GB, B // nb),
        in_specs=[
            pl.BlockSpec((nb * T, LANE), lambda i, h: (h, i)),
            pl.BlockSpec((nb * Tc, LANE), lambda i, h: (cbase + h, i)),
            wspec,
            pl.BlockSpec((2, S5_GB, LANE, 2 * LANE), lambda i, h: (0, i, 0, 0)),
            pl.BlockSpec((2, S5_GB, 2 * LANE, LANE), lambda i, h: (0, i, 0, 0)),
            aspec, aspec],
        out_specs=pl.BlockSpec((nb * T, LANE), lambda i, h: (h, i)),
        out_shape=jax.ShapeDtypeStruct((B * T, C_WIDTH), F32),
        scratch_shapes=[big(BF16), half, half, half, half,
                        pltpu.VMEM((S5_GB, nb * ncl, LANE), F32)],
        compiler_params=_cparams(("parallel", "parallel")),
        name="s5_scan",
    )(z, z, kt, wcat, ccat, are, aim)


def _s5_finish_kernel(y_ref, u_ref, d_ref, w_ref, b_ref, o_ref):
    y = y_ref[...] + d_ref[...] * u_ref[...]
    g = 0.5 * y * (1.0 + jnp.tanh(math.sqrt(2.0 / math.pi) * (y + 0.044715 * (y * y * y))))
    o_ref[...] = (g * _sigmoid(_dot(g.astype(BF16), w_ref[...]) + b_ref[...])).astype(o_ref.dtype)


def _s5_finish_call(y, z, d_skip, glu_w, glu_b, *, n_rows, tm):
    W = C_WIDTH
    return pl.pallas_call(
        _s5_finish_kernel,
        grid=(n_rows // tm,),
        in_specs=[
            pl.BlockSpec((tm, W), lambda i: (i, 0)),
            pl.BlockSpec((tm, W), lambda i: (i, 0)),
            pl.BlockSpec((1, W), lambda i: (0, 0)),
            pl.BlockSpec((W, W), lambda i: (0, 0)),
            pl.BlockSpec((1, W), lambda i: (0, 0)),
        ],
        out_specs=pl.BlockSpec((tm, W), lambda i: (i, 0)),
        out_shape=jax.ShapeDtypeStruct((n_rows, W), BF16),
        compiler_params=_cparams(("parallel",)),
        name="s5_finish",
    )(y, z, d_skip, glu_w, glu_b)


def _ones_lane(par):
    return D_DH if par == 0 else 0


def _attn_d_kernel(q_ref, k_ref, v_ref, kc_ref, vc_ref, cq_ref, sq_ref, ck_ref, sk_ref, sink_ref,
                   o_ref, kpe, kpo, vpe, vpo, kce, kco, vce, vco, band, *, t, tc, rk):
    W = D_WINDOW
    kv = pl.program_id(1)
    i = pl.program_id(2)
    lane = lax.broadcasted_iota(jnp.int32, (1, LANE), 1)
    lo = lane < D_DH

    def halves(x, ones=False):
        sel = jnp.where(kv == 0, x, pltpu.roll(x, 64, 1))
        e = jnp.where(lo, sel, 0.0)
        o = pltpu.roll(e, 64, 1)
        if ones:
            e = jnp.where(lane == _ones_lane(0), 1.0, e)
            o = jnp.where(lane == _ones_lane(1), 1.0, o)
        return e.astype(BF16), o.astype(BF16)

    @pl.when(i == 0)
    def _():
        zpad = jnp.zeros((W, LANE), BF16)
        for ref in (kpe, kpo, vpe, vpo):
            ref[0:W, :] = zpad
            ref[W + t:W + t + W, :] = zpad
        kce[...], kco[...] = halves(kc_ref[...])
        vce[...], vco[...] = halves(vc_ref[...], ones=True)

        def body(c, carry):
            rows = pl.ds(pl.multiple_of(c * rk, rk), rk)
            dst = pl.ds(pl.multiple_of(W + c * rk, math.gcd(W, rk)), rk)
            kpe[dst, :], kpo[dst, :] = halves(_rope64(k_ref[rows, :], ck_ref[rows, :], sk_ref[rows, :]))
            vpe[dst, :], vpo[dst, :] = halves(v_ref[rows, :], ones=True)
            return carry

        lax.fori_loop(0, t // rk, body, 0)

    tq = q_ref.shape[0]
    nslab = q_ref.shape[1] // LANE
    span = tq + 2 * W

    @pl.when(i == 0)
    def _():
        rr = jnp.bitwise_and(lax.broadcasted_iota(jnp.int32, (nslab * tq, span), 0), tq - 1)
        cc = lax.broadcasted_iota(jnp.int32, (nslab * tq, span), 1)
        band[...] = jnp.where(jnp.abs(rr + W - cc) <= W, 0.0, -jnp.inf)

    cos, sin = cq_ref[...], sq_ref[...]
    qs = jnp.concatenate(
        [_rope64(q_ref[:, j * LANE:(j + 1) * LANE], cos, sin) for j in range(nslab)], axis=0)
    qs = (qs * (D_DH ** -0.5 * LOG2E)).astype(BF16)
    rows_k = pl.ds(pl.multiple_of(i * tq, tq), span)
    kpos = i * tq - W + lax.broadcasted_iota(jnp.int32, (1, span), 1)
    bias = band[...] + jnp.where((kpos >= 0) & (kpos < t), 0.0, -jnp.inf)
    out = None
    lane_o = lax.broadcasted_iota(jnp.int32, (1, LANE), 1)
    for par, (kp, vp, kcx, vcx) in enumerate(((kpe, vpe, kce, vce), (kpo, vpo, kco, vco))):
        s = jnp.concatenate([_dot_nt(qs, kcx[...]), _dot_nt(qs, kp[rows_k, :]) + bias], axis=1)
        snk = LOG2E * jnp.concatenate(
            [jnp.broadcast_to(sink_ref[0, 0:1, 2 * j + par:2 * j + par + 1], (tq, 1)) for j in range(nslab)],
            axis=0)
        m = jnp.maximum(jnp.max(s, axis=-1, keepdims=True), snk)
        p = jnp.exp2(s - m).astype(BF16)
        o = _dot(p[:, 0:tc], vcx[...]) + _dot(p[:, tc:], vp[rows_k, :])
        one = _ones_lane(par)
        l = o[:, one:one + 1] + jnp.exp2(snk - m)
        own = (lane_o < D_DH) if par == 0 else (lane_o >= D_DH)
        o = jnp.where(own, o, 0.0) / l
        out = o if out is None else out + o
    for j in range(nslab):
        o_ref[:, j * LANE:(j + 1) * LANE] = out[j * tq:(j + 1) * tq, :].astype(o_ref.dtype)


def _attn_d_call(z, cos, sin, sink, *, B, T, Tc):
    tq = min(2 * Q_BLOCK, T)
    nq = T // tq
    grp_w = (D_HEADS // D_KV) * D_DH
    qcol = C_WIDTH // grp_w
    kcol = (C_WIDTH + D_HEADS * D_DH) // LANE
    cbase = B * T // Tc
    rk = min(512, T)
    pad = lambda n: pltpu.VMEM((n, LANE), BF16)
    return pl.pallas_call(
        functools.partial(_attn_d_kernel, t=T, tc=Tc, rk=rk),
        grid=(B, D_KV, nq),
        in_specs=[
            pl.BlockSpec((tq, grp_w), lambda b, kv, i: (b * nq + i, qcol + kv)),
            pl.BlockSpec((T, LANE), lambda b, kv, i: (b, kcol)),
            pl.BlockSpec((T, LANE), lambda b, kv, i: (b, kcol + 1)),
            pl.BlockSpec((Tc, LANE), lambda b, kv, i: (cbase + b, kcol)),
            pl.BlockSpec((Tc, LANE), lambda b, kv, i: (cbase + b, kcol + 1)),
            pl.BlockSpec((tq, LANE), lambda b, kv, i: (i, 0)),
            pl.BlockSpec((tq, LANE), lambda b, kv, i: (i, 0)),
            pl.BlockSpec((T, LANE), lambda b, kv, i: (0, 0)),
            pl.BlockSpec((T, LANE), lambda b, kv, i: (0, 0)),
            pl.BlockSpec((1, 1, LANE), lambda b, kv, i: (kv, 0, 0)),
        ],
        out_specs=pl.BlockSpec((tq, grp_w), lambda b, kv, i: (b * nq + i, kv)),
        out_shape=jax.ShapeDtypeStruct((B * T, D_HEADS * D_DH), BF16),
        scratch_shapes=[pad(T + 2 * D_WINDOW)] * 4 + [pad(Tc)] * 4
        + [pltpu.VMEM((grp_w // LANE * tq, tq + 2 * D_WINDOW), F32)],
        compiler_params=_cparams(("parallel", "parallel", "arbitrary")),
        name="attn_d",
    )(z, z, z, z, z, cos, sin, cos, sin, sink)


def _mod_rows(ml, idx):
    sel = ml[:, jnp.array(idx), :]
    return jnp.pad(sel, ((0, 0), (0, 8 - len(idx)), (0, 0)))


def kernel(x, c, ctx, c_ctx, mod_w, mod_b, norm_g, ffn_w1, ffn_w3, ffn_w2, ab_w_in, ab_b_gate, ab_w_out, a_gq, a_gk, b_head_g, cd_w_in, cd_w_out, s5_a_re, s5_a_im, s5_log_dt, s5_b_re, s5_b_im, s5_c_re, s5_c_im, s5_d, s5_glu_w, s5_glu_b, d_sink, final_g):
    B, T, D = x.shape
    Tc = ctx.shape[1]
    depth = mod_w.shape[0]
    n_lat, n_ctx = B * T, B * Tc
    n_all = n_lat + n_ctx
    tm = math.gcd(math.gcd(T, n_ctx), 512)
    tf = math.gcd(ffn_w1.shape[-1], 512)
    seg = dict(seg_rows=T, n_seg=B)

    h, hc0 = x, ctx
    cvec = jnp.zeros((8, D), F32).at[:B].set(c).at[B].set(c_ctx)
    mods = _modvec_call(cvec, mod_w, mod_b).reshape(depth, 8, N_MOD, D)[:, :B + 1]
    cos_a, sin_a = _rope_tables(T, A_DH)
    cos_d, sin_d = _rope_tables(T, D_DH)
    w1, w3, w2 = ffn_w1.astype(BF16), ffn_w3.astype(BF16), ffn_w2.astype(BF16)
    ones = jnp.ones((1, D), F32)

    for l in range(depth):
        last = l == depth - 1
        want_ctx = not last
        ml = mods[l]
        i = l // 2
        h = _ffn_call(h, hc0 if l == 0 else None, _mod_rows(ml, (0, 1, 2)), norm_g[l, 0][None],
                      w1[l, 0], w3[l, 0], w2[l, 0], ones,
                      n_rows=n_all, tm=tm, tf=tf, final_norm=False, **seg)
        mod_in = _mod_rows(ml, (3, 4))
        mod_out = _mod_rows(ml, (5,))
        if l % 2 == 0:
            w_in = jnp.pad(ab_w_in[i], ((0, 0), (0, AB_IN_PAD - ab_w_in.shape[-1]))).astype(BF16)
            z = _inproj_call(h, mod_in, norm_g[l, 1][None], w_in, tm=tm, tn=1024, **seg)
            ya = _attn_a_call(z, cos_a, sin_a, a_gq[i][None], a_gk[i][None], B=B, T=T, Tc=Tc)
            yac = _attn_a_ctx_call(z, a_gq[i][None], a_gk[i][None], B=B, T=T, Tc=Tc)
            ng = 4 * B_HEADS
            gates = z[:, AB_GATE_COL:AB_GATE_COL + ng]
            grow = gates.reshape(n_all // B_CHUNK, B_CHUNK, ng).transpose(0, 2, 1)
            bcol = jnp.pad(ab_b_gate[i], (0, LANE - ng))[None]
            brow = ab_b_gate[i][:, None]
            nloc, stat, scal, uaug = _mlstm_local_call(z, grow, bcol, brow, cb=math.gcd(n_all // B_CHUNK, 8))
            hf, hb = _mlstm_scan_call(z, nloc, stat, scal, uaug, B=B, T=T, Tc=Tc)
            yb = _mlstm_finish_call(hf, hb, z, b_head_g[i][None], tm=tm)
            xa = jnp.concatenate([ya, yac], axis=0)
            w_out = ab_w_out[i].astype(BF16)
            ka = A_HEADS * A_DH
            n_rows = n_all
        else:
            w_in = cd_w_in[i].astype(BF16)
            z = _inproj_call(h, mod_in, norm_g[l, 1][None], w_in, tm=tm, tn=w_in.shape[1] // 3, **seg)
            prm = [_s5_params(s5_a_re[i, d], s5_a_im[i, d], s5_log_dt[i, d], s5_b_re[i, d], s5_b_im[i, d],
                              s5_c_re[i, d], s5_c_im[i, d], reverse=(d == 1)) for d in range(2)]
            s5w = [jnp.stack([prm[0][k], prm[1][k]]) for k in range(len(prm[0]))]
            ys5 = _s5_call(z, *s5w, B=B, T=T, Tc=Tc, nb=math.gcd(B, 2))
            xa = _s5_finish_call(ys5, z, s5_d[i][None], s5_glu_w[i].astype(BF16), s5_glu_b[i][None],
                                 n_rows=n_lat, tm=tm)
            sink = jnp.pad(d_sink[i].reshape(D_KV, 1, D_HEADS // D_KV), ((0, 0), (0, 0), (0, LANE - D_HEADS // D_KV)))
            yb = _attn_d_call(z, cos_d, sin_d, sink, B=B, T=T, Tc=Tc)
            w_out = cd_w_out[i].astype(BF16)
            ka = C_WIDTH
            n_rows = n_all if want_ctx else n_lat
        if not want_ctx:
            n_rows = n_lat
        h = _outproj_call(xa, yb, w_out[:ka], w_out[ka:], h, mod_out, n_rows=n_rows, tm=tm, **seg)
        h = _ffn_call(h, None, _mod_rows(ml, (6, 7, 8)), norm_g[l, 2][None], w1[l, 1], w3[l, 1], w2[l, 1],
                      final_g[None] if last else ones, n_rows=n_rows, tm=tm, tf=tf, final_norm=last,
                      out_3d=last, **seg)
    return h
```

```python
import functools
import math

import jax
import jax.numpy as jnp
from jax import lax
from jax.experimental import pallas as pl
from jax.experimental.pallas import tpu as pltpu

F32 = jnp.float32
BF16 = jnp.bfloat16

EPS = 1e-6
ROPE_THETA = 10000.0
GRID_W = 64
N_MOD = 9
Q_BLOCK = 128

A_HEADS, A_KV, A_DH = 8, 2, 128
B_HEADS, B_DQK, B_DV, B_CHUNK = 4, 128, 256, 64
C_WIDTH, C_GROUP, C_STATE = 1024, 16, 64
C_GROUPS = C_WIDTH // C_GROUP
D_HEADS, D_KV, D_DH, D_WINDOW = 16, 2, 64, 128

AB_GATE_COL = 4608
AB_IN_PAD = 5120
S5_CHUNK = 8
S5_GB = 8

LANE = 128
LOG2E = math.log2(math.e)
VMEM_LIMIT = 56 * 1024 * 1024


def _cparams(sem, vmem_limit=VMEM_LIMIT):
    return pltpu.CompilerParams(dimension_semantics=sem, vmem_limit_bytes=vmem_limit)


def _rms(x):
    return x * lax.rsqrt(jnp.mean(x * x, axis=-1, keepdims=True) + EPS)


def _sigmoid(x):
    return 1.0 / (1.0 + jnp.exp(-x))


def _log_sigmoid(x):
    return jnp.minimum(x, 0.0) - jnp.log(1.0 + jnp.exp(-jnp.abs(x)))


def _dot(a, b):
    return jnp.dot(a, b, preferred_element_type=F32)


def _dot_nt(a, b):
    return lax.dot_general(a, b, (((1,), (1,)), ((), ())), preferred_element_type=F32)


def _dot_tn(a, b):
    return lax.dot_general(a, b, (((0,), (0,)), ((), ())), preferred_element_type=F32)


def _modvec_kernel(c_ref, w_ref, b_ref, o_ref):
    c = c_ref[...]
    s = (c * _sigmoid(c)).astype(BF16)
    o_ref[0] = _dot(s, w_ref[0].astype(BF16)) + b_ref[0]


def _modvec_call(cvec, mod_w, mod_b):
    L, D, N = mod_w.shape
    tn = math.gcd(D, 1024)
    return pl.pallas_call(
        _modvec_kernel,
        grid=(L, N // tn),
        in_specs=[
            pl.BlockSpec((8, D), lambda l, j: (0, 0)),
            pl.BlockSpec((1, D, tn), lambda l, j: (l, 0, j)),
            pl.BlockSpec((1, 1, tn), lambda l, j: (l, 0, j)),
        ],
        out_specs=pl.BlockSpec((1, 8, tn), lambda l, j: (l, 0, j)),
        out_shape=jax.ShapeDtypeStruct((L, 8, N), F32),
        compiler_params=_cparams(("parallel", "parallel")),
        name="modvec",
    )(cvec, mod_w, mod_b.reshape(L, 1, N))


def _modulate(h, g, mod):
    return (_rms(h) * g * (1.0 + mod[1:2, :]) + mod[0:1, :]).astype(BF16)


def _ffn_kernel(h_ref, hc_ref, mod_ref, g_ref, w1_ref, w3_ref, w2_ref, fg_ref, o_ref, hm_ref, acc_ref, *,
                final_norm, n_lat_tiles):
    j = pl.program_id(1)
    tm, d = o_ref.shape

    def load_h():
        if n_lat_tiles is None:
            return h_ref[...]
        return jnp.where(pl.program_id(0) < n_lat_tiles, h_ref[...], hc_ref[...].reshape(tm, d))

    @pl.when(j == 0)
    def _():
        hm_ref[...] = _modulate(load_h(), g_ref[...], mod_ref[0])
        acc_ref[...] = jnp.zeros_like(acc_ref)

    hm = hm_ref[...]
    a = _dot(hm, w1_ref[...])
    b = _dot(hm, w3_ref[...])
    gg = (a * _sigmoid(a)) * b
    acc_ref[...] += _dot(gg.astype(BF16), w2_ref[...])

    @pl.when(j == pl.num_programs(1) - 1)
    def _():
        out = load_h() + (0.5 * mod_ref[0, 2:3, :]) * acc_ref[...]
        if final_norm:
            out = _rms(out) * fg_ref[...]
        o_ref[...] = out


def _ffn_call(h, hc, mod, g, w1, w3, w2, fg, *, wsel, n_rows, seg_rows, n_seg, tm, tf, final_norm, out_3d=False):
    D = h.shape[-1]
    FF = w1.shape[-1]
    wl, wk = wsel
    tpb = seg_rows // tm
    seg = lambda i, j: (jnp.minimum(i * tm // seg_rows, n_seg), 0, 0)
    if hc is None:
        n_lat_tiles = None
        h_specs = [pl.BlockSpec((tm, D), lambda i, j: (i, 0)), pl.BlockSpec((8, D), lambda i, j: (0, 0))]
        hc = h
    else:
        tc = hc.shape[1]
        assert tm % tc == 0 and h.shape[1] == seg_rows
        n_lat_tiles = h.shape[0] * tpb
        lat = lambda i: jnp.minimum(i, n_lat_tiles - 1)
        h_specs = [pl.BlockSpec((None, tm, D), lambda i, j: (lat(i) // tpb, lat(i) % tpb, 0)),
                   pl.BlockSpec((tm // tc, tc, D), lambda i, j: (jnp.maximum(i - n_lat_tiles, 0), 0, 0))]
    if out_3d:
        assert n_rows % seg_rows == 0
        out_spec = pl.BlockSpec((None, tm, D), lambda i, j: (i // tpb, i % tpb, 0))
        out_shape = jax.ShapeDtypeStruct((n_rows // seg_rows, seg_rows, D), F32)
    else:
        out_spec = pl.BlockSpec((tm, D), lambda i, j: (i, 0))
        out_shape = jax.ShapeDtypeStruct((n_rows, D), F32)
    return pl.pallas_call(
        functools.partial(_ffn_kernel, final_norm=final_norm, n_lat_tiles=n_lat_tiles),
        grid=(n_rows // tm, FF // tf),
        in_specs=h_specs + [
            pl.BlockSpec((1, 8, D), seg),
            pl.BlockSpec((1, D), lambda i, j: (0, 0)),
            pl.BlockSpec((None, None, D, tf), lambda i, j: (wl, wk, 0, j)),
            pl.BlockSpec((None, None, D, tf), lambda i, j: (wl, wk, 0, j)),
            pl.BlockSpec((None, None, tf, D), lambda i, j: (wl, wk, j, 0)),
            pl.BlockSpec((1, D), lambda i, j: (0, 0)),
        ],
        out_specs=out_spec,
        out_shape=out_shape,
        scratch_shapes=[pltpu.VMEM((tm, D), BF16), pltpu.VMEM((tm, D), F32)],
        compiler_params=_cparams(("parallel", "arbitrary")),
        name="ffn",
    )(h, hc, mod, g, w1, w3, w2, fg)


def _inproj_kernel(h_ref, mod_ref, g_ref, w_ref, o_ref, hm_ref):
    @pl.when(pl.program_id(1) == 0)
    def _():
        hm_ref[...] = _modulate(h_ref[...], g_ref[...], mod_ref[0])

    o_ref[...] = _dot(hm_ref[...], w_ref[...])


def _inproj_call(h, mod, g, w, *, seg_rows, n_seg, tm, tn):
    R, D = h.shape
    N = w.shape[1]
    seg = lambda i, j: (jnp.minimum(i * tm // seg_rows, n_seg), 0, 0)
    return pl.pallas_call(
        _inproj_kernel,
        grid=(R // tm, N // tn),
        in_specs=[
            pl.BlockSpec((tm, D), lambda i, j: (i, 0)),
            pl.BlockSpec((1, 8, D), seg),
            pl.BlockSpec((1, D), lambda i, j: (0, 0)),
            pl.BlockSpec((D, tn), lambda i, j: (0, j)),
        ],
        out_specs=pl.BlockSpec((tm, tn), lambda i, j: (i, j)),
        out_shape=jax.ShapeDtypeStruct((R, N), F32),
        scratch_shapes=[pltpu.VMEM((tm, D), BF16)],
        compiler_params=_cparams(("parallel", "arbitrary")),
        name="inproj",
    )(h, mod, g, w)


def _outproj_kernel(xa_ref, xb_ref, wa_ref, wb_ref, h_ref, mod_ref, o_ref):
    y = _dot(xa_ref[...], wa_ref[...]) + _dot(xb_ref[...], wb_ref[...])
    o_ref[...] = h_ref[...] + mod_ref[0, 0:1, :] * y


def _outproj_call(xa, xb, w, h, mod, *, n_rows, seg_rows, n_seg, tm):
    D = h.shape[1]
    Ka, Kb = xa.shape[1], xb.shape[1]
    assert Ka == Kb and w.shape[0] == Ka + Kb
    seg = lambda i: (jnp.minimum(i * tm // seg_rows, n_seg), 0, 0)
    return pl.pallas_call(
        _outproj_kernel,
        grid=(n_rows // tm,),
        in_specs=[
            pl.BlockSpec((tm, Ka), lambda i: (i, 0)),
            pl.BlockSpec((tm, Kb), lambda i: (i, 0)),
            pl.BlockSpec((Ka, D), lambda i: (0, 0)),
            pl.BlockSpec((Kb, D), lambda i: (1, 0)),
            pl.BlockSpec((tm, D), lambda i: (i, 0)),
            pl.BlockSpec((1, 8, D), seg),
        ],
        out_specs=pl.BlockSpec((tm, D), lambda i: (i, 0)),
        out_shape=jax.ShapeDtypeStruct((n_rows, D), F32),
        compiler_params=_cparams(("parallel",)),
        name="outproj",
    )(xa, xb, w, w, h, mod)


def _rope_tables(T, head_dim):
    half = head_dim // 2
    n_freq = head_dim // 4
    t = jnp.arange(T)
    r = (t // GRID_W).astype(F32)
    cidx = (t % GRID_W).astype(F32)
    inv = ROPE_THETA ** (-jnp.arange(n_freq, dtype=F32) / n_freq)
    ang = jnp.concatenate([r[:, None] * inv, cidx[:, None] * inv], axis=-1)
    cos, sin = jnp.cos(ang), jnp.sin(ang)
    cos_h = jnp.concatenate([cos, cos], axis=-1)
    sin_h = jnp.concatenate([-sin, sin], axis=-1)
    reps = LANE // head_dim
    return jnp.tile(cos_h, (1, reps)), jnp.tile(sin_h, (1, reps))


def _rope128(x, cos, sin):
    return x * cos + pltpu.roll(x, 64, 1) * sin


def _rope64(x, cos, sin):
    lane = lax.broadcasted_iota(jnp.int32, x.shape, 1)
    first = jnp.bitwise_and(lane, 63) < 32
    partner = jnp.where(first, pltpu.roll(x, 96, 1), pltpu.roll(x, 32, 1))
    return x * cos + partner * sin


def _attn_a_kernel(q_ref, k_ref, v_ref, kc_ref, vc_ref, cq_ref, sq_ref, ck_ref, sk_ref, gq_ref, gk_ref,
                   o_ref, kall, vall, *, tc, t, rk, kc):
    grp = A_HEADS // A_KV

    @pl.when(pl.program_id(2) == 0)
    def _():
        gk = gk_ref[...]
        kall[0:tc, :] = (_rms(kc_ref[...]) * gk).astype(BF16)
        vall[0:tc, 0:A_DH] = vc_ref[...].astype(BF16)
        vall[:, A_DH:] = jnp.ones((tc + t, A_DH), BF16)

        def body(c, carry):
            rows = pl.ds(pl.multiple_of(c * rk, rk), rk)
            kn = _rms(k_ref[rows, :]) * gk
            dst = pl.ds(pl.multiple_of(tc + c * rk, math.gcd(tc, rk)), rk)
            kall[dst, :] = _rope128(kn, ck_ref[rows, :], sk_ref[rows, :]).astype(BF16)
            vall[dst, 0:A_DH] = v_ref[rows, :].astype(BF16)
            return carry

        lax.fori_loop(0, t // rk, body, 0)

    q4 = q_ref[...]
    qs = jnp.concatenate([q4[:, h * A_DH:(h + 1) * A_DH] for h in range(grp)], axis=0)
    qn = _rms(qs) * gq_ref[...]
    cos = jnp.concatenate([cq_ref[...]] * grp, axis=0)
    sin = jnp.concatenate([sq_ref[...]] * grp, axis=0)
    qr = (_rope128(qn, cos, sin) * (A_DH ** -0.5 * LOG2E)).astype(BF16)
    m = l = acc = None
    for k0, kn in [(0, tc)] + [(tc + c * kc, kc) for c in range(t // kc)]:
        s = _dot_nt(qr, kall[k0:k0 + kn, :])
        smax = jnp.max(s, axis=-1, keepdims=True)
        if m is None:
            m = smax
            acc = _dot(jnp.exp2(s - m).astype(BF16), vall[k0:k0 + kn, :])
        else:
            m_new = jnp.maximum(m, smax)
            acc = jnp.exp2(m - m_new) * acc + _dot(jnp.exp2(s - m_new).astype(BF16), vall[k0:k0 + kn, :])
            m = m_new
    o = acc[:, 0:A_DH] / acc[:, A_DH:]
    tq = q4.shape[0]
    for h in range(grp):
        o_ref[:, h * A_DH:(h + 1) * A_DH] = o[h * tq:(h + 1) * tq, :].astype(o_ref.dtype)


def _attn_a_call(z, cos, sin, gq, gk, *, B, T, Tc):
    tq = min(2 * Q_BLOCK, T)
    nq = T // tq
    grp_w = (A_HEADS // A_KV) * A_DH
    kcol = A_HEADS
    cbase = B * T // Tc
    rk = min(512, T)
    return pl.pallas_call(
        functools.partial(_attn_a_kernel, tc=Tc, t=T, rk=rk, kc=min(256, T)),
        grid=(B, A_KV, nq),
        in_specs=[
            pl.BlockSpec((tq, grp_w), lambda b, kv, i: (b * nq + i, kv)),
            pl.BlockSpec((T, A_DH), lambda b, kv, i: (b, kcol + kv)),
            pl.BlockSpec((T, A_DH), lambda b, kv, i: (b, kcol + A_KV + kv)),
            pl.BlockSpec((Tc, A_DH), lambda b, kv, i: (cbase + b, kcol + kv)),
            pl.BlockSpec((Tc, A_DH), lambda b, kv, i: (cbase + b, kcol + A_KV + kv)),
            pl.BlockSpec((tq, LANE), lambda b, kv, i: (i, 0)),
            pl.BlockSpec((tq, LANE), lambda b, kv, i: (i, 0)),
            pl.BlockSpec((T, LANE), lambda b, kv, i: (0, 0)),
            pl.BlockSpec((T, LANE), lambda b, kv, i: (0, 0)),
            pl.BlockSpec((1, A_DH), lambda b, kv, i: (0, 0)),
            pl.BlockSpec((1, A_DH), lambda b, kv, i: (0, 0)),
        ],
        out_specs=pl.BlockSpec((tq, grp_w), lambda b, kv, i: (b * nq + i, kv)),
        out_shape=jax.ShapeDtypeStruct((B * T, A_HEADS * A_DH), BF16),
        scratch_shapes=[pltpu.VMEM((Tc + T, A_DH), BF16), pltpu.VMEM((Tc + T, 2 * A_DH), BF16)],
        compiler_params=_cparams(("parallel", "parallel", "arbitrary")),
        name="attn_a",
    )(z, z, z, z, z, cos, sin, cos, sin, gq, gk)


def _attn_a_ctx_kernel(q_ref, kc_ref, vc_ref, gq_ref, gk_ref, o_ref):
    grp = A_HEADS // A_KV
    q4 = q_ref[...]
    tq = q4.shape[0]
    qs = jnp.concatenate([q4[:, h * A_DH:(h + 1) * A_DH] for h in range(grp)], axis=0)
    qn = (_rms(qs) * gq_ref[...] * (A_DH ** -0.5)).astype(BF16)
    kc = (_rms(kc_ref[...]) * gk_ref[...]).astype(BF16)
    s = _dot_nt(qn, kc)
    p = jnp.exp(s - jnp.max(s, axis=-1, keepdims=True))
    l = jnp.sum(p, axis=-1, keepdims=True)
    o = _dot(p.astype(BF16), vc_ref[...].astype(BF16)) / l
    for h in range(grp):
        o_ref[:, h * A_DH:(h + 1) * A_DH] = o[h * tq:(h + 1) * tq, :].astype(o_ref.dtype)


def _attn_a_ctx_call(z, gq, gk, *, B, T, Tc):
    grp_w = (A_HEADS // A_KV) * A_DH
    kcol = A_HEADS
    cbase = B * T // Tc
    return pl.pallas_call(
        _attn_a_ctx_kernel,
        grid=(B, A_KV),
        in_specs=[
            pl.BlockSpec((Tc, grp_w), lambda b, kv: (cbase + b, kv)),
            pl.BlockSpec((Tc, A_DH), lambda b, kv: (cbase + b, kcol + kv)),
            pl.BlockSpec((Tc, A_DH), lambda b, kv: (cbase + b, kcol + A_KV + kv)),
            pl.BlockSpec((1, A_DH), lambda b, kv: (0, 0)),
            pl.BlockSpec((1, A_DH), lambda b, kv: (0, 0)),
        ],
        out_specs=pl.BlockSpec((Tc, grp_w), lambda b, kv: (b, kv)),
        out_shape=jax.ShapeDtypeStruct((B * Tc, A_HEADS * A_DH), BF16),
        compiler_params=_cparams(("parallel", "parallel")),
        name="attn_a_ctx",
    )(z, z, z, gq, gk)


MLSTM_CAUG = B_DV + 8


def _mlstm_local_kernel(q_ref, k_ref, v0_ref, v1_ref, gc_ref, gr_ref, bcol_ref, brow_ref,
                        nloc_ref, stat_ref, scal_ref, uaug_ref, *, cb):
    L = B_CHUNK
    row = lax.broadcasted_iota(jnp.int32, (L, L), 0)
    col = lax.broadcasted_iota(jnp.int32, (L, L), 1)
    gcol = (gc_ref[...] + bcol_ref[...]).reshape(cb, L, LANE)
    grow = gr_ref[...] + brow_ref[...][None]
    sub0 = lax.broadcasted_iota(jnp.int32, (cb, 8, B_DQK), 1) == 0
    n_stat = 3 * B_HEADS
    stat_ref[:, :, n_stat:] = jnp.zeros((2, cb * L, stat_ref.shape[2] - n_stat), F32)
    for hd in range(B_HEADS):
        q = q_ref[:, hd * B_DQK:(hd + 1) * B_DQK].reshape(cb, L, B_DQK)
        k = k_ref[:, hd * B_DQK:(hd + 1) * B_DQK].reshape(cb, L, B_DQK)
        v_ref = v0_ref if hd < 2 else v1_ref
        v = v_ref[:, (hd % 2) * B_DV:(hd % 2 + 1) * B_DV].reshape(cb, L, B_DV)
        qb = (q * (B_DQK ** -0.5)).astype(BF16)
        kb = k.astype(BF16)
        vb = v.astype(BF16)
        qk = jnp.einsum('cld,csd->cls', qb, kb, preferred_element_type=F32)
        for d in range(2):
            gi = 2 * d * B_HEADS + hd
            gf = gi + B_HEADS
            li_col = gcol[:, :, gi:gi + 1]
            lf_col = _log_sigmoid(gcol[:, :, gf:gf + 1])
            li_row = grow[:, gi:gi + 1, :]
            lf_row = _log_sigmoid(grow[:, gf:gf + 1, :])
            mask = ((col <= row) if d == 0 else (col >= row))[None]
            mask_t = ((row <= col) if d == 0 else (row >= col))[None]
            bcum_col = jnp.sum(jnp.where(mask, lf_row, 0.0), axis=2, keepdims=True)
            bcum_row = jnp.sum(jnp.where(mask_t, lf_col, 0.0), axis=1, keepdims=True)
            log_d = jnp.where(mask, bcum_col - bcum_row + li_row, -jnp.inf)
            a = jnp.max(log_d, axis=2, keepdims=True)
            s = qk * jnp.exp(log_d - a)
            nloc = jnp.einsum('cls,csv->clv', s.astype(BF16), vb, preferred_element_type=F32)
            dloc = jnp.sum(s, axis=2, keepdims=True)
            b_last = jnp.sum(lf_row, axis=2, keepdims=True)
            log_w = b_last - bcum_col + li_col
            b_loc = jnp.max(log_w, axis=1, keepdims=True)
            w = jnp.exp(log_w - b_loc)
            u = jnp.einsum('csv,csd->cvd', (w * v).astype(BF16), kb, preferred_element_type=F32)
            nu = jnp.sum(w * kb.astype(F32), axis=1, keepdims=True)
            nloc_ref[d, :, hd * B_DV:(hd + 1) * B_DV] = nloc.reshape(cb * L, B_DV)
            stat_ref[d, :, hd:hd + 1] = (a - bcum_col).reshape(cb * L, 1)
            stat_ref[d, :, B_HEADS + hd:B_HEADS + hd + 1] = dloc.reshape(cb * L, 1)
            stat_ref[d, :, 2 * B_HEADS + hd:2 * B_HEADS + hd + 1] = bcum_col.reshape(cb * L, 1)
            scal_ref[d, :, hd:hd + 1, :] = jnp.broadcast_to(b_last, (cb, 1, LANE))
            scal_ref[d, :, B_HEADS + hd:B_HEADS + hd + 1, :] = jnp.broadcast_to(b_loc, (cb, 1, LANE))
            uaug_ref[d, :, hd, 0:B_DV, :] = u
            uaug_ref[d, :, hd, B_DV:MLSTM_CAUG, :] = jnp.where(sub0, nu, 0.0)


def _mlstm_local_call(z, grow, bcol, brow, *, cb):
    L = B_CHUNK
    R = z.shape[0]
    nchunk = R // L
    W = B_HEADS * B_DV
    return pl.pallas_call(
        functools.partial(_mlstm_local_kernel, cb=cb),
        grid=(nchunk // cb,),
        in_specs=[
            pl.BlockSpec((cb * L, 512), lambda i: (i, 3)),
            pl.BlockSpec((cb * L, 512), lambda i: (i, 4)),
            pl.BlockSpec((cb * L, 512), lambda i: (i, 5)),
            pl.BlockSpec((cb * L, 512), lambda i: (i, 6)),
            pl.BlockSpec((cb * L, LANE), lambda i: (i, AB_GATE_COL // LANE)),
            pl.BlockSpec((cb, 16, L), lambda i: (i, 0, 0)),
            pl.BlockSpec((1, LANE), lambda i: (0, 0)),
            pl.BlockSpec((16, 1), lambda i: (0, 0)),
        ],
        out_specs=[
            pl.BlockSpec((2, cb * L, W), lambda i: (0, i, 0)),
            pl.BlockSpec((2, cb * L, 16), lambda i: (0, i, 0)),
            pl.BlockSpec((2, cb, 8, LANE), lambda i: (0, i, 0, 0)),
            pl.BlockSpec((2, cb, B_HEADS, MLSTM_CAUG, B_DQK), lambda i: (0, i, 0, 0, 0)),
        ],
        out_shape=[
            jax.ShapeDtypeStruct((2, R, W), F32),
            jax.ShapeDtypeStruct((2, R, 16), F32),
            jax.ShapeDtypeStruct((2, nchunk, 8, LANE), F32),
            jax.ShapeDtypeStruct((2, nchunk, B_HEADS, MLSTM_CAUG, B_DQK), F32),
        ],
        compiler_params=_cparams(("parallel",)),
        name="mlstm_local",
    )(z, z, z, z, z, grow, bcol, brow)


def _mlstm_scan_kernel(qf_ref, nf_ref, stf_ref, scf_ref, uf_ref, qb_ref, nb_ref, stb_ref, scb_ref, ub_ref,
                       of_ref, ob_ref, c_s, m_s):
    @pl.when(pl.program_id(1) == 0)
    def _():
        c_s[...] = jnp.zeros_like(c_s)
        m_s[...] = jnp.zeros_like(m_s)

    L = B_CHUNK
    units = []
    for d, (q_ref, n_ref, st_ref, sc_ref, u_ref, o_ref) in enumerate((
            (qf_ref, nf_ref, stf_ref, scf_ref, uf_ref, of_ref),
            (qb_ref, nb_ref, stb_ref, scb_ref, ub_ref, ob_ref))):
        st = st_ref[0]
        for hd in range(B_HEADS):
            u = d * B_HEADS + hd
            units.append((u, hd, q_ref, n_ref, st, sc_ref, u_ref, o_ref, c_s[u], m_s[u, 0:1, :]))
    new_state = []
    for u, hd, q_ref, n_ref, st, sc_ref, u_ref, o_ref, c_prev, m_row in units:
        rep = lambda j: jnp.broadcast_to(st[:, j:j + 1], (L, LANE))
        g, dloc, bcum = rep(hd), rep(B_HEADS + hd), rep(2 * B_HEADS + hd)
        m = jnp.broadcast_to(m_row, (L, LANE))
        e_intra = jnp.exp(jnp.minimum(g - m, 0.0))
        w_inter = jnp.exp(jnp.minimum(m - g, 0.0))
        floor = jnp.exp(-(bcum + jnp.maximum(m, g)))
        qb = (q_ref[:, hd * B_DQK:(hd + 1) * B_DQK] * (B_DQK ** -0.5)).astype(BF16)
        inter = _dot_nt(qb, c_prev.astype(BF16))
        den = e_intra * dloc + w_inter * inter[:, B_DV:B_DV + LANE]
        r = 1.0 / jnp.maximum(jnp.abs(den), floor)
        for j in range(B_DV // LANE):
            c0 = hd * B_DV + j * LANE
            num = e_intra * n_ref[0, :, c0:c0 + LANE] + w_inter * inter[:, j * LANE:(j + 1) * LANE]
            o_ref[0, :, c0:c0 + LANE] = num * r
        b_last = sc_ref[0, 0, hd:hd + 1, :]
        b_loc = sc_ref[0, 0, B_HEADS + hd:B_HEADS + hd + 1, :]
        m_new = jnp.maximum(b_last + m_row, b_loc)
        decay = jnp.exp(b_last + m_row - m_new)
        scl = jnp.exp(b_loc - m_new)
        c_new = decay * c_prev[0:B_DV] + scl * u_ref[0, 0, hd, 0:B_DV, :]
        n_new = decay * c_prev[B_DV:] + scl * jnp.broadcast_to(u_ref[0, 0, hd, B_DV:B_DV + 1, :], (LANE, B_DQK))
        new_state.append((u, c_new, n_new, m_new))
    for u, c_new, n_new, m_new in new_state:
        c_s[u, 0:B_DV, :] = c_new
        c_s[u, B_DV:, :] = n_new
        m_s[u] = jnp.broadcast_to(m_new, m_s.shape[1:])


def _mlstm_scan_call(z, nloc, stat, scal, uaug, *, B, T, Tc):
    L = B_CHUNK
    ncl, ncc = T // L, Tc // L
    nc = ncl + ncc
    cbase = B * T // L
    R = z.shape[0]
    W = B_HEADS * B_DV

    def blk(b, c, d):
        is_ctx = c < ncc
        jc = c if d == 0 else ncc - 1 - c
        jl = (c - ncc) if d == 0 else (ncl - 1 - (c - ncc))
        return jnp.where(is_ctx, cbase + b * ncc + jc, b * ncl + jl)

    def specs(d):
        return [
            pl.BlockSpec((L, 512), lambda b, c: (blk(b, c, d), 3)),
            pl.BlockSpec((1, L, W), lambda b, c: (d, blk(b, c, d), 0)),
            pl.BlockSpec((1, L, 16), lambda b, c: (d, blk(b, c, d), 0)),
            pl.BlockSpec((1, 1, 8, LANE), lambda b, c: (d, blk(b, c, d), 0, 0)),
            pl.BlockSpec((1, 1, B_HEADS, MLSTM_CAUG, B_DQK), lambda b, c: (d, blk(b, c, d), 0, 0, 0)),
        ]

    out_sds = jax.ShapeDtypeStruct((1, R, W), F32)
    return pl.pallas_call(
        _mlstm_scan_kernel,
        grid=(B, nc),
        in_specs=specs(0) + specs(1),
        out_specs=[
            pl.BlockSpec((1, L, W), lambda b, c: (0, blk(b, c, 0), 0)),
            pl.BlockSpec((1, L, W), lambda b, c: (0, blk(b, c, 1), 0)),
        ],
        out_shape=[out_sds, out_sds],
        scratch_shapes=[
            pltpu.VMEM((2 * B_HEADS, B_DV + LANE, B_DQK), F32),
            pltpu.VMEM((2 * B_HEADS, 8, LANE), F32),
        ],
        compiler_params=_cparams(("parallel", "arbitrary")),
        name="mlstm_scan",
    )(z, nloc, stat, scal, uaug, z, nloc, stat, scal, uaug)


def _mlstm_finish_kernel(hf_ref, hb_ref, o0_ref, o1_ref, hg_ref, y_ref):
    for hd in range(B_HEADS):
        sl = slice(hd * B_DV, (hd + 1) * B_DV)
        hh = hf_ref[0, :, sl] + hb_ref[0, :, sl]
        o_ref = o0_ref if hd < 2 else o1_ref
        o = o_ref[:, (hd % 2) * B_DV:(hd % 2 + 1) * B_DV]
        y_ref[:, sl] = (_sigmoid(o) * _rms(hh) * hg_ref[:, sl]).astype(y_ref.dtype)


def _mlstm_finish_call(hf, hb, z, head_g, *, tm):
    R = z.shape[0]
    W = B_HEADS * B_DV
    return pl.pallas_call(
        _mlstm_finish_kernel,
        grid=(R // tm,),
        in_specs=[
            pl.BlockSpec((1, tm, W), lambda i: (0, i, 0)),
            pl.BlockSpec((1, tm, W), lambda i: (0, i, 0)),
            pl.BlockSpec((tm, 512), lambda i: (i, 7)),
            pl.BlockSpec((tm, 512), lambda i: (i, 8)),
            pl.BlockSpec((1, W), lambda i: (0, 0)),
        ],
        out_specs=pl.BlockSpec((tm, W), lambda i: (i, 0)),
        out_shape=jax.ShapeDtypeStruct((R, W), BF16),
        compiler_params=_cparams(("parallel",)),
        name="mlstm_finish",
    )(hf, hb, z, z, head_g)


def _s5_params(a_re, a_im, log_dt, b_re, b_im, c_re, c_im, reverse):
    hp = lax.Precision.HIGHEST
    Lc = S5_CHUNK
    dt = jnp.exp(log_dt)[:, None]
    lam_re, lam_im = a_re * dt, a_im * dt
    mag = jnp.exp(lam_re)
    ab_re, ab_im = mag * jnp.cos(lam_im), mag * jnp.sin(lam_im)
    den = a_re * a_re + a_im * a_im
    nr, ni = ab_re - 1.0, ab_im
    k_re, k_im = (nr * a_re + ni * a_im) / den, (ni * a_re - nr * a_im) / den
    bb_re = k_re[..., None] * b_re - k_im[..., None] * b_im
    bb_im = k_re[..., None] * b_im + k_im[..., None] * b_re
    tau = jnp.arange(Lc + 1, dtype=F32)[:, None, None]
    pm = jnp.exp(tau * lam_re)
    pw_re, pw_im = pm * jnp.cos(tau * lam_im), pm * jnp.sin(tau * lam_im)
    cp_re = c_re[None] * pw_re[:, :, None, :] - c_im[None] * pw_im[:, :, None, :]
    cp_im = c_re[None] * pw_im[:, :, None, :] + c_im[None] * pw_re[:, :, None, :]
    kk = jnp.einsum('tgoq,gqi->tgoi', jnp.concatenate([cp_re, -cp_im], axis=-1),
                    jnp.concatenate([bb_re, bb_im], axis=1), precision=hp)
    G = a_re.shape[0]
    i_idx = jnp.arange(Lc)[:, None]
    j_idx = jnp.arange(Lc)[None, :]
    lag = (i_idx - j_idx) if reverse else (j_idx - i_idx)
    valid = lag >= 0
    kt = jnp.where(valid[:, :, None, None, None], kk[jnp.clip(lag, 0, Lc)], 0.0)
    kt = kt.transpose(2, 0, 4, 1, 3).reshape(G, Lc * C_GROUP, Lc * C_GROUP)
    e = jnp.arange(Lc) if reverse else (Lc - 1 - jnp.arange(Lc))
    w_re = pw_re[e][:, :, :, None] * bb_re[None] - pw_im[e][:, :, :, None] * bb_im[None]
    w_im = pw_re[e][:, :, :, None] * bb_im[None] + pw_im[e][:, :, :, None] * bb_re[None]
    w_re = w_re.transpose(1, 0, 3, 2).reshape(G, Lc * C_GROUP, C_STATE)
    w_im = w_im.transpose(1, 0, 3, 2).reshape(G, Lc * C_GROUP, C_STATE)
    even = (jnp.arange(G) % 2 == 0)[:, None, None]
    zc = jnp.zeros_like(w_re)
    half_cols = lambda m: jnp.where(even, jnp.concatenate([m, zc], axis=-1), jnp.concatenate([zc, m], axis=-1))
    wre, wim = half_cols(w_re), half_cols(w_im)
    f = (Lc - jnp.arange(Lc)) if reverse else (jnp.arange(Lc) + 1)
    cj_re = cp_re[f].transpose(1, 3, 0, 2).reshape(G, C_STATE, Lc * C_GROUP)
    cj_im = -cp_im[f].transpose(1, 3, 0, 2).reshape(G, C_STATE, Lc * C_GROUP)
    zr = jnp.zeros_like(cj_re)
    half_rows = lambda m: jnp.where(even, jnp.concatenate([m, zr], axis=1), jnp.concatenate([zr, m], axis=1))
    cre, cim = half_rows(cj_re), half_rows(cj_im)
    pair = lambda v: v.reshape(G // 2, 1, 2 * C_STATE)
    are, aim = pair(pw_re[Lc]), pair(pw_im[Lc])
    wcat = jnp.concatenate([wre, wim], axis=-1)
    ccat = jnp.concatenate([cre, cim], axis=1)
    return kt.astype(BF16), wcat.astype(BF16), ccat.astype(BF16), are, aim


def _granule_transpose(xs):
    xs = list(xs)
    gran = jnp.right_shift(lax.broadcasted_iota(jnp.int32, (1, LANE), 1), C_GROUP.bit_length() - 1)
    for bit in (1, 2, 4):
        hi = jnp.bitwise_and(gran, bit) != 0
        for a in range(S5_GB):
            if a & bit:
                continue
            p = a | bit
            xa, xp = xs[a], xs[p]
            xs[a] = jnp.where(hi, pltpu.roll(xp, bit * C_GROUP, 1), xa)
            xs[p] = jnp.where(hi, xp, pltpu.roll(xa, LANE - bit * C_GROUP, 1))
    return xs


def _s5_kernel(ul_ref, uc_ref, kt_ref, wcat_ref, ccat_ref, are_ref, aim_ref, y_ref,
               ug_s, wr_s, wi_s, xr_s, xi_s, y_s, *, nb, ncc, ncl):
    Lc = S5_CHUNK
    lat0 = nb * ncc
    for src, row0, n in ((uc_ref, 0, nb * ncc), (ul_ref, lat0, nb * ncl)):
        folded = _granule_transpose([src[pl.ds(i, n, stride=Lc), :] for i in range(Lc)])
        for g in range(S5_GB):
            ug_s[g, row0:row0 + n, :] = folded[g].astype(BF16)

    npair = S5_GB // 2
    for d in range(2):
        for k in range(npair):
            u0, u1 = ug_s[2 * k], ug_s[2 * k + 1]
            w = _dot(u0, wcat_ref[d, 2 * k]) + _dot(u1, wcat_ref[d, 2 * k + 1])
            wr_s[k] = w[:, 0:LANE]
            wi_s[k] = w[:, LANE:]
        are = [jnp.broadcast_to(are_ref[d, k], (nb, LANE)) for k in range(npair)]
        aim = [jnp.broadcast_to(aim_ref[d, k], (nb, LANE)) for k in range(npair)]

        def make_body(base, n):
            def body(s, carry):
                c = s if d == 0 else n - 1 - s
                rows = pl.ds(base + c, nb, stride=n)
                new = []
                for k in range(npair):
                    xr, xi = carry[2 * k], carry[2 * k + 1]
                    xr_s[k, rows, :] = xr
                    xi_s[k, rows, :] = xi
                    new.append(are[k] * xr - aim[k] * xi + wr_s[k, rows, :])
                    new.append(are[k] * xi + aim[k] * xr + wi_s[k, rows, :])
                return tuple(new)
            return body

        state = tuple(jnp.zeros((nb, LANE), F32) for _ in range(2 * npair))
        state = lax.fori_loop(0, ncc, make_body(0, ncc), state, unroll=4)
        lax.fori_loop(0, ncl, make_body(lat0, ncl), state, unroll=4)
        for g in range(S5_GB):
            y = (_dot(ug_s[g, lat0:, :], kt_ref[d, g])
                 + _dot(jnp.concatenate([xr_s[g // 2, lat0:, :], xi_s[g // 2, lat0:, :]], axis=1).astype(BF16),
                        ccat_ref[d, g]))
            if d == 0:
                y_s[g] = y
            else:
                y_s[g] += y

    unfolded = _granule_transpose([y_s[g] for g in range(S5_GB)])
    for j in range(Lc):
        y_ref[pl.ds(j, nb * ncl, stride=Lc), :] = unfolded[j]


def _s5_call(z, kt, wcat, ccat, are, aim, *, B, T, Tc, nb):
    Lc = S5_CHUNK
    ncl, ncc = T // Lc, Tc // Lc
    nr = nb * (ncl + ncc)
    cbase = B * T // (nb * Tc)
    wspec = pl.BlockSpec((2, S5_GB, LANE, LANE), lambda i, h: (0, i, 0, 0))
    aspec = pl.BlockSpec((2, S5_GB // 2, 1, LANE), lambda i, h: (0, i, 0, 0))
    big = lambda dt: pltpu.VMEM((S5_GB, nr, LANE), dt)
    half = pltpu.VMEM((S5_GB // 2, nr, LANE), F32)
    return pl.pallas_call(
        functools.partial(_s5_kernel, nb=nb, ncc=ncc, ncl=ncl),
        grid=(C_GROUPS // S5_GB, B // nb),
        in_specs=[
            pl.BlockSpec((nb * T, LANE), lambda i, h: (h, i)),
            pl.BlockSpec((nb * Tc, LANE), lambda i, h: (cbase + h, i)),
            wspec,
            pl.BlockSpec((2, S5_GB, LANE, 2 * LANE), lambda i, h: (0, i, 0, 0)),
            pl.BlockSpec((2, S5_GB, 2 * LANE, LANE), lambda i, h: (0, i, 0, 0)),
            aspec, aspec],
        out_specs=pl.BlockSpec((nb * T, LANE), lambda i, h: (h, i)),
        out_shape=jax.ShapeDtypeStruct((B * T, C_WIDTH), F32),
        scratch_shapes=[big(BF16), half, half, half, half,
                        pltpu.VMEM((S5_GB, nb * ncl, LANE), F32)],
        compiler_params=_cparams(("parallel", "parallel")),
        name="s5_scan",
    )(z, z, kt, wcat, ccat, are, aim)


def _s5_finish_kernel(y_ref, u_ref, d_ref, w_ref, b_ref, o_ref):
    y = y_ref[...] + d_ref[...] * u_ref[...]
    g = 0.5 * y * (1.0 + jnp.tanh(math.sqrt(2.0 / math.pi) * (y + 0.044715 * (y * y * y))))
    o_ref[...] = (g * _sigmoid(_dot(g.astype(BF16), w_ref[...]) + b_ref[...])).astype(o_ref.dtype)


def _s5_finish_call(y, z, d_skip, glu_w, glu_b, *, n_rows, tm):
    W = C_WIDTH
    return pl.pallas_call(
        _s5_finish_kernel,
        grid=(n_rows // tm,),
        in_specs=[
            pl.BlockSpec((tm, W), lambda i: (i, 0)),
            pl.BlockSpec((tm, W), lambda i: (i, 0)),
            pl.BlockSpec((1, W), lambda i: (0, 0)),
            pl.BlockSpec((W, W), lambda i: (0, 0)),
            pl.BlockSpec((1, W), lambda i: (0, 0)),
        ],
        out_specs=pl.BlockSpec((tm, W), lambda i: (i, 0)),
        out_shape=jax.ShapeDtypeStruct((n_rows, W), BF16),
        compiler_params=_cparams(("parallel",)),
        name="s5_finish",
    )(y, z, d_skip, glu_w, glu_b)


def _ones_lane(par):
    return D_DH if par == 0 else 0


def _attn_d_kernel(q_ref, k_ref, v_ref, kc_ref, vc_ref, cq_ref, sq_ref, ck_ref, sk_ref, sink_ref,
                   o_ref, kpe, kpo, vpe, vpo, kce, kco, vce, vco, band, *, t, tc, rk):
    W = D_WINDOW
    kv = pl.program_id(1)
    i = pl.program_id(2)
    lane = lax.broadcasted_iota(jnp.int32, (1, LANE), 1)
    lo = lane < D_DH

    def halves(x, ones=False):
        sel = jnp.where(kv == 0, x, pltpu.roll(x, 64, 1))
        e = jnp.where(lo, sel, 0.0)
        o = pltpu.roll(e, 64, 1)
        if ones:
            e = jnp.where(lane == _ones_lane(0), 1.0, e)
            o = jnp.where(lane == _ones_lane(1), 1.0, o)
        return e.astype(BF16), o.astype(BF16)

    @pl.when(i == 0)
    def _():
        zpad = jnp.zeros((W, LANE), BF16)
        for ref in (kpe, kpo, vpe, vpo):
            ref[0:W, :] = zpad
            ref[W + t:W + t + W, :] = zpad
        kce[...], kco[...] = halves(kc_ref[...])
        vce[...], vco[...] = halves(vc_ref[...], ones=True)

        def body(c, carry):
            rows = pl.ds(pl.multiple_of(c * rk, rk), rk)
            dst = pl.ds(pl.multiple_of(W + c * rk, math.gcd(W, rk)), rk)
            kpe[dst, :], kpo[dst, :] = halves(_rope64(k_ref[rows, :], ck_ref[rows, :], sk_ref[rows, :]))
            vpe[dst, :], vpo[dst, :] = halves(v_ref[rows, :], ones=True)
            return carry

        lax.fori_loop(0, t // rk, body, 0)

    tq = q_ref.shape[0]
    nslab = q_ref.shape[1] // LANE
    span = tq + 2 * W

    @pl.when(i == 0)
    def _():
        rr = jnp.bitwise_and(lax.broadcasted_iota(jnp.int32, (nslab * tq, span), 0), tq - 1)
        cc = lax.broadcasted_iota(jnp.int32, (nslab * tq, span), 1)
        band[...] = jnp.where(jnp.abs(rr + W - cc) <= W, 0.0, -jnp.inf)

    cos, sin = cq_ref[...], sq_ref[...]
    qs = jnp.concatenate(
        [_rope64(q_ref[:, j * LANE:(j + 1) * LANE], cos, sin) for j in range(nslab)], axis=0)
    qs = (qs * (D_DH ** -0.5 * LOG2E)).astype(BF16)
    rows_k = pl.ds(pl.multiple_of(i * tq, tq), span)
    kpos = i * tq - W + lax.broadcasted_iota(jnp.int32, (1, span), 1)
    bias = band[...] + jnp.where((kpos >= 0) & (kpos < t), 0.0, -jnp.inf)
    out = None
    lane_o = lax.broadcasted_iota(jnp.int32, (1, LANE), 1)
    for par, (kp, vp, kcx, vcx) in enumerate(((kpe, vpe, kce, vce), (kpo, vpo, kco, vco))):
        s = jnp.concatenate([_dot_nt(qs, kcx[...]), _dot_nt(qs, kp[rows_k, :]) + bias], axis=1)
        snk = LOG2E * jnp.concatenate(
            [jnp.broadcast_to(sink_ref[0, 0:1, 2 * j + par:2 * j + par + 1], (tq, 1)) for j in range(nslab)],
            axis=0)
        m = jnp.maximum(jnp.max(s, axis=-1, keepdims=True), snk)
        p = jnp.exp2(s - m).astype(BF16)
        o = _dot(p[:, 0:tc], vcx[...]) + _dot(p[:, tc:], vp[rows_k, :])
        one = _ones_lane(par)
        l = o[:, one:one + 1] + jnp.exp2(snk - m)
        own = (lane_o < D_DH) if par == 0 else (lane_o >= D_DH)
        o = jnp.where(own, o, 0.0) / l
        out = o if out is None else out + o
    for j in range(nslab):
        o_ref[:, j * LANE:(j + 1) * LANE] = out[j * tq:(j + 1) * tq, :].astype(o_ref.dtype)


def _attn_d_call(z, cos, sin, sink, *, B, T, Tc):
    tq = min(2 * Q_BLOCK, T)
    nq = T // tq
    grp_w = (D_HEADS // D_KV) * D_DH
    qcol = C_WIDTH // grp_w
    kcol = (C_WIDTH + D_HEADS * D_DH) // LANE
    cbase = B * T // Tc
    rk = min(512, T)
    pad = lambda n: pltpu.VMEM((n, LANE), BF16)
    return pl.pallas_call(
        functools.partial(_attn_d_kernel, t=T, tc=Tc, rk=rk),
        grid=(B, D_KV, nq),
        in_specs=[
            pl.BlockSpec((tq, grp_w), lambda b, kv, i: (b * nq + i, qcol + kv)),
            pl.BlockSpec((T, LANE), lambda b, kv, i: (b, kcol)),
            pl.BlockSpec((T, LANE), lambda b, kv, i: (b, kcol + 1)),
            pl.BlockSpec((Tc, LANE), lambda b, kv, i: (cbase + b, kcol)),
            pl.BlockSpec((Tc, LANE), lambda b, kv, i: (cbase + b, kcol + 1)),
            pl.BlockSpec((tq, LANE), lambda b, kv, i: (i, 0)),
            pl.BlockSpec((tq, LANE), lambda b, kv, i: (i, 0)),
            pl.BlockSpec((T, LANE), lambda b, kv, i: (0, 0)),
            pl.BlockSpec((T, LANE), lambda b, kv, i: (0, 0)),
            pl.BlockSpec((1, 1, LANE), lambda b, kv, i: (kv, 0, 0)),
        ],
        out_specs=pl.BlockSpec((tq, grp_w), lambda b, kv, i: (b * nq + i, kv)),
        out_shape=jax.ShapeDtypeStruct((B * T, D_HEADS * D_DH), BF16),
        scratch_shapes=[pad(T + 2 * D_WINDOW)] * 4 + [pad(Tc)] * 4
        + [pltpu.VMEM((grp_w // LANE * tq, tq + 2 * D_WINDOW), F32)],
        compiler_params=_cparams(("parallel", "parallel", "arbitrary")),
        name="attn_d",
    )(z, z, z, z, z, cos, sin, cos, sin, sink)


def _mod_rows(ml, idx):
    sel = ml[:, jnp.array(idx), :]
    return jnp.pad(sel, ((0, 0), (0, 8 - len(idx)), (0, 0)))


def kernel(x, c, ctx, c_ctx, mod_w, mod_b, norm_g, ffn_w1, ffn_w3, ffn_w2, ab_w_in, ab_b_gate, ab_w_out, a_gq, a_gk, b_head_g, cd_w_in, cd_w_out, s5_a_re, s5_a_im, s5_log_dt, s5_b_re, s5_b_im, s5_c_re, s5_c_im, s5_d, s5_glu_w, s5_glu_b, d_sink, final_g):
    B, T, D = x.shape
    Tc = ctx.shape[1]
    depth = mod_w.shape[0]
    n_lat, n_ctx = B * T, B * Tc
    n_all = n_lat + n_ctx
    tm = math.gcd(math.gcd(T, n_ctx), 512)
    tm_mm = math.gcd(math.gcd(T, n_ctx), 1024)
    tf = math.gcd(ffn_w1.shape[-1], 512)
    seg = dict(seg_rows=T, n_seg=B)

    h, hc0 = x, ctx
    cvec = jnp.zeros((8, D), F32).at[:B].set(c).at[B].set(c_ctx)
    mods = _modvec_call(cvec, mod_w, mod_b).reshape(depth, 8, N_MOD, D)[:, :B + 1]
    cos_a, sin_a = _rope_tables(T, A_DH)
    cos_d, sin_d = _rope_tables(T, D_DH)
    w1, w3, w2 = ffn_w1.astype(BF16), ffn_w3.astype(BF16), ffn_w2.astype(BF16)
    ones = jnp.ones((1, D), F32)

    for l in range(depth):
        last = l == depth - 1
        want_ctx = not last
        ml = mods[l]
        i = l // 2
        h = _ffn_call(h, hc0 if l == 0 else None, _mod_rows(ml, (0, 1, 2)), norm_g[l, 0][None],
                      w1, w3, w2, ones, wsel=(l, 0),
                      n_rows=n_all, tm=tm, tf=tf, final_norm=False, **seg)
        mod_in = _mod_rows(ml, (3, 4))
        mod_out = _mod_rows(ml, (5,))
        if l % 2 == 0:
            w_in = jnp.pad(ab_w_in[i], ((0, 0), (0, AB_IN_PAD - ab_w_in.shape[-1]))).astype(BF16)
            z = _inproj_call(h, mod_in, norm_g[l, 1][None], w_in, tm=tm_mm, tn=1024, **seg)
            ya = _attn_a_call(z, cos_a, sin_a, a_gq[i][None], a_gk[i][None], B=B, T=T, Tc=Tc)
            yac = _attn_a_ctx_call(z, a_gq[i][None], a_gk[i][None], B=B, T=T, Tc=Tc)
            ng = 4 * B_HEADS
            gates = z[:, AB_GATE_COL:AB_GATE_COL + ng]
            grow = gates.reshape(n_all // B_CHUNK, B_CHUNK, ng).transpose(0, 2, 1)
            bcol = jnp.pad(ab_b_gate[i], (0, LANE - ng))[None]
            brow = ab_b_gate[i][:, None]
            nloc, stat, scal, uaug = _mlstm_local_call(z, grow, bcol, brow, cb=math.gcd(n_all // B_CHUNK, 8))
            hf, hb = _mlstm_scan_call(z, nloc, stat, scal, uaug, B=B, T=T, Tc=Tc)
            yb = _mlstm_finish_call(hf, hb, z, b_head_g[i][None], tm=tm)
            xa = jnp.concatenate([ya, yac], axis=0)
            w_out = ab_w_out[i].astype(BF16)
            n_rows = n_all
        else:
            w_in = cd_w_in[i].astype(BF16)
            z = _inproj_call(h, mod_in, norm_g[l, 1][None], w_in, tm=tm_mm, tn=w_in.shape[1] // 3, **seg)
            prm = [_s5_params(s5_a_re[i, d], s5_a_im[i, d], s5_log_dt[i, d], s5_b_re[i, d], s5_b_im[i, d],
                              s5_c_re[i, d], s5_c_im[i, d], reverse=(d == 1)) for d in range(2)]
            s5w = [jnp.stack([prm[0][k], prm[1][k]]) for k in range(len(prm[0]))]
            ys5 = _s5_call(z, *s5w, B=B, T=T, Tc=Tc, nb=math.gcd(B, 2))
            xa = _s5_finish_call(ys5, z, s5_d[i][None], s5_glu_w[i].astype(BF16), s5_glu_b[i][None],
                                 n_rows=n_lat, tm=tm)
            sink = jnp.pad(d_sink[i].reshape(D_KV, 1, D_HEADS // D_KV), ((0, 0), (0, 0), (0, LANE - D_HEADS // D_KV)))
            yb = _attn_d_call(z, cos_d, sin_d, sink, B=B, T=T, Tc=Tc)
            w_out = cd_w_out[i].astype(BF16)
            n_rows = n_all if want_ctx else n_lat
        if not want_ctx:
            n_rows = n_lat
        h = _outproj_call(xa, yb, w_out, h, mod_out, n_rows=n_rows, tm=tm, **seg)
        h = _ffn_call(h, None, _mod_rows(ml, (6, 7, 8)), norm_g[l, 2][None], w1, w3, w2,
                      final_g[None] if last else ones, wsel=(l, 1), n_rows=n_rows, tm=tm, tf=tf, final_norm=last,
                      out_3d=last, **seg)
    return h
```

```python
import functools
import math

import jax
import jax.numpy as jnp
from jax import lax
from jax.experimental import pallas as pl
from jax.experimental.pallas import tpu as pltpu

F32 = jnp.float32
BF16 = jnp.bfloat16

EPS = 1e-6
ROPE_THETA = 10000.0
GRID_W = 64
N_MOD = 9
Q_BLOCK = 128

A_HEADS, A_KV, A_DH = 8, 2, 128
B_HEADS, B_DQK, B_DV, B_CHUNK = 4, 128, 256, 64
C_WIDTH, C_GROUP, C_STATE = 1024, 16, 64
C_GROUPS = C_WIDTH // C_GROUP
D_HEADS, D_KV, D_DH, D_WINDOW = 16, 2, 64, 128

AB_GATE_COL = 4608
AB_IN_PAD = 5120
S5_CHUNK = 8
S5_GB = 8

LANE = 128
LOG2E = math.log2(math.e)
VMEM_LIMIT = 56 * 1024 * 1024


def _cparams(sem, vmem_limit=VMEM_LIMIT):
    return pltpu.CompilerParams(dimension_semantics=sem, vmem_limit_bytes=vmem_limit)


def _rms(x):
    return x * lax.rsqrt(jnp.mean(x * x, axis=-1, keepdims=True) + EPS)


def _sigmoid(x):
    return 1.0 / (1.0 + jnp.exp(-x))


def _log_sigmoid(x):
    return jnp.minimum(x, 0.0) - jnp.log(1.0 + jnp.exp(-jnp.abs(x)))


def _dot(a, b):
    return jnp.dot(a, b, preferred_element_type=F32)


def _dot_nt(a, b):
    return lax.dot_general(a, b, (((1,), (1,)), ((), ())), preferred_element_type=F32)


def _modvec_kernel(c_ref, w_ref, b_ref, o_ref):
    c = c_ref[...]
    s = (c * _sigmoid(c)).astype(BF16)
    o_ref[0] = _dot(s, w_ref[0].astype(BF16)) + b_ref[0]


def _modvec_call(cvec, mod_w, mod_b):
    L, D, N = mod_w.shape
    tn = math.gcd(D, 1024)
    return pl.pallas_call(
        _modvec_kernel,
        grid=(L, N // tn),
        in_specs=[
            pl.BlockSpec((8, D), lambda l, j: (0, 0)),
            pl.BlockSpec((1, D, tn), lambda l, j: (l, 0, j)),
            pl.BlockSpec((1, 1, tn), lambda l, j: (l, 0, j)),
        ],
        out_specs=pl.BlockSpec((1, 8, tn), lambda l, j: (l, 0, j)),
        out_shape=jax.ShapeDtypeStruct((L, 8, N), F32),
        compiler_params=_cparams(("parallel", "parallel")),
        name="modvec",
    )(cvec, mod_w, mod_b.reshape(L, 1, N))


def _modulate(h, g, mod):
    return (_rms(h) * g * (1.0 + mod[1:2, :]) + mod[0:1, :]).astype(BF16)


def _ffn_kernel(h_ref, hc_ref, mod_ref, g_ref, w1_ref, w3_ref, w2_ref, fg_ref, o_ref, hm_ref, acc_ref, *,
                final_norm, n_lat_tiles):
    j = pl.program_id(1)
    tm, d = o_ref.shape

    def load_h():
        if n_lat_tiles is None:
            return h_ref[...]
        return jnp.where(pl.program_id(0) < n_lat_tiles, h_ref[...], hc_ref[...].reshape(tm, d))

    @pl.when(j == 0)
    def _():
        hm_ref[...] = _modulate(load_h(), g_ref[...], mod_ref[0])
        acc_ref[...] = jnp.zeros_like(acc_ref)

    hm = hm_ref[...]
    a = _dot(hm, w1_ref[...])
    b = _dot(hm, w3_ref[...])
    gg = (a * _sigmoid(a)) * b
    acc_ref[...] += _dot(gg.astype(BF16), w2_ref[...])

    @pl.when(j == pl.num_programs(1) - 1)
    def _():
        out = load_h() + (0.5 * mod_ref[0, 2:3, :]) * acc_ref[...]
        if final_norm:
            out = _rms(out) * fg_ref[...]
        o_ref[...] = out


def _ffn_call(h, hc, mod, g, w1, w3, w2, fg, *, wsel, n_rows, seg_rows, n_seg, tm, tf, final_norm, out_3d=False):
    D = h.shape[-1]
    FF = w1.shape[-1]
    wl, wk = wsel
    tpb = seg_rows // tm
    seg = lambda i, j: (jnp.minimum(i * tm // seg_rows, n_seg), 0, 0)
    if hc is None:
        n_lat_tiles = None
        h_specs = [pl.BlockSpec((tm, D), lambda i, j: (i, 0)), pl.BlockSpec((8, D), lambda i, j: (0, 0))]
        hc = h
    else:
        tc = hc.shape[1]
        assert tm % tc == 0 and h.shape[1] == seg_rows
        n_lat_tiles = h.shape[0] * tpb
        lat = lambda i: jnp.minimum(i, n_lat_tiles - 1)
        h_specs = [pl.BlockSpec((None, tm, D), lambda i, j: (lat(i) // tpb, lat(i) % tpb, 0)),
                   pl.BlockSpec((tm // tc, tc, D), lambda i, j: (jnp.maximum(i - n_lat_tiles, 0), 0, 0))]
    if out_3d:
        assert n_rows % seg_rows == 0
        out_spec = pl.BlockSpec((None, tm, D), lambda i, j: (i // tpb, i % tpb, 0))
        out_shape = jax.ShapeDtypeStruct((n_rows // seg_rows, seg_rows, D), F32)
    else:
        out_spec = pl.BlockSpec((tm, D), lambda i, j: (i, 0))
        out_shape = jax.ShapeDtypeStruct((n_rows, D), F32)
    return pl.pallas_call(
        functools.partial(_ffn_kernel, final_norm=final_norm, n_lat_tiles=n_lat_tiles),
        grid=(n_rows // tm, FF // tf),
        in_specs=h_specs + [
            pl.BlockSpec((1, 8, D), seg),
            pl.BlockSpec((1, D), lambda i, j: (0, 0)),
            pl.BlockSpec((None, None, D, tf), lambda i, j: (wl, wk, 0, j)),
            pl.BlockSpec((None, None, D, tf), lambda i, j: (wl, wk, 0, j)),
            pl.BlockSpec((None, None, tf, D), lambda i, j: (wl, wk, j, 0)),
            pl.BlockSpec((1, D), lambda i, j: (0, 0)),
        ],
        out_specs=out_spec,
        out_shape=out_shape,
        scratch_shapes=[pltpu.VMEM((tm, D), BF16), pltpu.VMEM((tm, D), F32)],
        compiler_params=_cparams(("parallel", "arbitrary")),
        name="ffn",
    )(h, hc, mod, g, w1, w3, w2, fg)


def _inproj_kernel(h_ref, mod_ref, g_ref, w_ref, o_ref, hm_ref):
    @pl.when(pl.program_id(1) == 0)
    def _():
        hm_ref[...] = _modulate(h_ref[...], g_ref[...], mod_ref[0])

    o_ref[...] = _dot(hm_ref[...], w_ref[...])


def _inproj_call(h, mod, g, w, *, seg_rows, n_seg, tm, tn):
    R, D = h.shape
    N = w.shape[1]
    seg = lambda i, j: (jnp.minimum(i * tm // seg_rows, n_seg), 0, 0)
    return pl.pallas_call(
        _inproj_kernel,
        grid=(R // tm, N // tn),
        in_specs=[
            pl.BlockSpec((tm, D), lambda i, j: (i, 0)),
            pl.BlockSpec((1, 8, D), seg),
            pl.BlockSpec((1, D), lambda i, j: (0, 0)),
            pl.BlockSpec((D, tn), lambda i, j: (0, j)),
        ],
        out_specs=pl.BlockSpec((tm, tn), lambda i, j: (i, j)),
        out_shape=jax.ShapeDtypeStruct((R, N), F32),
        scratch_shapes=[pltpu.VMEM((tm, D), BF16)],
        compiler_params=_cparams(("parallel", "arbitrary")),
        name="inproj",
    )(h, mod, g, w)


def _outproj_kernel(xa_ref, xb_ref, wa_ref, wb_ref, h_ref, mod_ref, o_ref):
    y = _dot(xa_ref[...], wa_ref[...]) + _dot(xb_ref[...], wb_ref[...])
    o_ref[...] = h_ref[...] + mod_ref[0, 0:1, :] * y


def _outproj_call(xa, xb, w, h, mod, *, n_rows, seg_rows, n_seg, tm):
    D = h.shape[1]
    Ka, Kb = xa.shape[1], xb.shape[1]
    assert Ka == Kb and w.shape[0] == Ka + Kb
    seg = lambda i: (jnp.minimum(i * tm // seg_rows, n_seg), 0, 0)
    return pl.pallas_call(
        _outproj_kernel,
        grid=(n_rows // tm,),
        in_specs=[
            pl.BlockSpec((tm, Ka), lambda i: (i, 0)),
            pl.BlockSpec((tm, Kb), lambda i: (i, 0)),
            pl.BlockSpec((Ka, D), lambda i: (0, 0)),
            pl.BlockSpec((Kb, D), lambda i: (1, 0)),
            pl.BlockSpec((tm, D), lambda i: (i, 0)),
            pl.BlockSpec((1, 8, D), seg),
        ],
        out_specs=pl.BlockSpec((tm, D), lambda i: (i, 0)),
        out_shape=jax.ShapeDtypeStruct((n_rows, D), F32),
        compiler_params=_cparams(("parallel",)),
        name="outproj",
    )(xa, xb, w, w, h, mod)


def _rope_tables(T, head_dim):
    half = head_dim // 2
    n_freq = head_dim // 4
    t = jnp.arange(T)
    r = (t // GRID_W).astype(F32)
    cidx = (t % GRID_W).astype(F32)
    inv = ROPE_THETA ** (-jnp.arange(n_freq, dtype=F32) / n_freq)
    ang = jnp.concatenate([r[:, None] * inv, cidx[:, None] * inv], axis=-1)
    cos, sin = jnp.cos(ang), jnp.sin(ang)
    cos_h = jnp.concatenate([cos, cos], axis=-1)
    sin_h = jnp.concatenate([-sin, sin], axis=-1)
    reps = LANE // head_dim
    return jnp.tile(cos_h, (1, reps)), jnp.tile(sin_h, (1, reps))


def _rope128(x, cos, sin):
    return x * cos + pltpu.roll(x, 64, 1) * sin


def _rope64(x, cos, sin):
    lane = lax.broadcasted_iota(jnp.int32, x.shape, 1)
    first = jnp.bitwise_and(lane, 63) < 32
    partner = jnp.where(first, pltpu.roll(x, 96, 1), pltpu.roll(x, 32, 1))
    return x * cos + partner * sin


def _attn_a_kernel(q_ref, k_ref, v_ref, kc_ref, vc_ref, cq_ref, sq_ref, ck_ref, sk_ref, gq_ref, gk_ref,
                   o_ref, kall, vall, *, tc, t, rk, kc):
    grp = A_HEADS // A_KV

    @pl.when(pl.program_id(2) == 0)
    def _():
        gk = gk_ref[...]
        kall[0:tc, :] = (_rms(kc_ref[...]) * gk).astype(BF16)
        vall[0:tc, 0:A_DH] = vc_ref[...].astype(BF16)
        vall[:, A_DH:] = jnp.ones((tc + t, A_DH), BF16)

        def body(c, carry):
            rows = pl.ds(pl.multiple_of(c * rk, rk), rk)
            kn = _rms(k_ref[rows, :]) * gk
            dst = pl.ds(pl.multiple_of(tc + c * rk, math.gcd(tc, rk)), rk)
            kall[dst, :] = _rope128(kn, ck_ref[rows, :], sk_ref[rows, :]).astype(BF16)
            vall[dst, 0:A_DH] = v_ref[rows, :].astype(BF16)
            return carry

        lax.fori_loop(0, t // rk, body, 0)

    q4 = q_ref[...]
    qs = jnp.concatenate([q4[:, h * A_DH:(h + 1) * A_DH] for h in range(grp)], axis=0)
    qn = _rms(qs) * gq_ref[...]
    cos = jnp.concatenate([cq_ref[...]] * grp, axis=0)
    sin = jnp.concatenate([sq_ref[...]] * grp, axis=0)
    qr = (_rope128(qn, cos, sin) * (A_DH ** -0.5 * LOG2E)).astype(BF16)
    m = l = acc = None
    for k0, kn in [(0, tc)] + [(tc + c * kc, kc) for c in range(t // kc)]:
        s = _dot_nt(qr, kall[k0:k0 + kn, :])
        smax = jnp.max(s, axis=-1, keepdims=True)
        if m is None:
            m = smax
            acc = _dot(jnp.exp2(s - m).astype(BF16), vall[k0:k0 + kn, :])
        else:
            m_new = jnp.maximum(m, smax)
            acc = jnp.exp2(m - m_new) * acc + _dot(jnp.exp2(s - m_new).astype(BF16), vall[k0:k0 + kn, :])
            m = m_new
    o = acc[:, 0:A_DH] / acc[:, A_DH:]
    tq = q4.shape[0]
    for h in range(grp):
        o_ref[:, h * A_DH:(h + 1) * A_DH] = o[h * tq:(h + 1) * tq, :].astype(o_ref.dtype)


def _attn_a_call(z, cos, sin, gq, gk, *, B, T, Tc):
    tq = min(2 * Q_BLOCK, T)
    nq = T // tq
    grp_w = (A_HEADS // A_KV) * A_DH
    kcol = A_HEADS
    cbase = B * T // Tc
    rk = min(512, T)
    return pl.pallas_call(
        functools.partial(_attn_a_kernel, tc=Tc, t=T, rk=rk, kc=min(256, T)),
        grid=(B, A_KV, nq),
        in_specs=[
            pl.BlockSpec((tq, grp_w), lambda b, kv, i: (b * nq + i, kv)),
            pl.BlockSpec((T, A_DH), lambda b, kv, i: (b, kcol + kv)),
            pl.BlockSpec((T, A_DH), lambda b, kv, i: (b, kcol + A_KV + kv)),
            pl.BlockSpec((Tc, A_DH), lambda b, kv, i: (cbase + b, kcol + kv)),
            pl.BlockSpec((Tc, A_DH), lambda b, kv, i: (cbase + b, kcol + A_KV + kv)),
            pl.BlockSpec((tq, LANE), lambda b, kv, i: (i, 0)),
            pl.BlockSpec((tq, LANE), lambda b, kv, i: (i, 0)),
            pl.BlockSpec((T, LANE), lambda b, kv, i: (0, 0)),
            pl.BlockSpec((T, LANE), lambda b, kv, i: (0, 0)),
            pl.BlockSpec((1, A_DH), lambda b, kv, i: (0, 0)),
            pl.BlockSpec((1, A_DH), lambda b, kv, i: (0, 0)),
        ],
        out_specs=pl.BlockSpec((tq, grp_w), lambda b, kv, i: (b * nq + i, kv)),
        out_shape=jax.ShapeDtypeStruct((B * T, A_HEADS * A_DH), BF16),
        scratch_shapes=[pltpu.VMEM((Tc + T, A_DH), BF16), pltpu.VMEM((Tc + T, 2 * A_DH), BF16)],
        compiler_params=_cparams(("parallel", "parallel", "arbitrary")),
        name="attn_a",
    )(z, z, z, z, z, cos, sin, cos, sin, gq, gk)


def _attn_a_ctx_kernel(q_ref, kc_ref, vc_ref, gq_ref, gk_ref, o_ref):
    grp = A_HEADS // A_KV
    q4 = q_ref[...]
    tq = q4.shape[0]
    qs = jnp.concatenate([q4[:, h * A_DH:(h + 1) * A_DH] for h in range(grp)], axis=0)
    qn = (_rms(qs) * gq_ref[...] * (A_DH ** -0.5)).astype(BF16)
    kc = (_rms(kc_ref[...]) * gk_ref[...]).astype(BF16)
    s = _dot_nt(qn, kc)
    p = jnp.exp(s - jnp.max(s, axis=-1, keepdims=True))
    l = jnp.sum(p, axis=-1, keepdims=True)
    o = _dot(p.astype(BF16), vc_ref[...].astype(BF16)) / l
    for h in range(grp):
        o_ref[:, h * A_DH:(h + 1) * A_DH] = o[h * tq:(h + 1) * tq, :].astype(o_ref.dtype)


def _attn_a_ctx_call(z, gq, gk, *, B, T, Tc):
    grp_w = (A_HEADS // A_KV) * A_DH
    kcol = A_HEADS
    cbase = B * T // Tc
    return pl.pallas_call(
        _attn_a_ctx_kernel,
        grid=(B, A_KV),
        in_specs=[
            pl.BlockSpec((Tc, grp_w), lambda b, kv: (cbase + b, kv)),
            pl.BlockSpec((Tc, A_DH), lambda b, kv: (cbase + b, kcol + kv)),
            pl.BlockSpec((Tc, A_DH), lambda b, kv: (cbase + b, kcol + A_KV + kv)),
            pl.BlockSpec((1, A_DH), lambda b, kv: (0, 0)),
            pl.BlockSpec((1, A_DH), lambda b, kv: (0, 0)),
        ],
        out_specs=pl.BlockSpec((Tc, grp_w), lambda b, kv: (b, kv)),
        out_shape=jax.ShapeDtypeStruct((B * Tc, A_HEADS * A_DH), BF16),
        compiler_params=_cparams(("parallel", "parallel")),
        name="attn_a_ctx",
    )(z, z, z, gq, gk)


MLSTM_CAUG = B_DV + 8


def _mlstm_local_kernel(q_ref, k_ref, v0_ref, v1_ref, gc_ref, gr_ref, bcol_ref, brow_ref,
                        nloc_ref, stat_ref, scal_ref, uaug_ref, *, cb):
    L = B_CHUNK
    row = lax.broadcasted_iota(jnp.int32, (L, L), 0)
    col = lax.broadcasted_iota(jnp.int32, (L, L), 1)
    gcol = (gc_ref[...] + bcol_ref[...]).reshape(cb, L, LANE)
    grow = gr_ref[...] + brow_ref[...][None]
    sub0 = lax.broadcasted_iota(jnp.int32, (cb, 8, B_DQK), 1) == 0
    n_stat = 3 * B_HEADS
    stat_ref[:, :, n_stat:] = jnp.zeros((2, cb * L, stat_ref.shape[2] - n_stat), F32)
    for hd in range(B_HEADS):
        q = q_ref[:, hd * B_DQK:(hd + 1) * B_DQK].reshape(cb, L, B_DQK)
        k = k_ref[:, hd * B_DQK:(hd + 1) * B_DQK].reshape(cb, L, B_DQK)
        v_ref = v0_ref if hd < 2 else v1_ref
        v = v_ref[:, (hd % 2) * B_DV:(hd % 2 + 1) * B_DV].reshape(cb, L, B_DV)
        qb = (q * (B_DQK ** -0.5)).astype(BF16)
        kb = k.astype(BF16)
        vb = v.astype(BF16)
        qk = jnp.einsum('cld,csd->cls', qb, kb, preferred_element_type=F32)
        for d in range(2):
            gi = 2 * d * B_HEADS + hd
            gf = gi + B_HEADS
            li_col = gcol[:, :, gi:gi + 1]
            lf_col = _log_sigmoid(gcol[:, :, gf:gf + 1])
            li_row = grow[:, gi:gi + 1, :]
            lf_row = _log_sigmoid(grow[:, gf:gf + 1, :])
            mask = ((col <= row) if d == 0 else (col >= row))[None]
            mask_t = ((row <= col) if d == 0 else (row >= col))[None]
            bcum_col = jnp.sum(jnp.where(mask, lf_row, 0.0), axis=2, keepdims=True)
            bcum_row = jnp.sum(jnp.where(mask_t, lf_col, 0.0), axis=1, keepdims=True)
            log_d = jnp.where(mask, bcum_col - bcum_row + li_row, -jnp.inf)
            a = jnp.max(log_d, axis=2, keepdims=True)
            s = qk * jnp.exp(log_d - a)
            nloc = jnp.einsum('cls,csv->clv', s.astype(BF16), vb, preferred_element_type=F32)
            dloc = jnp.sum(s, axis=2, keepdims=True)
            b_last = jnp.sum(lf_row, axis=2, keepdims=True)
            log_w = b_last - bcum_col + li_col
            b_loc = jnp.max(log_w, axis=1, keepdims=True)
            w = jnp.exp(log_w - b_loc)
            u = jnp.einsum('csv,csd->cvd', (w * v).astype(BF16), kb, preferred_element_type=F32)
            nu = jnp.sum(w * kb.astype(F32), axis=1, keepdims=True)
            nloc_ref[d, :, hd * B_DV:(hd + 1) * B_DV] = nloc.reshape(cb * L, B_DV)
            stat_ref[d, :, hd:hd + 1] = (a - bcum_col).reshape(cb * L, 1)
            stat_ref[d, :, B_HEADS + hd:B_HEADS + hd + 1] = dloc.reshape(cb * L, 1)
            stat_ref[d, :, 2 * B_HEADS + hd:2 * B_HEADS + hd + 1] = bcum_col.reshape(cb * L, 1)
            scal_ref[d, :, hd:hd + 1, :] = jnp.broadcast_to(b_last, (cb, 1, LANE))
            scal_ref[d, :, B_HEADS + hd:B_HEADS + hd + 1, :] = jnp.broadcast_to(b_loc, (cb, 1, LANE))
            uaug_ref[d, :, hd, 0:B_DV, :] = u
            uaug_ref[d, :, hd, B_DV:MLSTM_CAUG, :] = jnp.where(sub0, nu, 0.0)


def _mlstm_local_call(z, grow, bcol, brow, *, cb):
    L = B_CHUNK
    R = z.shape[0]
    nchunk = R // L
    W = B_HEADS * B_DV
    return pl.pallas_call(
        functools.partial(_mlstm_local_kernel, cb=cb),
        grid=(nchunk // cb,),
        in_specs=[
            pl.BlockSpec((cb * L, 512), lambda i: (i, 3)),
            pl.BlockSpec((cb * L, 512), lambda i: (i, 4)),
            pl.BlockSpec((cb * L, 512), lambda i: (i, 5)),
            pl.BlockSpec((cb * L, 512), lambda i: (i, 6)),
            pl.BlockSpec((cb * L, LANE), lambda i: (i, AB_GATE_COL // LANE)),
            pl.BlockSpec((cb, 16, L), lambda i: (i, 0, 0)),
            pl.BlockSpec((1, LANE), lambda i: (0, 0)),
            pl.BlockSpec((16, 1), lambda i: (0, 0)),
        ],
        out_specs=[
            pl.BlockSpec((2, cb * L, W), lambda i: (0, i, 0)),
            pl.BlockSpec((2, cb * L, 16), lambda i: (0, i, 0)),
            pl.BlockSpec((2, cb, 8, LANE), lambda i: (0, i, 0, 0)),
            pl.BlockSpec((2, cb, B_HEADS, MLSTM_CAUG, B_DQK), lambda i: (0, i, 0, 0, 0)),
        ],
        out_shape=[
            jax.ShapeDtypeStruct((2, R, W), F32),
            jax.ShapeDtypeStruct((2, R, 16), F32),
            jax.ShapeDtypeStruct((2, nchunk, 8, LANE), F32),
            jax.ShapeDtypeStruct((2, nchunk, B_HEADS, MLSTM_CAUG, B_DQK), F32),
        ],
        compiler_params=_cparams(("parallel",)),
        name="mlstm_local",
    )(z, z, z, z, z, grow, bcol, brow)


def _mlstm_scan_kernel(qf_ref, nf_ref, stf_ref, scf_ref, uf_ref, qb_ref, nb_ref, stb_ref, scb_ref, ub_ref,
                       of_ref, ob_ref, c_s, m_s, *, cps):
    @pl.when(pl.program_id(1) == 0)
    def _():
        c_s[...] = jnp.zeros_like(c_s)
        m_s[...] = jnp.zeros_like(m_s)

    L = B_CHUNK
    units = []
    for d, (q_ref, n_ref, st_ref, sc_ref, u_ref, o_ref) in enumerate((
            (qf_ref, nf_ref, stf_ref, scf_ref, uf_ref, of_ref),
            (qb_ref, nb_ref, stb_ref, scb_ref, ub_ref, ob_ref))):
        for hd in range(B_HEADS):
            u = d * B_HEADS + hd
            units.append((u, d, hd, q_ref, n_ref, st_ref, sc_ref, u_ref, o_ref,
                          c_s[u, 0:B_DV, :], c_s[u, B_DV:, :], m_s[u, 0:1, :]))
    final_state = []
    for u, d, hd, q_ref, n_ref, st_ref, sc_ref, u_ref, o_ref, c_mat, n_rep, m_row in units:
        for step in range(cps):
            ck = step if d == 0 else cps - 1 - step
            rows = slice(ck * L, (ck + 1) * L)
            st = st_ref[0, rows, :]
            rep = lambda j: jnp.broadcast_to(st[:, j:j + 1], (L, LANE))
            g, dloc, bcum = rep(hd), rep(B_HEADS + hd), rep(2 * B_HEADS + hd)
            m = jnp.broadcast_to(m_row, (L, LANE))
            e_intra = jnp.exp(jnp.minimum(g - m, 0.0))
            w_inter = jnp.exp(jnp.minimum(m - g, 0.0))
            floor = jnp.exp(-(bcum + jnp.maximum(m, g)))
            qb = (q_ref[rows, hd * B_DQK:(hd + 1) * B_DQK] * (B_DQK ** -0.5)).astype(BF16)
            c_prev = jnp.concatenate([c_mat, n_rep], axis=0).astype(BF16)
            inter = _dot_nt(qb, c_prev)
            den = e_intra * dloc + w_inter * inter[:, B_DV:B_DV + LANE]
            r = 1.0 / jnp.maximum(jnp.abs(den), floor)
            for j in range(B_DV // LANE):
                c0 = hd * B_DV + j * LANE
                num = e_intra * n_ref[0, rows, c0:c0 + LANE] + w_inter * inter[:, j * LANE:(j + 1) * LANE]
                o_ref[0, rows, c0:c0 + LANE] = num * r
            b_last = sc_ref[0, ck, hd:hd + 1, :]
            b_loc = sc_ref[0, ck, B_HEADS + hd:B_HEADS + hd + 1, :]
            m_new = jnp.maximum(b_last + m_row, b_loc)
            decay = jnp.exp(b_last + m_row - m_new)
            scl = jnp.exp(b_loc - m_new)
            c_mat = decay * c_mat + scl * u_ref[0, ck, hd, 0:B_DV, :]
            n_rep = decay * n_rep + scl * jnp.broadcast_to(u_ref[0, ck, hd, B_DV:B_DV + 1, :], (LANE, B_DQK))
            m_row = m_new
        final_state.append((u, c_mat, n_rep, m_row))
    for u, c_mat, n_rep, m_row in final_state:
        c_s[u, 0:B_DV, :] = c_mat
        c_s[u, B_DV:, :] = n_rep
        m_s[u] = jnp.broadcast_to(m_row, m_s.shape[1:])


def _mlstm_scan_call(z, nloc, stat, scal, uaug, *, B, T, Tc):
    L = B_CHUNK
    ncl, ncc = T // L, Tc // L
    cps = math.gcd(math.gcd(ncl, ncc), 4)
    npl, npc = ncl // cps, ncc // cps
    cbase = B * T // (L * cps)
    R = z.shape[0]
    W = B_HEADS * B_DV

    def blk(b, s, d):
        is_ctx = s < npc
        jc = s if d == 0 else npc - 1 - s
        jl = (s - npc) if d == 0 else (npl - 1 - (s - npc))
        return jnp.where(is_ctx, cbase + b * npc + jc, b * npl + jl)

    def specs(d):
        return [
            pl.BlockSpec((cps * L, 512), lambda b, s: (blk(b, s, d), 3)),
            pl.BlockSpec((1, cps * L, W), lambda b, s: (d, blk(b, s, d), 0)),
            pl.BlockSpec((1, cps * L, 16), lambda b, s: (d, blk(b, s, d), 0)),
            pl.BlockSpec((1, cps, 8, LANE), lambda b, s: (d, blk(b, s, d), 0, 0)),
            pl.BlockSpec((1, cps, B_HEADS, MLSTM_CAUG, B_DQK), lambda b, s: (d, blk(b, s, d), 0, 0, 0)),
        ]

    out_sds = jax.ShapeDtypeStruct((1, R, W), F32)
    return pl.pallas_call(
        functools.partial(_mlstm_scan_kernel, cps=cps),
        grid=(B, npl + npc),
        in_specs=specs(0) + specs(1),
        out_specs=[
            pl.BlockSpec((1, cps * L, W), lambda b, s: (0, blk(b, s, 0), 0)),
            pl.BlockSpec((1, cps * L, W), lambda b, s: (0, blk(b, s, 1), 0)),
        ],
        out_shape=[out_sds, out_sds],
        scratch_shapes=[
            pltpu.VMEM((2 * B_HEADS, B_DV + LANE, B_DQK), F32),
            pltpu.VMEM((2 * B_HEADS, 8, LANE), F32),
        ],
        compiler_params=_cparams(("parallel", "arbitrary")),
        name="mlstm_scan",
    )(z, nloc, stat, scal, uaug, z, nloc, stat, scal, uaug)


def _mlstm_finish_kernel(hf_ref, hb_ref, o0_ref, o1_ref, hg_ref, y_ref):
    for hd in range(B_HEADS):
        sl = slice(hd * B_DV, (hd + 1) * B_DV)
        hh = hf_ref[0, :, sl] + hb_ref[0, :, sl]
        o_ref = o0_ref if hd < 2 else o1_ref
        o = o_ref[:, (hd % 2) * B_DV:(hd % 2 + 1) * B_DV]
        y_ref[:, sl] = (_sigmoid(o) * _rms(hh) * hg_ref[:, sl]).astype(y_ref.dtype)


def _mlstm_finish_call(hf, hb, z, head_g, *, tm):
    R = z.shape[0]
    W = B_HEADS * B_DV
    return pl.pallas_call(
        _mlstm_finish_kernel,
        grid=(R // tm,),
        in_specs=[
            pl.BlockSpec((1, tm, W), lambda i: (0, i, 0)),
            pl.BlockSpec((1, tm, W), lambda i: (0, i, 0)),
            pl.BlockSpec((tm, 512), lambda i: (i, 7)),
            pl.BlockSpec((tm, 512), lambda i: (i, 8)),
            pl.BlockSpec((1, W), lambda i: (0, 0)),
        ],
        out_specs=pl.BlockSpec((tm, W), lambda i: (i, 0)),
        out_shape=jax.ShapeDtypeStruct((R, W), BF16),
        compiler_params=_cparams(("parallel",)),
        name="mlstm_finish",
    )(hf, hb, z, z, head_g)


def _s5_params(a_re, a_im, log_dt, b_re, b_im, c_re, c_im, reverse):
    hp = lax.Precision.HIGHEST
    Lc = S5_CHUNK
    dt = jnp.exp(log_dt)[:, None]
    lam_re, lam_im = a_re * dt, a_im * dt
    mag = jnp.exp(lam_re)
    ab_re, ab_im = mag * jnp.cos(lam_im), mag * jnp.sin(lam_im)
    den = a_re * a_re + a_im * a_im
    nr, ni = ab_re - 1.0, ab_im
    k_re, k_im = (nr * a_re + ni * a_im) / den, (ni * a_re - nr * a_im) / den
    bb_re = k_re[..., None] * b_re - k_im[..., None] * b_im
    bb_im = k_re[..., None] * b_im + k_im[..., None] * b_re
    tau = jnp.arange(Lc + 1, dtype=F32)[:, None, None]
    pm = jnp.exp(tau * lam_re)
    pw_re, pw_im = pm * jnp.cos(tau * lam_im), pm * jnp.sin(tau * lam_im)
    cp_re = c_re[None] * pw_re[:, :, None, :] - c_im[None] * pw_im[:, :, None, :]
    cp_im = c_re[None] * pw_im[:, :, None, :] + c_im[None] * pw_re[:, :, None, :]
    kk = jnp.einsum('tgoq,gqi->tgoi', jnp.concatenate([cp_re, -cp_im], axis=-1),
                    jnp.concatenate([bb_re, bb_im], axis=1), precision=hp)
    G = a_re.shape[0]
    i_idx = jnp.arange(Lc)[:, None]
    j_idx = jnp.arange(Lc)[None, :]
    lag = (i_idx - j_idx) if reverse else (j_idx - i_idx)
    valid = lag >= 0
    kt = jnp.where(valid[:, :, None, None, None], kk[jnp.clip(lag, 0, Lc)], 0.0)
    kt = kt.transpose(2, 0, 4, 1, 3).reshape(G, Lc * C_GROUP, Lc * C_GROUP)
    e = jnp.arange(Lc) if reverse else (Lc - 1 - jnp.arange(Lc))
    w_re = pw_re[e][:, :, :, None] * bb_re[None] - pw_im[e][:, :, :, None] * bb_im[None]
    w_im = pw_re[e][:, :, :, None] * bb_im[None] + pw_im[e][:, :, :, None] * bb_re[None]
    w_re = w_re.transpose(1, 0, 3, 2).reshape(G, Lc * C_GROUP, C_STATE)
    w_im = w_im.transpose(1, 0, 3, 2).reshape(G, Lc * C_GROUP, C_STATE)
    even = (jnp.arange(G) % 2 == 0)[:, None, None]
    zc = jnp.zeros_like(w_re)
    half_cols = lambda m: jnp.where(even, jnp.concatenate([m, zc], axis=-1), jnp.concatenate([zc, m], axis=-1))
    wre, wim = half_cols(w_re), half_cols(w_im)
    f = (Lc - jnp.arange(Lc)) if reverse else (jnp.arange(Lc) + 1)
    cj_re = cp_re[f].transpose(1, 3, 0, 2).reshape(G, C_STATE, Lc * C_GROUP)
    cj_im = -cp_im[f].transpose(1, 3, 0, 2).reshape(G, C_STATE, Lc * C_GROUP)
    zr = jnp.zeros_like(cj_re)
    half_rows = lambda m: jnp.where(even, jnp.concatenate([m, zr], axis=1), jnp.concatenate([zr, m], axis=1))
    cre, cim = half_rows(cj_re), half_rows(cj_im)
    pair = lambda v: v.reshape(G // 2, 1, 2 * C_STATE)
    are, aim = pair(pw_re[Lc]), pair(pw_im[Lc])
    wcat = jnp.concatenate([wre, wim], axis=-1)
    ccat = jnp.concatenate([cre, cim], axis=1)
    return kt.astype(BF16), wcat.astype(BF16), ccat.astype(BF16), are, aim


def _granule_transpose(xs):
    xs = list(xs)
    gran = jnp.right_shift(lax.broadcasted_iota(jnp.int32, (1, LANE), 1), C_GROUP.bit_length() - 1)
    for bit in (1, 2, 4):
        hi = jnp.bitwise_and(gran, bit) != 0
        for a in range(S5_GB):
            if a & bit:
                continue
            p = a | bit
            xa, xp = xs[a], xs[p]
            xs[a] = jnp.where(hi, pltpu.roll(xp, bit * C_GROUP, 1), xa)
            xs[p] = jnp.where(hi, xp, pltpu.roll(xa, LANE - bit * C_GROUP, 1))
    return xs


def _s5_kernel(ul_ref, uc_ref, kt_ref, wcat_ref, ccat_ref, are_ref, aim_ref, y_ref,
               ug_s, wr_s, wi_s, xr_s, xi_s, y_s, *, nb, ncc, ncl):
    Lc = S5_CHUNK
    lat0 = nb * ncc
    for src, row0, n in ((uc_ref, 0, nb * ncc), (ul_ref, lat0, nb * ncl)):
        folded = _granule_transpose([src[pl.ds(i, n, stride=Lc), :] for i in range(Lc)])
        for g in range(S5_GB):
            ug_s[g, row0:row0 + n, :] = folded[g].astype(BF16)

    npair = S5_GB // 2
    for d in range(2):
        for k in range(npair):
            u0, u1 = ug_s[2 * k], ug_s[2 * k + 1]
            w = _dot(u0, wcat_ref[d, 2 * k]) + _dot(u1, wcat_ref[d, 2 * k + 1])
            wr_s[k] = w[:, 0:LANE]
            wi_s[k] = w[:, LANE:]
        are = [jnp.broadcast_to(are_ref[d, k], (nb, LANE)) for k in range(npair)]
        aim = [jnp.broadcast_to(aim_ref[d, k], (nb, LANE)) for k in range(npair)]

        def make_body(base, n):
            def body(s, carry):
                c = s if d == 0 else n - 1 - s
                rows = pl.ds(base + c, nb, stride=n)
                new = []
                for k in range(npair):
                    xr, xi = carry[2 * k], carry[2 * k + 1]
                    xr_s[k, rows, :] = xr
                    xi_s[k, rows, :] = xi
                    new.append(are[k] * xr - aim[k] * xi + wr_s[k, rows, :])
                    new.append(are[k] * xi + aim[k] * xr + wi_s[k, rows, :])
                return tuple(new)
            return body

        state = tuple(jnp.zeros((nb, LANE), F32) for _ in range(2 * npair))
        state = lax.fori_loop(0, ncc, make_body(0, ncc), state, unroll=4)
        lax.fori_loop(0, ncl, make_body(lat0, ncl), state, unroll=4)
        for g in range(S5_GB):
            y = (_dot(ug_s[g, lat0:, :], kt_ref[d, g])
                 + _dot(jnp.concatenate([xr_s[g // 2, lat0:, :], xi_s[g // 2, lat0:, :]], axis=1).astype(BF16),
                        ccat_ref[d, g]))
            if d == 0:
                y_s[g] = y
            else:
                y_s[g] += y

    unfolded = _granule_transpose([y_s[g] for g in range(S5_GB)])
    for j in range(Lc):
        y_ref[pl.ds(j, nb * ncl, stride=Lc), :] = unfolded[j]


def _s5_call(z, kt, wcat, ccat, are, aim, *, B, T, Tc, nb):
    Lc = S5_CHUNK
    ncl, ncc = T // Lc, Tc // Lc
    nr = nb * (ncl + ncc)
    cbase = B * T // (nb * Tc)
    wspec = pl.BlockSpec((2, S5_GB, LANE, LANE), lambda i, h: (0, i, 0, 0))
    aspec = pl.BlockSpec((2, S5_GB // 2, 1, LANE), lambda i, h: (0, i, 0, 0))
    big = lambda dt: pltpu.VMEM((S5_GB, nr, LANE), dt)
    half = pltpu.VMEM((S5_GB // 2, nr, LANE), F32)
    return pl.pallas_call(
        functools.partial(_s5_kernel, nb=nb, ncc=ncc, ncl=ncl),
        grid=(C_GROUPS // S5_GB, B // nb),
        in_specs=[
            pl.BlockSpec((nb * T, LANE), lambda i, h: (h, i)),
            pl.BlockSpec((nb * Tc, LANE), lambda i, h: (cbase + h, i)),
            wspec,
            pl.BlockSpec((2, S5_GB, LANE, 2 * LANE), lambda i, h: (0, i, 0, 0)),
            pl.BlockSpec((2, S5_GB, 2 * LANE, LANE), lambda i, h: (0, i, 0, 0)),
            aspec, aspec],
        out_specs=pl.BlockSpec((nb * T, LANE), lambda i, h: (h, i)),
        out_shape=jax.ShapeDtypeStruct((B * T, C_WIDTH), F32),
        scratch_shapes=[big(BF16), half, half, half, half,
                        pltpu.VMEM((S5_GB, nb * ncl, LANE), F32)],
        compiler_params=_cparams(("parallel", "parallel")),
        name="s5_scan",
    )(z, z, kt, wcat, ccat, are, aim)


def _s5_finish_kernel(y_ref, u_ref, d_ref, w_ref, b_ref, o_ref):
    y = y_ref[...] + d_ref[...] * u_ref[...]
    g = 0.5 * y * (1.0 + jnp.tanh(math.sqrt(2.0 / math.pi) * (y + 0.044715 * (y * y * y))))
    o_ref[...] = (g * _sigmoid(_dot(g.astype(BF16), w_ref[...]) + b_ref[...])).astype(o_ref.dtype)


def _s5_finish_call(y, z, d_skip, glu_w, glu_b, *, n_rows, tm):
    W = C_WIDTH
    return pl.pallas_call(
        _s5_finish_kernel,
        grid=(n_rows // tm,),
        in_specs=[
            pl.BlockSpec((tm, W), lambda i: (i, 0)),
            pl.BlockSpec((tm, W), lambda i: (i, 0)),
            pl.BlockSpec((1, W), lambda i: (0, 0)),
            pl.BlockSpec((W, W), lambda i: (0, 0)),
            pl.BlockSpec((1, W), lambda i: (0, 0)),
        ],
        out_specs=pl.BlockSpec((tm, W), lambda i: (i, 0)),
        out_shape=jax.ShapeDtypeStruct((n_rows, W), BF16),
        compiler_params=_cparams(("parallel",)),
        name="s5_finish",
    )(y, z, d_skip, glu_w, glu_b)


def _ones_lane(par):
    return D_DH if par == 0 else 0


def _attn_d_kernel(q_ref, k_ref, v_ref, kc_ref, vc_ref, cq_ref, sq_ref, ck_ref, sk_ref, sink_ref,
                   o_ref, kpe, kpo, vpe, vpo, kce, kco, vce, vco, band, *, t, tc, rk):
    W = D_WINDOW
    kv = pl.program_id(1)
    i = pl.program_id(2)
    lane = lax.broadcasted_iota(jnp.int32, (1, LANE), 1)
    lo = lane < D_DH

    def halves(x, ones=False):
        sel = jnp.where(kv == 0, x, pltpu.roll(x, 64, 1))
        e = jnp.where(lo, sel, 0.0)
        o = pltpu.roll(e, 64, 1)
        if ones:
            e = jnp.where(lane == _ones_lane(0), 1.0, e)
            o = jnp.where(lane == _ones_lane(1), 1.0, o)
        return e.astype(BF16), o.astype(BF16)

    @pl.when(i == 0)
    def _():
        zpad = jnp.zeros((W, LANE), BF16)
        for ref in (kpe, kpo, vpe, vpo):
            ref[0:W, :] = zpad
            ref[W + t:W + t + W, :] = zpad
        kce[...], kco[...] = halves(kc_ref[...])
        vce[...], vco[...] = halves(vc_ref[...], ones=True)

        def body(c, carry):
            rows = pl.ds(pl.multiple_of(c * rk, rk), rk)
            dst = pl.ds(pl.multiple_of(W + c * rk, math.gcd(W, rk)), rk)
            kpe[dst, :], kpo[dst, :] = halves(_rope64(k_ref[rows, :], ck_ref[rows, :], sk_ref[rows, :]))
            vpe[dst, :], vpo[dst, :] = halves(v_ref[rows, :], ones=True)
            return carry

        lax.fori_loop(0, t // rk, body, 0)

    tq = q_ref.shape[0]
    nslab = q_ref.shape[1] // LANE
    span = tq + 2 * W

    @pl.when(i == 0)
    def _():
        rr = jnp.bitwise_and(lax.broadcasted_iota(jnp.int32, (nslab * tq, span), 0), tq - 1)
        cc = lax.broadcasted_iota(jnp.int32, (nslab * tq, span), 1)
        band[...] = jnp.where(jnp.abs(rr + W - cc) <= W, 0.0, -jnp.inf)

    cos, sin = cq_ref[...], sq_ref[...]
    qs = jnp.concatenate(
        [_rope64(q_ref[:, j * LANE:(j + 1) * LANE], cos, sin) for j in range(nslab)], axis=0)
    qs = (qs * (D_DH ** -0.5 * LOG2E)).astype(BF16)
    rows_k = pl.ds(pl.multiple_of(i * tq, tq), span)
    kpos = i * tq - W + lax.broadcasted_iota(jnp.int32, (1, span), 1)
    bias = band[...] + jnp.where((kpos >= 0) & (kpos < t), 0.0, -jnp.inf)
    out = None
    lane_o = lax.broadcasted_iota(jnp.int32, (1, LANE), 1)
    for par, (kp, vp, kcx, vcx) in enumerate(((kpe, vpe, kce, vce), (kpo, vpo, kco, vco))):
        s = jnp.concatenate([_dot_nt(qs, kcx[...]), _dot_nt(qs, kp[rows_k, :]) + bias], axis=1)
        snk = LOG2E * jnp.concatenate(
            [jnp.broadcast_to(sink_ref[0, 0:1, 2 * j + par:2 * j + par + 1], (tq, 1)) for j in range(nslab)],
            axis=0)
        m = jnp.maximum(jnp.max(s, axis=-1, keepdims=True), snk)
        p = jnp.exp2(s - m).astype(BF16)
        o = _dot(p[:, 0:tc], vcx[...]) + _dot(p[:, tc:], vp[rows_k, :])
        one = _ones_lane(par)
        l = o[:, one:one + 1] + jnp.exp2(snk - m)
        own = (lane_o < D_DH) if par == 0 else (lane_o >= D_DH)
        o = jnp.where(own, o, 0.0) / l
        out = o if out is None else out + o
    for j in range(nslab):
        o_ref[:, j * LANE:(j + 1) * LANE] = out[j * tq:(j + 1) * tq, :].astype(o_ref.dtype)


def _attn_d_call(z, cos, sin, sink, *, B, T, Tc):
    tq = min(2 * Q_BLOCK, T)
    nq = T // tq
    grp_w = (D_HEADS // D_KV) * D_DH
    qcol = C_WIDTH // grp_w
    kcol = (C_WIDTH + D_HEADS * D_DH) // LANE
    cbase = B * T // Tc
    rk = min(512, T)
    pad = lambda n: pltpu.VMEM((n, LANE), BF16)
    return pl.pallas_call(
        functools.partial(_attn_d_kernel, t=T, tc=Tc, rk=rk),
        grid=(B, D_KV, nq),
        in_specs=[
            pl.BlockSpec((tq, grp_w), lambda b, kv, i: (b * nq + i, qcol + kv)),
            pl.BlockSpec((T, LANE), lambda b, kv, i: (b, kcol)),
            pl.BlockSpec((T, LANE), lambda b, kv, i: (b, kcol + 1)),
            pl.BlockSpec((Tc, LANE), lambda b, kv, i: (cbase + b, kcol)),
            pl.BlockSpec((Tc, LANE), lambda b, kv, i: (cbase + b, kcol + 1)),
            pl.BlockSpec((tq, LANE), lambda b, kv, i: (i, 0)),
            pl.BlockSpec((tq, LANE), lambda b, kv, i: (i, 0)),
            pl.BlockSpec((T, LANE), lambda b, kv, i: (0, 0)),
            pl.BlockSpec((T, LANE), lambda b, kv, i: (0, 0)),
            pl.BlockSpec((1, 1, LANE), lambda b, kv, i: (kv, 0, 0)),
        ],
        out_specs=pl.BlockSpec((tq, grp_w), lambda b, kv, i: (b * nq + i, kv)),
        out_shape=jax.ShapeDtypeStruct((B * T, D_HEADS * D_DH), BF16),
        scratch_shapes=[pad(T + 2 * D_WINDOW)] * 4 + [pad(Tc)] * 4
        + [pltpu.VMEM((grp_w // LANE * tq, tq + 2 * D_WINDOW), F32)],
        compiler_params=_cparams(("parallel", "parallel", "arbitrary")),
        name="attn_d",
    )(z, z, z, z, z, cos, sin, cos, sin, sink)


def _mod_rows(ml, idx):
    sel = ml[:, jnp.array(idx), :]
    return jnp.pad(sel, ((0, 0), (0, 8 - len(idx)), (0, 0)))


def kernel(x, c, ctx, c_ctx, mod_w, mod_b, norm_g, ffn_w1, ffn_w3, ffn_w2, ab_w_in, ab_b_gate, ab_w_out, a_gq, a_gk, b_head_g, cd_w_in, cd_w_out, s5_a_re, s5_a_im, s5_log_dt, s5_b_re, s5_b_im, s5_c_re, s5_c_im, s5_d, s5_glu_w, s5_glu_b, d_sink, final_g):
    B, T, D = x.shape
    Tc = ctx.shape[1]
    depth = mod_w.shape[0]
    n_lat, n_ctx = B * T, B * Tc
    n_all = n_lat + n_ctx
    tm = math.gcd(math.gcd(T, n_ctx), 512)
    tm_mm = math.gcd(math.gcd(T, n_ctx), 1024)
    tf = math.gcd(ffn_w1.shape[-1], 512)
    seg = dict(seg_rows=T, n_seg=B)

    h, hc0 = x, ctx
    cvec = jnp.zeros((8, D), F32).at[:B].set(c).at[B].set(c_ctx)
    mods = _modvec_call(cvec, mod_w, mod_b).reshape(depth, 8, N_MOD, D)[:, :B + 1]
    cos_a, sin_a = _rope_tables(T, A_DH)
    cos_d, sin_d = _rope_tables(T, D_DH)
    w1, w3, w2 = ffn_w1.astype(BF16), ffn_w3.astype(BF16), ffn_w2.astype(BF16)
    ones = jnp.ones((1, D), F32)

    for l in range(depth):
        last = l == depth - 1
        want_ctx = not last
        ml = mods[l]
        i = l // 2
        h = _ffn_call(h, hc0 if l == 0 else None, _mod_rows(ml, (0, 1, 2)), norm_g[l, 0][None],
                      w1, w3, w2, ones, wsel=(l, 0),
                      n_rows=n_all, tm=tm, tf=tf, final_norm=False, **seg)
        mod_in = _mod_rows(ml, (3, 4))
        mod_out = _mod_rows(ml, (5,))
        if l % 2 == 0:
            w_in = jnp.pad(ab_w_in[i], ((0, 0), (0, AB_IN_PAD - ab_w_in.shape[-1]))).astype(BF16)
            z = _inproj_call(h, mod_in, norm_g[l, 1][None], w_in, tm=tm_mm, tn=1024, **seg)
            ya = _attn_a_call(z, cos_a, sin_a, a_gq[i][None], a_gk[i][None], B=B, T=T, Tc=Tc)
            yac = _attn_a_ctx_call(z, a_gq[i][None], a_gk[i][None], B=B, T=T, Tc=Tc)
            ng = 4 * B_HEADS
            gates = z[:, AB_GATE_COL:AB_GATE_COL + ng]
            grow = gates.reshape(n_all // B_CHUNK, B_CHUNK, ng).transpose(0, 2, 1)
            bcol = jnp.pad(ab_b_gate[i], (0, LANE - ng))[None]
            brow = ab_b_gate[i][:, None]
            nloc, stat, scal, uaug = _mlstm_local_call(z, grow, bcol, brow, cb=math.gcd(n_all // B_CHUNK, 8))
            hf, hb = _mlstm_scan_call(z, nloc, stat, scal, uaug, B=B, T=T, Tc=Tc)
            yb = _mlstm_finish_call(hf, hb, z, b_head_g[i][None], tm=tm)
            xa = jnp.concatenate([ya, yac], axis=0)
            w_out = ab_w_out[i].astype(BF16)
            n_rows = n_all
        else:
            w_in = cd_w_in[i].astype(BF16)
            z = _inproj_call(h, mod_in, norm_g[l, 1][None], w_in, tm=tm_mm, tn=w_in.shape[1] // 3, **seg)
            prm = [_s5_params(s5_a_re[i, d], s5_a_im[i, d], s5_log_dt[i, d], s5_b_re[i, d], s5_b_im[i, d],
                              s5_c_re[i, d], s5_c_im[i, d], reverse=(d == 1)) for d in range(2)]
            s5w = [jnp.stack([prm[0][k], prm[1][k]]) for k in range(len(prm[0]))]
            ys5 = _s5_call(z, *s5w, B=B, T=T, Tc=Tc, nb=math.gcd(B, 2))
            xa = _s5_finish_call(ys5, z, s5_d[i][None], s5_glu_w[i].astype(BF16), s5_glu_b[i][None],
                                 n_rows=n_lat, tm=tm)
            sink = jnp.pad(d_sink[i].reshape(D_KV, 1, D_HEADS // D_KV), ((0, 0), (0, 0), (0, LANE - D_HEADS // D_KV)))
            yb = _attn_d_call(z, cos_d, sin_d, sink, B=B, T=T, Tc=Tc)
            w_out = cd_w_out[i].astype(BF16)
            n_rows = n_all if want_ctx else n_lat
        if not want_ctx:
            n_rows = n_lat
        h = _outproj_call(xa, yb, w_out, h, mod_out, n_rows=n_rows, tm=tm, **seg)
        h = _ffn_call(h, None, _mod_rows(ml, (6, 7, 8)), norm_g[l, 2][None], w1, w3, w2,
                      final_g[None] if last else ones, wsel=(l, 1), n_rows=n_rows, tm=tm, tf=tf, final_norm=last,
                      out_3d=last, **seg)
    return h
```

```python
import functools
import math

import jax
import jax.numpy as jnp
from jax import lax
from jax.experimental import pallas as pl
from jax.experimental.pallas import tpu as pltpu

F32 = jnp.float32
BF16 = jnp.bfloat16

EPS = 1e-6
ROPE_THETA = 10000.0
GRID_W = 64
N_MOD = 9
Q_BLOCK = 128

A_HEADS, A_KV, A_DH = 8, 2, 128
B_HEADS, B_DQK, B_DV, B_CHUNK = 4, 128, 256, 64
C_WIDTH, C_GROUP, C_STATE = 1024, 16, 64
C_GROUPS = C_WIDTH // C_GROUP
D_HEADS, D_KV, D_DH, D_WINDOW = 16, 2, 64, 128

AB_GATE_COL = 4608
AB_IN_PAD = 5120
S5_CHUNK = 8
S5_GB = 8

LANE = 128
LOG2E = math.log2(math.e)
VMEM_LIMIT = 56 * 1024 * 1024


def _cparams(sem, vmem_limit=VMEM_LIMIT):
    return pltpu.CompilerParams(dimension_semantics=sem, vmem_limit_bytes=vmem_limit)


def _rms(x):
    return x * lax.rsqrt(jnp.mean(x * x, axis=-1, keepdims=True) + EPS)


def _sigmoid(x):
    return 1.0 / (1.0 + jnp.exp(-x))


def _log_sigmoid(x):
    return jnp.minimum(x, 0.0) - jnp.log(1.0 + jnp.exp(-jnp.abs(x)))


def _dot(a, b):
    return jnp.dot(a, b, preferred_element_type=F32)


def _dot_nt(a, b):
    return lax.dot_general(a, b, (((1,), (1,)), ((), ())), preferred_element_type=F32)


def _modvec_kernel(c_ref, w_ref, b_ref, o_ref):
    c = c_ref[...]
    s = (c * _sigmoid(c)).astype(BF16)
    o_ref[0] = _dot(s, w_ref[0].astype(BF16)) + b_ref[0]


def _modvec_call(cvec, mod_w, mod_b):
    L, D, N = mod_w.shape
    tn = math.gcd(D, 1024)
    return pl.pallas_call(
        _modvec_kernel,
        grid=(L, N // tn),
        in_specs=[
            pl.BlockSpec((8, D), lambda l, j: (0, 0)),
            pl.BlockSpec((1, D, tn), lambda l, j: (l, 0, j)),
            pl.BlockSpec((1, 1, tn), lambda l, j: (l, 0, j)),
        ],
        out_specs=pl.BlockSpec((1, 8, tn), lambda l, j: (l, 0, j)),
        out_shape=jax.ShapeDtypeStruct((L, 8, N), F32),
        compiler_params=_cparams(("parallel", "parallel")),
        name="modvec",
    )(cvec, mod_w, mod_b.reshape(L, 1, N))


def _modulate(h, g, mod):
    return (_rms(h) * g * (1.0 + mod[1:2, :]) + mod[0:1, :]).astype(BF16)


def _ffn_kernel(h_ref, hc_ref, mod_ref, g_ref, w1_ref, w3_ref, w2_ref, fg_ref, o_ref, hm_ref, acc_ref, *,
                final_norm, n_lat_tiles, nj):
    j = pl.program_id(1)
    tm, d = o_ref.shape

    def load_h():
        if n_lat_tiles is None:
            return h_ref[...]
        return jnp.where(pl.program_id(0) < n_lat_tiles, h_ref[...], hc_ref[...].reshape(tm, d))

    def column_tile(first, last):
        if first:
            hm_ref[...] = _modulate(load_h(), g_ref[...], mod_ref[0])
        hm = hm_ref[...]
        a = _dot(hm, w1_ref[...])
        b = _dot(hm, w3_ref[...])
        gg = (a * _sigmoid(a)) * b
        down = _dot(gg.astype(BF16), w2_ref[...])
        total = down if first else acc_ref[...] + down
        if last:
            out = load_h() + (0.5 * mod_ref[0, 2:3, :]) * total
            if final_norm:
                out = _rms(out) * fg_ref[...]
            o_ref[...] = out
        else:
            acc_ref[...] = total

    if nj == 1:
        column_tile(True, True)
    else:
        pl.when(j == 0)(lambda: column_tile(True, False))
        pl.when((j > 0) & (j < nj - 1))(lambda: column_tile(False, False))
        pl.when(j == nj - 1)(lambda: column_tile(False, True))


def _ffn_call(h, hc, mod, g, w1, w3, w2, fg, *, wsel, n_rows, seg_rows, n_seg, tm, tf, final_norm, out_3d=False):
    D = h.shape[-1]
    FF = w1.shape[-1]
    wl, wk = wsel
    tpb = seg_rows // tm
    seg = lambda i, j: (jnp.minimum(i * tm // seg_rows, n_seg), 0, 0)
    if hc is None:
        n_lat_tiles = None
        h_specs = [pl.BlockSpec((tm, D), lambda i, j: (i, 0)), pl.BlockSpec((8, D), lambda i, j: (0, 0))]
        hc = h
    else:
        tc = hc.shape[1]
        assert tm % tc == 0 and h.shape[1] == seg_rows
        n_lat_tiles = h.shape[0] * tpb
        lat = lambda i: jnp.minimum(i, n_lat_tiles - 1)
        h_specs = [pl.BlockSpec((None, tm, D), lambda i, j: (lat(i) // tpb, lat(i) % tpb, 0)),
                   pl.BlockSpec((tm // tc, tc, D), lambda i, j: (jnp.maximum(i - n_lat_tiles, 0), 0, 0))]
    if out_3d:
        assert n_rows % seg_rows == 0
        out_spec = pl.BlockSpec((None, tm, D), lambda i, j: (i // tpb, i % tpb, 0))
        out_shape = jax.ShapeDtypeStruct((n_rows // seg_rows, seg_rows, D), F32)
    else:
        out_spec = pl.BlockSpec((tm, D), lambda i, j: (i, 0))
        out_shape = jax.ShapeDtypeStruct((n_rows, D), F32)
    return pl.pallas_call(
        functools.partial(_ffn_kernel, final_norm=final_norm, n_lat_tiles=n_lat_tiles, nj=FF // tf),
        grid=(n_rows // tm, FF // tf),
        in_specs=h_specs + [
            pl.BlockSpec((1, 8, D), seg),
            pl.BlockSpec((1, D), lambda i, j: (0, 0)),
            pl.BlockSpec((None, None, D, tf), lambda i, j: (wl, wk, 0, j)),
            pl.BlockSpec((None, None, D, tf), lambda i, j: (wl, wk, 0, j)),
            pl.BlockSpec((None, None, tf, D), lambda i, j: (wl, wk, j, 0)),
            pl.BlockSpec((1, D), lambda i, j: (0, 0)),
        ],
        out_specs=out_spec,
        out_shape=out_shape,
        scratch_shapes=[pltpu.VMEM((tm, D), BF16), pltpu.VMEM((tm, D), F32)],
        compiler_params=_cparams(("parallel", "arbitrary")),
        name="ffn",
    )(h, hc, mod, g, w1, w3, w2, fg)


def _inproj_kernel(h_ref, mod_ref, g_ref, w_ref, o_ref, hm_ref):
    @pl.when(pl.program_id(1) == 0)
    def _():
        hm_ref[...] = _modulate(h_ref[...], g_ref[...], mod_ref[0])

    o_ref[...] = _dot(hm_ref[...], w_ref[...])


def _inproj_call(h, mod, g, w, *, seg_rows, n_seg, tm, tn):
    R, D = h.shape
    N = w.shape[1]
    seg = lambda i, j: (jnp.minimum(i * tm // seg_rows, n_seg), 0, 0)
    return pl.pallas_call(
        _inproj_kernel,
        grid=(R // tm, N // tn),
        in_specs=[
            pl.BlockSpec((tm, D), lambda i, j: (i, 0)),
            pl.BlockSpec((1, 8, D), seg),
            pl.BlockSpec((1, D), lambda i, j: (0, 0)),
            pl.BlockSpec((D, tn), lambda i, j: (0, j)),
        ],
        out_specs=pl.BlockSpec((tm, tn), lambda i, j: (i, j)),
        out_shape=jax.ShapeDtypeStruct((R, N), F32),
        scratch_shapes=[pltpu.VMEM((tm, D), BF16)],
        compiler_params=_cparams(("parallel", "arbitrary")),
        name="inproj",
    )(h, mod, g, w)


def _outproj_kernel(xa_ref, xb_ref, wa_ref, wb_ref, h_ref, mod_ref, o_ref):
    y = _dot(xa_ref[...], wa_ref[...]) + _dot(xb_ref[...], wb_ref[...])
    o_ref[...] = h_ref[...] + mod_ref[0, 0:1, :] * y


def _outproj_call(xa, xb, w, h, mod, *, n_rows, seg_rows, n_seg, tm):
    D = h.shape[1]
    Ka, Kb = xa.shape[1], xb.shape[1]
    assert Ka == Kb and w.shape[0] == Ka + Kb
    seg = lambda i: (jnp.minimum(i * tm // seg_rows, n_seg), 0, 0)
    return pl.pallas_call(
        _outproj_kernel,
        grid=(n_rows // tm,),
        in_specs=[
            pl.BlockSpec((tm, Ka), lambda i: (i, 0)),
            pl.BlockSpec((tm, Kb), lambda i: (i, 0)),
            pl.BlockSpec((Ka, D), lambda i: (0, 0)),
            pl.BlockSpec((Kb, D), lambda i: (1, 0)),
            pl.BlockSpec((tm, D), lambda i: (i, 0)),
            pl.BlockSpec((1, 8, D), seg),
        ],
        out_specs=pl.BlockSpec((tm, D), lambda i: (i, 0)),
        out_shape=jax.ShapeDtypeStruct((n_rows, D), F32),
        compiler_params=_cparams(("parallel",)),
        name="outproj",
    )(xa, xb, w, w, h, mod)


def _rope_tables(T, head_dim):
    half = head_dim // 2
    n_freq = head_dim // 4
    t = jnp.arange(T)
    r = (t // GRID_W).astype(F32)
    cidx = (t % GRID_W).astype(F32)
    inv = ROPE_THETA ** (-jnp.arange(n_freq, dtype=F32) / n_freq)
    ang = jnp.concatenate([r[:, None] * inv, cidx[:, None] * inv], axis=-1)
    cos, sin = jnp.cos(ang), jnp.sin(ang)
    cos_h = jnp.concatenate([cos, cos], axis=-1)
    sin_h = jnp.concatenate([-sin, sin], axis=-1)
    reps = LANE // head_dim
    return jnp.tile(cos_h, (1, reps)), jnp.tile(sin_h, (1, reps))


def _rope128(x, cos, sin):
    return x * cos + pltpu.roll(x, 64, 1) * sin


def _rope64(x, cos, sin):
    lane = lax.broadcasted_iota(jnp.int32, x.shape, 1)
    first = jnp.bitwise_and(lane, 63) < 32
    partner = jnp.where(first, pltpu.roll(x, 96, 1), pltpu.roll(x, 32, 1))
    return x * cos + partner * sin


def _attn_a_kernel(q_ref, k_ref, v_ref, kc_ref, vc_ref, cq_ref, sq_ref, ck_ref, sk_ref, gq_ref, gk_ref,
                   o_ref, kall, vall, *, tc, t, rk, kc):
    grp = A_HEADS // A_KV

    @pl.when(pl.program_id(2) == 0)
    def _():
        gk = gk_ref[...]
        kall[0:tc, :] = (_rms(kc_ref[...]) * gk).astype(BF16)
        vall[0:tc, 0:A_DH] = vc_ref[...].astype(BF16)
        vall[:, A_DH:] = jnp.ones((tc + t, A_DH), BF16)

        def body(c, carry):
            rows = pl.ds(pl.multiple_of(c * rk, rk), rk)
            kn = _rms(k_ref[rows, :]) * gk
            dst = pl.ds(pl.multiple_of(tc + c * rk, math.gcd(tc, rk)), rk)
            kall[dst, :] = _rope128(kn, ck_ref[rows, :], sk_ref[rows, :]).astype(BF16)
            vall[dst, 0:A_DH] = v_ref[rows, :].astype(BF16)
            return carry

        lax.fori_loop(0, t // rk, body, 0)

    q4 = q_ref[...]
    qs = jnp.concatenate([q4[:, h * A_DH:(h + 1) * A_DH] for h in range(grp)], axis=0)
    qn = _rms(qs) * gq_ref[...]
    cos = jnp.concatenate([cq_ref[...]] * grp, axis=0)
    sin = jnp.concatenate([sq_ref[...]] * grp, axis=0)
    qr = (_rope128(qn, cos, sin) * (A_DH ** -0.5 * LOG2E)).astype(BF16)
    m = l = acc = None
    for k0, kn in [(0, tc)] + [(tc + c * kc, kc) for c in range(t // kc)]:
        s = _dot_nt(qr, kall[k0:k0 + kn, :])
        smax = jnp.max(s, axis=-1, keepdims=True)
        if m is None:
            m = smax
            acc = _dot(jnp.exp2(s - m).astype(BF16), vall[k0:k0 + kn, :])
        else:
            m_new = jnp.maximum(m, smax)
            acc = jnp.exp2(m - m_new) * acc + _dot(jnp.exp2(s - m_new).astype(BF16), vall[k0:k0 + kn, :])
            m = m_new
    o = acc[:, 0:A_DH] / acc[:, A_DH:]
    tq = q4.shape[0]
    for h in range(grp):
        o_ref[:, h * A_DH:(h + 1) * A_DH] = o[h * tq:(h + 1) * tq, :].astype(o_ref.dtype)


def _attn_a_call(z, cos, sin, gq, gk, *, B, T, Tc):
    tq = min(2 * Q_BLOCK, T)
    nq = T // tq
    grp_w = (A_HEADS // A_KV) * A_DH
    kcol = A_HEADS
    cbase = B * T // Tc
    rk = min(512, T)
    return pl.pallas_call(
        functools.partial(_attn_a_kernel, tc=Tc, t=T, rk=rk, kc=min(256, T)),
        grid=(B, A_KV, nq),
        in_specs=[
            pl.BlockSpec((tq, grp_w), lambda b, kv, i: (b * nq + i, kv)),
            pl.BlockSpec((T, A_DH), lambda b, kv, i: (b, kcol + kv)),
            pl.BlockSpec((T, A_DH), lambda b, kv, i: (b, kcol + A_KV + kv)),
            pl.BlockSpec((Tc, A_DH), lambda b, kv, i: (cbase + b, kcol + kv)),
            pl.BlockSpec((Tc, A_DH), lambda b, kv, i: (cbase + b, kcol + A_KV + kv)),
            pl.BlockSpec((tq, LANE), lambda b, kv, i: (i, 0)),
            pl.BlockSpec((tq, LANE), lambda b, kv, i: (i, 0)),
            pl.BlockSpec((T, LANE), lambda b, kv, i: (0, 0)),
            pl.BlockSpec((T, LANE), lambda b, kv, i: (0, 0)),
            pl.BlockSpec((1, A_DH), lambda b, kv, i: (0, 0)),
            pl.BlockSpec((1, A_DH), lambda b, kv, i: (0, 0)),
        ],
        out_specs=pl.BlockSpec((tq, grp_w), lambda b, kv, i: (b * nq + i, kv)),
        out_shape=jax.ShapeDtypeStruct((B * T, A_HEADS * A_DH), BF16),
        scratch_shapes=[pltpu.VMEM((Tc + T, A_DH), BF16), pltpu.VMEM((Tc + T, 2 * A_DH), BF16)],
        compiler_params=_cparams(("parallel", "parallel", "arbitrary")),
        name="attn_a",
    )(z, z, z, z, z, cos, sin, cos, sin, gq, gk)


def _attn_a_ctx_kernel(q_ref, kc_ref, vc_ref, gq_ref, gk_ref, o_ref):
    grp = A_HEADS // A_KV
    q4 = q_ref[...]
    tq = q4.shape[0]
    qs = jnp.concatenate([q4[:, h * A_DH:(h + 1) * A_DH] for h in range(grp)], axis=0)
    qn = (_rms(qs) * gq_ref[...] * (A_DH ** -0.5)).astype(BF16)
    kc = (_rms(kc_ref[...]) * gk_ref[...]).astype(BF16)
    s = _dot_nt(qn, kc)
    p = jnp.exp(s - jnp.max(s, axis=-1, keepdims=True))
    l = jnp.sum(p, axis=-1, keepdims=True)
    o = _dot(p.astype(BF16), vc_ref[...].astype(BF16)) / l
    for h in range(grp):
        o_ref[:, h * A_DH:(h + 1) * A_DH] = o[h * tq:(h + 1) * tq, :].astype(o_ref.dtype)


def _attn_a_ctx_call(z, gq, gk, *, B, T, Tc):
    grp_w = (A_HEADS // A_KV) * A_DH
    kcol = A_HEADS
    cbase = B * T // Tc
    return pl.pallas_call(
        _attn_a_ctx_kernel,
        grid=(B, A_KV),
        in_specs=[
            pl.BlockSpec((Tc, grp_w), lambda b, kv: (cbase + b, kv)),
            pl.BlockSpec((Tc, A_DH), lambda b, kv: (cbase + b, kcol + kv)),
            pl.BlockSpec((Tc, A_DH), lambda b, kv: (cbase + b, kcol + A_KV + kv)),
            pl.BlockSpec((1, A_DH), lambda b, kv: (0, 0)),
            pl.BlockSpec((1, A_DH), lambda b, kv: (0, 0)),
        ],
        out_specs=pl.BlockSpec((Tc, grp_w), lambda b, kv: (b, kv)),
        out_shape=jax.ShapeDtypeStruct((B * Tc, A_HEADS * A_DH), BF16),
        compiler_params=_cparams(("parallel", "parallel")),
        name="attn_a_ctx",
    )(z, z, z, gq, gk)


MLSTM_CAUG = B_DV + 8


def _mlstm_local_kernel(q_ref, k_ref, v0_ref, v1_ref, gc_ref, gr_ref, bcol_ref, brow_ref,
                        nloc_ref, stat_ref, scal_ref, uaug_ref, *, cb):
    L = B_CHUNK
    row = lax.broadcasted_iota(jnp.int32, (L, L), 0)
    col = lax.broadcasted_iota(jnp.int32, (L, L), 1)
    gcol = (gc_ref[...] + bcol_ref[...]).reshape(cb, L, LANE)
    grow = gr_ref[...] + brow_ref[...][None]
    sub0 = lax.broadcasted_iota(jnp.int32, (cb, 8, B_DQK), 1) == 0
    n_stat = 3 * B_HEADS
    stat_ref[:, :, n_stat:] = jnp.zeros((2, cb * L, stat_ref.shape[2] - n_stat), F32)
    for hd in range(B_HEADS):
        q = q_ref[:, hd * B_DQK:(hd + 1) * B_DQK].reshape(cb, L, B_DQK)
        k = k_ref[:, hd * B_DQK:(hd + 1) * B_DQK].reshape(cb, L, B_DQK)
        v_ref = v0_ref if hd < 2 else v1_ref
        v = v_ref[:, (hd % 2) * B_DV:(hd % 2 + 1) * B_DV].reshape(cb, L, B_DV)
        qb = (q * (B_DQK ** -0.5)).astype(BF16)
        kb = k.astype(BF16)
        vb = v.astype(BF16)
        qk = jnp.einsum('cld,csd->cls', qb, kb, preferred_element_type=F32)
        for d in range(2):
            gi = 2 * d * B_HEADS + hd
            gf = gi + B_HEADS
            li_col = gcol[:, :, gi:gi + 1]
            lf_col = _log_sigmoid(gcol[:, :, gf:gf + 1])
            li_row = grow[:, gi:gi + 1, :]
            lf_row = _log_sigmoid(grow[:, gf:gf + 1, :])
            mask = ((col <= row) if d == 0 else (col >= row))[None]
            mask_t = ((row <= col) if d == 0 else (row >= col))[None]
            bcum_col = jnp.sum(jnp.where(mask, lf_row, 0.0), axis=2, keepdims=True)
            bcum_row = jnp.sum(jnp.where(mask_t, lf_col, 0.0), axis=1, keepdims=True)
            log_d = jnp.where(mask, bcum_col - bcum_row + li_row, -jnp.inf)
            a = jnp.max(log_d, axis=2, keepdims=True)
            s = qk * jnp.exp(log_d - a)
            nloc = jnp.einsum('cls,csv->clv', s.astype(BF16), vb, preferred_element_type=F32)
            dloc = jnp.sum(s, axis=2, keepdims=True)
            b_last = jnp.sum(lf_row, axis=2, keepdims=True)
            log_w = b_last - bcum_col + li_col
            b_loc = jnp.max(log_w, axis=1, keepdims=True)
            w = jnp.exp(log_w - b_loc)
            u = jnp.einsum('csv,csd->cvd', (w * v).astype(BF16), kb, preferred_element_type=F32)
            nu = jnp.sum(w * kb.astype(F32), axis=1, keepdims=True)
            nloc_ref[d, :, hd * B_DV:(hd + 1) * B_DV] = nloc.reshape(cb * L, B_DV)
            stat_ref[d, :, hd:hd + 1] = (a - bcum_col).reshape(cb * L, 1)
            stat_ref[d, :, B_HEADS + hd:B_HEADS + hd + 1] = dloc.reshape(cb * L, 1)
            stat_ref[d, :, 2 * B_HEADS + hd:2 * B_HEADS + hd + 1] = bcum_col.reshape(cb * L, 1)
            scal_ref[d, :, hd:hd + 1, :] = jnp.broadcast_to(b_last, (cb, 1, LANE))
            scal_ref[d, :, B_HEADS + hd:B_HEADS + hd + 1, :] = jnp.broadcast_to(b_loc, (cb, 1, LANE))
            uaug_ref[d, :, hd, 0:B_DV, :] = u
            uaug_ref[d, :, hd, B_DV:MLSTM_CAUG, :] = jnp.where(sub0, nu, 0.0)


def _mlstm_local_call(z, grow, bcol, brow, *, cb):
    L = B_CHUNK
    R = z.shape[0]
    nchunk = R // L
    W = B_HEADS * B_DV
    return pl.pallas_call(
        functools.partial(_mlstm_local_kernel, cb=cb),
        grid=(nchunk // cb,),
        in_specs=[
            pl.BlockSpec((cb * L, 512), lambda i: (i, 3)),
            pl.BlockSpec((cb * L, 512), lambda i: (i, 4)),
            pl.BlockSpec((cb * L, 512), lambda i: (i, 5)),
            pl.BlockSpec((cb * L, 512), lambda i: (i, 6)),
            pl.BlockSpec((cb * L, LANE), lambda i: (i, AB_GATE_COL // LANE)),
            pl.BlockSpec((cb, 16, L), lambda i: (i, 0, 0)),
            pl.BlockSpec((1, LANE), lambda i: (0, 0)),
            pl.BlockSpec((16, 1), lambda i: (0, 0)),
        ],
        out_specs=[
            pl.BlockSpec((2, cb * L, W), lambda i: (0, i, 0)),
            pl.BlockSpec((2, cb * L, 16), lambda i: (0, i, 0)),
            pl.BlockSpec((2, cb, 8, LANE), lambda i: (0, i, 0, 0)),
            pl.BlockSpec((2, cb, B_HEADS, MLSTM_CAUG, B_DQK), lambda i: (0, i, 0, 0, 0)),
        ],
        out_shape=[
            jax.ShapeDtypeStruct((2, R, W), F32),
            jax.ShapeDtypeStruct((2, R, 16), F32),
            jax.ShapeDtypeStruct((2, nchunk, 8, LANE), F32),
            jax.ShapeDtypeStruct((2, nchunk, B_HEADS, MLSTM_CAUG, B_DQK), F32),
        ],
        compiler_params=_cparams(("parallel",)),
        name="mlstm_local",
    )(z, z, z, z, z, grow, bcol, brow)


def _mlstm_scan_kernel(qf_ref, nf_ref, stf_ref, scf_ref, uf_ref, qb_ref, nb_ref, stb_ref, scb_ref, ub_ref,
                       of_ref, ob_ref, c_s, m_s, *, cps):
    @pl.when(pl.program_id(1) == 0)
    def _():
        c_s[...] = jnp.zeros_like(c_s)
        m_s[...] = jnp.zeros_like(m_s)

    L = B_CHUNK
    units = []
    for d, (q_ref, n_ref, st_ref, sc_ref, u_ref, o_ref) in enumerate((
            (qf_ref, nf_ref, stf_ref, scf_ref, uf_ref, of_ref),
            (qb_ref, nb_ref, stb_ref, scb_ref, ub_ref, ob_ref))):
        for hd in range(B_HEADS):
            u = d * B_HEADS + hd
            units.append((u, d, hd, q_ref, n_ref, st_ref, sc_ref, u_ref, o_ref,
                          c_s[u, 0:B_DV, :], c_s[u, B_DV:, :], m_s[u, 0:1, :]))
    final_state = []
    for u, d, hd, q_ref, n_ref, st_ref, sc_ref, u_ref, o_ref, c_mat, n_rep, m_row in units:
        for step in range(cps):
            ck = step if d == 0 else cps - 1 - step
            rows = slice(ck * L, (ck + 1) * L)
            st = st_ref[0, rows, :]
            rep = lambda j: jnp.broadcast_to(st[:, j:j + 1], (L, LANE))
            g, dloc, bcum = rep(hd), rep(B_HEADS + hd), rep(2 * B_HEADS + hd)
            m = jnp.broadcast_to(m_row, (L, LANE))
            e_intra = jnp.exp(jnp.minimum(g - m, 0.0))
            w_inter = jnp.exp(jnp.minimum(m - g, 0.0))
            floor = jnp.exp(-(bcum + jnp.maximum(m, g)))
            qb = (q_ref[rows, hd * B_DQK:(hd + 1) * B_DQK] * (B_DQK ** -0.5)).astype(BF16)
            c_prev = jnp.concatenate([c_mat, n_rep], axis=0).astype(BF16)
            inter = _dot_nt(qb, c_prev)
            den = e_intra * dloc + w_inter * inter[:, B_DV:B_DV + LANE]
            r = 1.0 / jnp.maximum(jnp.abs(den), floor)
            for j in range(B_DV // LANE):
                c0 = hd * B_DV + j * LANE
                num = e_intra * n_ref[0, rows, c0:c0 + LANE] + w_inter * inter[:, j * LANE:(j + 1) * LANE]
                o_ref[0, rows, c0:c0 + LANE] = num * r
            b_last = sc_ref[0, ck, hd:hd + 1, :]
            b_loc = sc_ref[0, ck, B_HEADS + hd:B_HEADS + hd + 1, :]
            m_new = jnp.maximum(b_last + m_row, b_loc)
            decay = jnp.exp(b_last + m_row - m_new)
            scl = jnp.exp(b_loc - m_new)
            c_mat = decay * c_mat + scl * u_ref[0, ck, hd, 0:B_DV, :]
            n_rep = decay * n_rep + scl * jnp.broadcast_to(u_ref[0, ck, hd, B_DV:B_DV + 1, :], (LANE, B_DQK))
            m_row = m_new
        final_state.append((u, c_mat, n_rep, m_row))
    for u, c_mat, n_rep, m_row in final_state:
        c_s[u, 0:B_DV, :] = c_mat
        c_s[u, B_DV:, :] = n_rep
        m_s[u] = jnp.broadcast_to(m_row, m_s.shape[1:])


def _mlstm_scan_call(z, nloc, stat, scal, uaug, *, B, T, Tc):
    L = B_CHUNK
    ncl, ncc = T // L, Tc // L
    cps = math.gcd(math.gcd(ncl, ncc), 4)
    npl, npc = ncl // cps, ncc // cps
    cbase = B * T // (L * cps)
    R = z.shape[0]
    W = B_HEADS * B_DV

    def blk(b, s, d):
        is_ctx = s < npc
        jc = s if d == 0 else npc - 1 - s
        jl = (s - npc) if d == 0 else (npl - 1 - (s - npc))
        return jnp.where(is_ctx, cbase + b * npc + jc, b * npl + jl)

    def specs(d):
        return [
            pl.BlockSpec((cps * L, 512), lambda b, s: (blk(b, s, d), 3)),
            pl.BlockSpec((1, cps * L, W), lambda b, s: (d, blk(b, s, d), 0)),
            pl.BlockSpec((1, cps * L, 16), lambda b, s: (d, blk(b, s, d), 0)),
            pl.BlockSpec((1, cps, 8, LANE), lambda b, s: (d, blk(b, s, d), 0, 0)),
            pl.BlockSpec((1, cps, B_HEADS, MLSTM_CAUG, B_DQK), lambda b, s: (d, blk(b, s, d), 0, 0, 0)),
        ]

    out_sds = jax.ShapeDtypeStruct((1, R, W), F32)
    return pl.pallas_call(
        functools.partial(_mlstm_scan_kernel, cps=cps),
        grid=(B, npl + npc),
        in_specs=specs(0) + specs(1),
        out_specs=[
            pl.BlockSpec((1, cps * L, W), lambda b, s: (0, blk(b, s, 0), 0)),
            pl.BlockSpec((1, cps * L, W), lambda b, s: (0, blk(b, s, 1), 0)),
        ],
        out_shape=[out_sds, out_sds],
        scratch_shapes=[
            pltpu.VMEM((2 * B_HEADS, B_DV + LANE, B_DQK), F32),
            pltpu.VMEM((2 * B_HEADS, 8, LANE), F32),
        ],
        compiler_params=_cparams(("parallel", "arbitrary")),
        name="mlstm_scan",
    )(z, nloc, stat, scal, uaug, z, nloc, stat, scal, uaug)


def _mlstm_finish_kernel(hf_ref, hb_ref, o0_ref, o1_ref, hg_ref, y_ref):
    for hd in range(B_HEADS):
        sl = slice(hd * B_DV, (hd + 1) * B_DV)
        hh = hf_ref[0, :, sl] + hb_ref[0, :, sl]
        o_ref = o0_ref if hd < 2 else o1_ref
        o = o_ref[:, (hd % 2) * B_DV:(hd % 2 + 1) * B_DV]
        y_ref[:, sl] = (_sigmoid(o) * _rms(hh) * hg_ref[:, sl]).astype(y_ref.dtype)


def _mlstm_finish_call(hf, hb, z, head_g, *, tm):
    R = z.shape[0]
    W = B_HEADS * B_DV
    return pl.pallas_call(
        _mlstm_finish_kernel,
        grid=(R // tm,),
        in_specs=[
            pl.BlockSpec((1, tm, W), lambda i: (0, i, 0)),
            pl.BlockSpec((1, tm, W), lambda i: (0, i, 0)),
            pl.BlockSpec((tm, 512), lambda i: (i, 7)),
            pl.BlockSpec((tm, 512), lambda i: (i, 8)),
            pl.BlockSpec((1, W), lambda i: (0, 0)),
        ],
        out_specs=pl.BlockSpec((tm, W), lambda i: (i, 0)),
        out_shape=jax.ShapeDtypeStruct((R, W), BF16),
        compiler_params=_cparams(("parallel",)),
        name="mlstm_finish",
    )(hf, hb, z, z, head_g)


def _s5_params(a_re, a_im, log_dt, b_re, b_im, c_re, c_im, reverse):
    hp = lax.Precision.HIGHEST
    Lc = S5_CHUNK
    dt = jnp.exp(log_dt)[:, None]
    lam_re, lam_im = a_re * dt, a_im * dt
    mag = jnp.exp(lam_re)
    ab_re, ab_im = mag * jnp.cos(lam_im), mag * jnp.sin(lam_im)
    den = a_re * a_re + a_im * a_im
    nr, ni = ab_re - 1.0, ab_im
    k_re, k_im = (nr * a_re + ni * a_im) / den, (ni * a_re - nr * a_im) / den
    bb_re = k_re[..., None] * b_re - k_im[..., None] * b_im
    bb_im = k_re[..., None] * b_im + k_im[..., None] * b_re
    tau = jnp.arange(Lc + 1, dtype=F32)[:, None, None]
    pm = jnp.exp(tau * lam_re)
    pw_re, pw_im = pm * jnp.cos(tau * lam_im), pm * jnp.sin(tau * lam_im)
    cp_re = c_re[None] * pw_re[:, :, None, :] - c_im[None] * pw_im[:, :, None, :]
    cp_im = c_re[None] * pw_im[:, :, None, :] + c_im[None] * pw_re[:, :, None, :]
    kk = jnp.einsum('tgoq,gqi->tgoi', jnp.concatenate([cp_re, -cp_im], axis=-1),
                    jnp.concatenate([bb_re, bb_im], axis=1), precision=hp)
    G = a_re.shape[0]
    i_idx = jnp.arange(Lc)[:, None]
    j_idx = jnp.arange(Lc)[None, :]
    lag = (i_idx - j_idx) if reverse else (j_idx - i_idx)
    valid = lag >= 0
    kt = jnp.where(valid[:, :, None, None, None], kk[jnp.clip(lag, 0, Lc)], 0.0)
    kt = kt.transpose(2, 0, 4, 1, 3).reshape(G, Lc * C_GROUP, Lc * C_GROUP)
    e = jnp.arange(Lc) if reverse else (Lc - 1 - jnp.arange(Lc))
    w_re = pw_re[e][:, :, :, None] * bb_re[None] - pw_im[e][:, :, :, None] * bb_im[None]
    w_im = pw_re[e][:, :, :, None] * bb_im[None] + pw_im[e][:, :, :, None] * bb_re[None]
    w_re = w_re.transpose(1, 0, 3, 2).reshape(G, Lc * C_GROUP, C_STATE)
    w_im = w_im.transpose(1, 0, 3, 2).reshape(G, Lc * C_GROUP, C_STATE)
    even = (jnp.arange(G) % 2 == 0)[:, None, None]
    zc = jnp.zeros_like(w_re)
    half_cols = lambda m: jnp.where(even, jnp.concatenate([m, zc], axis=-1), jnp.concatenate([zc, m], axis=-1))
    wre, wim = half_cols(w_re), half_cols(w_im)
    f = (Lc - jnp.arange(Lc)) if reverse else (jnp.arange(Lc) + 1)
    cj_re = cp_re[f].transpose(1, 3, 0, 2).reshape(G, C_STATE, Lc * C_GROUP)
    cj_im = -cp_im[f].transpose(1, 3, 0, 2).reshape(G, C_STATE, Lc * C_GROUP)
    zr = jnp.zeros_like(cj_re)
    half_rows = lambda m: jnp.where(even, jnp.concatenate([m, zr], axis=1), jnp.concatenate([zr, m], axis=1))
    cre, cim = half_rows(cj_re), half_rows(cj_im)
    pair = lambda v: v.reshape(G // 2, 1, 2 * C_STATE)
    are, aim = pair(pw_re[Lc]), pair(pw_im[Lc])
    wcat = jnp.concatenate([wre, wim], axis=-1)
    ccat = jnp.concatenate([cre, cim], axis=1)
    return kt.astype(BF16), wcat.astype(BF16), ccat.astype(BF16), are, aim


def _granule_transpose(xs):
    xs = list(xs)
    gran = jnp.right_shift(lax.broadcasted_iota(jnp.int32, (1, LANE), 1), C_GROUP.bit_length() - 1)
    for bit in (1, 2, 4):
        hi = jnp.bitwise_and(gran, bit) != 0
        for a in range(S5_GB):
            if a & bit:
                continue
            p = a | bit
            xa, xp = xs[a], xs[p]
            xs[a] = jnp.where(hi, pltpu.roll(xp, bit * C_GROUP, 1), xa)
            xs[p] = jnp.where(hi, xp, pltpu.roll(xa, LANE - bit * C_GROUP, 1))
    return xs


def _s5_kernel(ul_ref, uc_ref, kt_ref, wcat_ref, ccat_ref, are_ref, aim_ref, y_ref,
               ug_s, wr_s, wi_s, xr_s, xi_s, y_s, *, nb, ncc, ncl):
    Lc = S5_CHUNK
    lat0 = nb * ncc
    for src, row0, n in ((uc_ref, 0, nb * ncc), (ul_ref, lat0, nb * ncl)):
        folded = _granule_transpose([src[pl.ds(i, n, stride=Lc), :] for i in range(Lc)])
        for g in range(S5_GB):
            ug_s[g, row0:row0 + n, :] = folded[g].astype(BF16)

    npair = S5_GB // 2
    for d in range(2):
        for k in range(npair):
            u0, u1 = ug_s[2 * k], ug_s[2 * k + 1]
            w = _dot(u0, wcat_ref[d, 2 * k]) + _dot(u1, wcat_ref[d, 2 * k + 1])
            wr_s[k] = w[:, 0:LANE]
            wi_s[k] = w[:, LANE:]
        are = [jnp.broadcast_to(are_ref[d, k], (nb, LANE)) for k in range(npair)]
        aim = [jnp.broadcast_to(aim_ref[d, k], (nb, LANE)) for k in range(npair)]

        def make_body(base, n):
            def body(s, carry):
                c = s if d == 0 else n - 1 - s
                rows = pl.ds(base + c, nb, stride=n)
                new = []
                for k in range(npair):
                    xr, xi = carry[2 * k], carry[2 * k + 1]
                    xr_s[k, rows, :] = xr
                    xi_s[k, rows, :] = xi
                    new.append(are[k] * xr - aim[k] * xi + wr_s[k, rows, :])
                    new.append(are[k] * xi + aim[k] * xr + wi_s[k, rows, :])
                return tuple(new)
            return body

        state = tuple(jnp.zeros((nb, LANE), F32) for _ in range(2 * npair))
        state = lax.fori_loop(0, ncc, make_body(0, ncc), state, unroll=8)
        lax.fori_loop(0, ncl, make_body(lat0, ncl), state, unroll=8)
        for g in range(S5_GB):
            y = (_dot(ug_s[g, lat0:, :], kt_ref[d, g])
                 + _dot(jnp.concatenate([xr_s[g // 2, lat0:, :], xi_s[g // 2, lat0:, :]], axis=1).astype(BF16),
                        ccat_ref[d, g]))
            if d == 0:
                y_s[g] = y
            else:
                y_s[g] += y

    unfolded = _granule_transpose([y_s[g] for g in range(S5_GB)])
    for j in range(Lc):
        y_ref[pl.ds(j, nb * ncl, stride=Lc), :] = unfolded[j]


def _s5_call(z, kt, wcat, ccat, are, aim, *, B, T, Tc, nb):
    Lc = S5_CHUNK
    ncl, ncc = T // Lc, Tc // Lc
    nr = nb * (ncl + ncc)
    cbase = B * T // (nb * Tc)
    wspec = pl.BlockSpec((2, S5_GB, LANE, LANE), lambda i, h: (0, i, 0, 0))
    aspec = pl.BlockSpec((2, S5_GB // 2, 1, LANE), lambda i, h: (0, i, 0, 0))
    big = lambda dt: pltpu.VMEM((S5_GB, nr, LANE), dt)
    half = pltpu.VMEM((S5_GB // 2, nr, LANE), F32)
    return pl.pallas_call(
        functools.partial(_s5_kernel, nb=nb, ncc=ncc, ncl=ncl),
        grid=(C_GROUPS // S5_GB, B // nb),
        in_specs=[
            pl.BlockSpec((nb * T, LANE), lambda i, h: (h, i)),
            pl.BlockSpec((nb * Tc, LANE), lambda i, h: (cbase + h, i)),
            wspec,
            pl.BlockSpec((2, S5_GB, LANE, 2 * LANE), lambda i, h: (0, i, 0, 0)),
            pl.BlockSpec((2, S5_GB, 2 * LANE, LANE), lambda i, h: (0, i, 0, 0)),
            aspec, aspec],
        out_specs=pl.BlockSpec((nb * T, LANE), lambda i, h: (h, i)),
        out_shape=jax.ShapeDtypeStruct((B * T, C_WIDTH), F32),
        scratch_shapes=[big(BF16), half, half, half, half,
                        pltpu.VMEM((S5_GB, nb * ncl, LANE), F32)],
        compiler_params=_cparams(("parallel", "parallel")),
        name="s5_scan",
    )(z, z, kt, wcat, ccat, are, aim)


def _s5_finish_kernel(y_ref, u_ref, d_ref, w_ref, b_ref, o_ref):
    y = y_ref[...] + d_ref[...] * u_ref[...]
    g = 0.5 * y * (1.0 + jnp.tanh(math.sqrt(2.0 / math.pi) * (y + 0.044715 * (y * y * y))))
    o_ref[...] = (g * _sigmoid(_dot(g.astype(BF16), w_ref[...]) + b_ref[...])).astype(o_ref.dtype)


def _s5_finish_call(y, z, d_skip, glu_w, glu_b, *, n_rows, tm):
    W = C_WIDTH
    return pl.pallas_call(
        _s5_finish_kernel,
        grid=(n_rows // tm,),
        in_specs=[
            pl.BlockSpec((tm, W), lambda i: (i, 0)),
            pl.BlockSpec((tm, W), lambda i: (i, 0)),
            pl.BlockSpec((1, W), lambda i: (0, 0)),
            pl.BlockSpec((W, W), lambda i: (0, 0)),
            pl.BlockSpec((1, W), lambda i: (0, 0)),
        ],
        out_specs=pl.BlockSpec((tm, W), lambda i: (i, 0)),
        out_shape=jax.ShapeDtypeStruct((n_rows, W), BF16),
        compiler_params=_cparams(("parallel",)),
        name="s5_finish",
    )(y, z, d_skip, glu_w, glu_b)


def _ones_lane(par):
    return D_DH if par == 0 else 0


def _attn_d_kernel(q_ref, k_ref, v_ref, kc_ref, vc_ref, cq_ref, sq_ref, ck_ref, sk_ref, sink_ref,
                   o_ref, kpe, kpo, vpe, vpo, kce, kco, vce, vco, band, *, t, tc, rk):
    W = D_WINDOW
    kv = pl.program_id(1)
    i = pl.program_id(2)
    lane = lax.broadcasted_iota(jnp.int32, (1, LANE), 1)
    lo = lane < D_DH

    def halves(x, ones=False):
        sel = jnp.where(kv == 0, x, pltpu.roll(x, 64, 1))
        e = jnp.where(lo, sel, 0.0)
        o = pltpu.roll(e, 64, 1)
        if ones:
            e = jnp.where(lane == _ones_lane(0), 1.0, e)
            o = jnp.where(lane == _ones_lane(1), 1.0, o)
        return e.astype(BF16), o.astype(BF16)

    @pl.when(i == 0)
    def _():
        zpad = jnp.zeros((W, LANE), BF16)
        for ref in (kpe, kpo, vpe, vpo):
            ref[0:W, :] = zpad
            ref[W + t:W + t + W, :] = zpad
        kce[...], kco[...] = halves(kc_ref[...])
        vce[...], vco[...] = halves(vc_ref[...], ones=True)

        def body(c, carry):
            rows = pl.ds(pl.multiple_of(c * rk, rk), rk)
            dst = pl.ds(pl.multiple_of(W + c * rk, math.gcd(W, rk)), rk)
            kpe[dst, :], kpo[dst, :] = halves(_rope64(k_ref[rows, :], ck_ref[rows, :], sk_ref[rows, :]))
            vpe[dst, :], vpo[dst, :] = halves(v_ref[rows, :], ones=True)
            return carry

        lax.fori_loop(0, t // rk, body, 0)

    tq = q_ref.shape[0]
    nslab = q_ref.shape[1] // LANE
    span = tq + 2 * W

    @pl.when(i == 0)
    def _():
        rr = jnp.bitwise_and(lax.broadcasted_iota(jnp.int32, (nslab * tq, span), 0), tq - 1)
        cc = lax.broadcasted_iota(jnp.int32, (nslab * tq, span), 1)
        band[...] = jnp.where(jnp.abs(rr + W - cc) <= W, 0.0, -jnp.inf)

    cos, sin = cq_ref[...], sq_ref[...]
    qs = jnp.concatenate(
        [_rope64(q_ref[:, j * LANE:(j + 1) * LANE], cos, sin) for j in range(nslab)], axis=0)
    qs = (qs * (D_DH ** -0.5 * LOG2E)).astype(BF16)
    rows_k = pl.ds(pl.multiple_of(i * tq, tq), span)
    kpos = i * tq - W + lax.broadcasted_iota(jnp.int32, (1, span), 1)
    bias = band[...] + jnp.where((kpos >= 0) & (kpos < t), 0.0, -jnp.inf)
    out = None
    lane_o = lax.broadcasted_iota(jnp.int32, (1, LANE), 1)
    for par, (kp, vp, kcx, vcx) in enumerate(((kpe, vpe, kce, vce), (kpo, vpo, kco, vco))):
        s = jnp.concatenate([_dot_nt(qs, kcx[...]), _dot_nt(qs, kp[rows_k, :]) + bias], axis=1)
        snk = LOG2E * jnp.concatenate(
            [jnp.broadcast_to(sink_ref[0, 0:1, 2 * j + par:2 * j + par + 1], (tq, 1)) for j in range(nslab)],
            axis=0)
        m = jnp.maximum(jnp.max(s, axis=-1, keepdims=True), snk)
        p = jnp.exp2(s - m).astype(BF16)
        o = _dot(p[:, 0:tc], vcx[...]) + _dot(p[:, tc:], vp[rows_k, :])
        one = _ones_lane(par)
        l = o[:, one:one + 1] + jnp.exp2(snk - m)
        own = (lane_o < D_DH) if par == 0 else (lane_o >= D_DH)
        o = jnp.where(own, o, 0.0) / l
        out = o if out is None else out + o
    for j in range(nslab):
        o_ref[:, j * LANE:(j + 1) * LANE] = out[j * tq:(j + 1) * tq, :].astype(o_ref.dtype)


def _attn_d_call(z, cos, sin, sink, *, B, T, Tc):
    tq = min(2 * Q_BLOCK, T)
    nq = T // tq
    grp_w = (D_HEADS // D_KV) * D_DH
    qcol = C_WIDTH // grp_w
    kcol = (C_WIDTH + D_HEADS * D_DH) // LANE
    cbase = B * T // Tc
    rk = min(512, T)
    pad = lambda n: pltpu.VMEM((n, LANE), BF16)
    return pl.pallas_call(
        functools.partial(_attn_d_kernel, t=T, tc=Tc, rk=rk),
        grid=(B, D_KV, nq),
        in_specs=[
            pl.BlockSpec((tq, grp_w), lambda b, kv, i: (b * nq + i, qcol + kv)),
            pl.BlockSpec((T, LANE), lambda b, kv, i: (b, kcol)),
            pl.BlockSpec((T, LANE), lambda b, kv, i: (b, kcol + 1)),
            pl.BlockSpec((Tc, LANE), lambda b, kv, i: (cbase + b, kcol)),
            pl.BlockSpec((Tc, LANE), lambda b, kv, i: (cbase + b, kcol + 1)),
            pl.BlockSpec((tq, LANE), lambda b, kv, i: (i, 0)),
            pl.BlockSpec((tq, LANE), lambda b, kv, i: (i, 0)),
            pl.BlockSpec((T, LANE), lambda b, kv, i: (0, 0)),
            pl.BlockSpec((T, LANE), lambda b, kv, i: (0, 0)),
            pl.BlockSpec((1, 1, LANE), lambda b, kv, i: (kv, 0, 0)),
        ],
        out_specs=pl.BlockSpec((tq, grp_w), lambda b, kv, i: (b * nq + i, kv)),
        out_shape=jax.ShapeDtypeStruct((B * T, D_HEADS * D_DH), BF16),
        scratch_shapes=[pad(T + 2 * D_WINDOW)] * 4 + [pad(Tc)] * 4
        + [pltpu.VMEM((grp_w // LANE * tq, tq + 2 * D_WINDOW), F32)],
        compiler_params=_cparams(("parallel", "parallel", "arbitrary")),
        name="attn_d",
    )(z, z, z, z, z, cos, sin, cos, sin, sink)


def _mod_rows(ml, idx):
    sel = ml[:, jnp.array(idx), :]
    return jnp.pad(sel, ((0, 0), (0, 8 - len(idx)), (0, 0)))


def kernel(x, c, ctx, c_ctx, mod_w, mod_b, norm_g, ffn_w1, ffn_w3, ffn_w2, ab_w_in, ab_b_gate, ab_w_out, a_gq, a_gk, b_head_g, cd_w_in, cd_w_out, s5_a_re, s5_a_im, s5_log_dt, s5_b_re, s5_b_im, s5_c_re, s5_c_im, s5_d, s5_glu_w, s5_glu_b, d_sink, final_g):
    B, T, D = x.shape
    Tc = ctx.shape[1]
    depth = mod_w.shape[0]
    n_lat, n_ctx = B * T, B * Tc
    n_all = n_lat + n_ctx
    tm = math.gcd(math.gcd(T, n_ctx), 512)
    tm_mm = math.gcd(math.gcd(T, n_ctx), 1024)
    tf = math.gcd(ffn_w1.shape[-1], 512)
    seg = dict(seg_rows=T, n_seg=B)

    h, hc0 = x, ctx
    cvec = jnp.zeros((8, D), F32).at[:B].set(c).at[B].set(c_ctx)
    mods = _modvec_call(cvec, mod_w, mod_b).reshape(depth, 8, N_MOD, D)[:, :B + 1]
    cos_a, sin_a = _rope_tables(T, A_DH)
    cos_d, sin_d = _rope_tables(T, D_DH)
    w1, w3, w2 = ffn_w1.astype(BF16), ffn_w3.astype(BF16), ffn_w2.astype(BF16)
    ones = jnp.ones((1, D), F32)

    for l in range(depth):
        last = l == depth - 1
        want_ctx = not last
        ml = mods[l]
        i = l // 2
        h = _ffn_call(h, hc0 if l == 0 else None, _mod_rows(ml, (0, 1, 2)), norm_g[l, 0][None],
                      w1, w3, w2, ones, wsel=(l, 0),
                      n_rows=n_all, tm=tm, tf=tf, final_norm=False, **seg)
        mod_in = _mod_rows(ml, (3, 4))
        mod_out = _mod_rows(ml, (5,))
        if l % 2 == 0:
            w_in = jnp.pad(ab_w_in[i], ((0, 0), (0, AB_IN_PAD - ab_w_in.shape[-1]))).astype(BF16)
            z = _inproj_call(h, mod_in, norm_g[l, 1][None], w_in, tm=tm_mm, tn=1024, **seg)
            ya = _attn_a_call(z, cos_a, sin_a, a_gq[i][None], a_gk[i][None], B=B, T=T, Tc=Tc)
            yac = _attn_a_ctx_call(z, a_gq[i][None], a_gk[i][None], B=B, T=T, Tc=Tc)
            ng = 4 * B_HEADS
            gates = z[:, AB_GATE_COL:AB_GATE_COL + ng]
            grow = gates.reshape(n_all // B_CHUNK, B_CHUNK, ng).transpose(0, 2, 1)
            bcol = jnp.pad(ab_b_gate[i], (0, LANE - ng))[None]
            brow = ab_b_gate[i][:, None]
            nloc, stat, scal, uaug = _mlstm_local_call(z, grow, bcol, brow, cb=math.gcd(n_all // B_CHUNK, 8))
            hf, hb = _mlstm_scan_call(z, nloc, stat, scal, uaug, B=B, T=T, Tc=Tc)
            yb = _mlstm_finish_call(hf, hb, z, b_head_g[i][None], tm=tm)
            xa = jnp.concatenate([ya, yac], axis=0)
            w_out = ab_w_out[i].astype(BF16)
            n_rows = n_all
        else:
            w_in = cd_w_in[i].astype(BF16)
            z = _inproj_call(h, mod_in, norm_g[l, 1][None], w_in, tm=tm_mm, tn=w_in.shape[1] // 2, **seg)
            prm = [_s5_params(s5_a_re[i, d], s5_a_im[i, d], s5_log_dt[i, d], s5_b_re[i, d], s5_b_im[i, d],
                              s5_c_re[i, d], s5_c_im[i, d], reverse=(d == 1)) for d in range(2)]
            s5w = [jnp.stack([prm[0][k], prm[1][k]]) for k in range(len(prm[0]))]
            ys5 = _s5_call(z, *s5w, B=B, T=T, Tc=Tc, nb=math.gcd(B, 2))
            xa = _s5_finish_call(ys5, z, s5_d[i][None], s5_glu_w[i].astype(BF16), s5_glu_b[i][None],
                                 n_rows=n_lat, tm=tm)
            sink = jnp.pad(d_sink[i].reshape(D_KV, 1, D_HEADS // D_KV), ((0, 0), (0, 0), (0, LANE - D_HEADS // D_KV)))
            yb = _attn_d_call(z, cos_d, sin_d, sink, B=B, T=T, Tc=Tc)
            w_out = cd_w_out[i].astype(BF16)
            n_rows = n_all if want_ctx else n_lat
        if not want_ctx:
            n_rows = n_lat
        h = _outproj_call(xa, yb, w_out, h, mod_out, n_rows=n_rows, tm=tm, **seg)
        h = _ffn_call(h, None, _mod_rows(ml, (6, 7, 8)), norm_g[l, 2][None], w1, w3, w2,
                      final_g[None] if last else ones, wsel=(l, 1), n_rows=n_rows, tm=tm, tf=tf, final_norm=last,
                      out_3d=last, **seg)
    return h
```

```python
import functools
import math

import jax
import jax.numpy as jnp
from jax import lax
from jax.experimental import pallas as pl
from jax.experimental.pallas import tpu as pltpu

F32 = jnp.float32
BF16 = jnp.bfloat16

EPS = 1e-6
ROPE_THETA = 10000.0
GRID_W = 64
N_MOD = 9
Q_BLOCK = 128

A_HEADS, A_KV, A_DH = 8, 2, 128
B_HEADS, B_DQK, B_DV, B_CHUNK = 4, 128, 256, 64
C_WIDTH, C_GROUP, C_STATE = 1024, 16, 64
C_GROUPS = C_WIDTH // C_GROUP
D_HEADS, D_KV, D_DH, D_WINDOW = 16, 2, 64, 128

AB_GATE_COL = 4608
AB_IN_PAD = 5120
S5_CHUNK = 8
S5_GB = 8

LANE = 128
LOG2E = math.log2(math.e)
VMEM_LIMIT = 56 * 1024 * 1024


def _cparams(sem, vmem_limit=VMEM_LIMIT):
    return pltpu.CompilerParams(dimension_semantics=sem, vmem_limit_bytes=vmem_limit)


def _rms(x):
    return x * lax.rsqrt(jnp.mean(x * x, axis=-1, keepdims=True) + EPS)


def _sigmoid(x):
    return 1.0 / (1.0 + jnp.exp(-x))


def _log_sigmoid(x):
    return jnp.minimum(x, 0.0) - jnp.log(1.0 + jnp.exp(-jnp.abs(x)))


def _dot(a, b):
    return jnp.dot(a, b, preferred_element_type=F32)


def _dot_nt(a, b):
    return lax.dot_general(a, b, (((1,), (1,)), ((), ())), preferred_element_type=F32)


def _modvec_kernel(c_ref, w_ref, b_ref, o_ref):
    c = c_ref[...]
    s = (c * _sigmoid(c)).astype(BF16)
    o_ref[0] = _dot(s, w_ref[0].astype(BF16)) + b_ref[0]


def _modvec_call(cvec, mod_w, mod_b):
    L, D, N = mod_w.shape
    tn = math.gcd(D, 1024)
    return pl.pallas_call(
        _modvec_kernel,
        grid=(L, N // tn),
        in_specs=[
            pl.BlockSpec((8, D), lambda l, j: (0, 0)),
            pl.BlockSpec((1, D, tn), lambda l, j: (l, 0, j)),
            pl.BlockSpec((1, 1, tn), lambda l, j: (l, 0, j)),
        ],
        out_specs=pl.BlockSpec((1, 8, tn), lambda l, j: (l, 0, j)),
        out_shape=jax.ShapeDtypeStruct((L, 8, N), F32),
        compiler_params=_cparams(("parallel", "parallel")),
        name="modvec",
    )(cvec, mod_w, mod_b.reshape(L, 1, N))


def _modulate(h, g, mod):
    return (_rms(h) * g * (1.0 + mod[1:2, :]) + mod[0:1, :]).astype(BF16)


def _ffn_kernel(h_ref, hc_ref, mod_ref, g_ref, w1_ref, w3_ref, w2_ref, fg_ref, o_ref, hm_ref, acc_ref, *,
                final_norm, n_lat_tiles, nj):
    j = pl.program_id(1)
    tm, d = o_ref.shape

    def load_h():
        if n_lat_tiles is None:
            return h_ref[...]
        return jnp.where(pl.program_id(0) < n_lat_tiles, h_ref[...], hc_ref[...].reshape(tm, d))

    def column_tile(first, last):
        if first:
            hm_ref[...] = _modulate(load_h(), g_ref[...], mod_ref[0])
        hm = hm_ref[...]
        a = _dot(hm, w1_ref[...])
        b = _dot(hm, w3_ref[...])
        gg = (a * _sigmoid(a)) * b
        down = _dot(gg.astype(BF16), w2_ref[...])
        total = down if first else acc_ref[...] + down
        if last:
            out = load_h() + (0.5 * mod_ref[0, 2:3, :]) * total
            if final_norm:
                out = _rms(out) * fg_ref[...]
            o_ref[...] = out
        else:
            acc_ref[...] = total

    if nj == 1:
        column_tile(True, True)
    else:
        pl.when(j == 0)(lambda: column_tile(True, False))
        pl.when((j > 0) & (j < nj - 1))(lambda: column_tile(False, False))
        pl.when(j == nj - 1)(lambda: column_tile(False, True))


def _ffn_call(h, hc, mod, g, w1, w3, w2, fg, *, wsel, n_rows, seg_rows, n_seg, tm, tf, final_norm, out_3d=False):
    D = h.shape[-1]
    FF = w1.shape[-1]
    wl, wk = wsel
    tpb = seg_rows // tm
    seg = lambda i, j: (jnp.minimum(i * tm // seg_rows, n_seg), 0, 0)
    if hc is None:
        n_lat_tiles = None
        h_specs = [pl.BlockSpec((tm, D), lambda i, j: (i, 0)), pl.BlockSpec((8, D), lambda i, j: (0, 0))]
        hc = h
    else:
        tc = hc.shape[1]
        assert tm % tc == 0 and h.shape[1] == seg_rows
        n_lat_tiles = h.shape[0] * tpb
        lat = lambda i: jnp.minimum(i, n_lat_tiles - 1)
        h_specs = [pl.BlockSpec((None, tm, D), lambda i, j: (lat(i) // tpb, lat(i) % tpb, 0)),
                   pl.BlockSpec((tm // tc, tc, D), lambda i, j: (jnp.maximum(i - n_lat_tiles, 0), 0, 0))]
    if out_3d:
        assert n_rows % seg_rows == 0
        out_spec = pl.BlockSpec((None, tm, D), lambda i, j: (i // tpb, i % tpb, 0))
        out_shape = jax.ShapeDtypeStruct((n_rows // seg_rows, seg_rows, D), F32)
    else:
        out_spec = pl.BlockSpec((tm, D), lambda i, j: (i, 0))
        out_shape = jax.ShapeDtypeStruct((n_rows, D), F32)
    return pl.pallas_call(
        functools.partial(_ffn_kernel, final_norm=final_norm, n_lat_tiles=n_lat_tiles, nj=FF // tf),
        grid=(n_rows // tm, FF // tf),
        in_specs=h_specs + [
            pl.BlockSpec((1, 8, D), seg),
            pl.BlockSpec((1, D), lambda i, j: (0, 0)),
            pl.BlockSpec((None, None, D, tf), lambda i, j: (wl, wk, 0, j)),
            pl.BlockSpec((None, None, D, tf), lambda i, j: (wl, wk, 0, j)),
            pl.BlockSpec((None, None, tf, D), lambda i, j: (wl, wk, j, 0)),
            pl.BlockSpec((1, D), lambda i, j: (0, 0)),
        ],
        out_specs=out_spec,
        out_shape=out_shape,
        scratch_shapes=[pltpu.VMEM((tm, D), BF16), pltpu.VMEM((tm, D), F32)],
        compiler_params=_cparams(("parallel", "arbitrary")),
        name="ffn",
    )(h, hc, mod, g, w1, w3, w2, fg)


def _inproj_kernel(h_ref, mod_ref, g_ref, w_ref, o_ref, hm_ref):
    def column_tile(first):
        if first:
            hm_ref[...] = _modulate(h_ref[...], g_ref[...], mod_ref[0])
        o_ref[...] = _dot(hm_ref[...], w_ref[...])

    pl.when(pl.program_id(1) == 0)(lambda: column_tile(True))
    pl.when(pl.program_id(1) > 0)(lambda: column_tile(False))


def _inproj_call(h, mod, g, w, *, seg_rows, n_seg, tm, tn):
    R, D = h.shape
    N = w.shape[1]
    seg = lambda i, j: (jnp.minimum(i * tm // seg_rows, n_seg), 0, 0)
    return pl.pallas_call(
        _inproj_kernel,
        grid=(R // tm, N // tn),
        in_specs=[
            pl.BlockSpec((tm, D), lambda i, j: (i, 0)),
            pl.BlockSpec((1, 8, D), seg),
            pl.BlockSpec((1, D), lambda i, j: (0, 0)),
            pl.BlockSpec((D, tn), lambda i, j: (0, j)),
        ],
        out_specs=pl.BlockSpec((tm, tn), lambda i, j: (i, j)),
        out_shape=jax.ShapeDtypeStruct((R, N), F32),
        scratch_shapes=[pltpu.VMEM((tm, D), BF16)],
        compiler_params=_cparams(("parallel", "arbitrary")),
        name="inproj",
    )(h, mod, g, w)


def _outproj_kernel(xa_ref, xb_ref, wa_ref, wb_ref, h_ref, mod_ref, o_ref):
    y = _dot(xa_ref[...], wa_ref[...]) + _dot(xb_ref[...], wb_ref[...])
    o_ref[...] = h_ref[...] + mod_ref[0, 0:1, :] * y


def _outproj_call(xa, xb, w, h, mod, *, n_rows, seg_rows, n_seg, tm):
    D = h.shape[1]
    Ka, Kb = xa.shape[1], xb.shape[1]
    assert Ka == Kb and w.shape[0] == Ka + Kb
    seg = lambda i: (jnp.minimum(i * tm // seg_rows, n_seg), 0, 0)
    return pl.pallas_call(
        _outproj_kernel,
        grid=(n_rows // tm,),
        in_specs=[
            pl.BlockSpec((tm, Ka), lambda i: (i, 0)),
            pl.BlockSpec((tm, Kb), lambda i: (i, 0)),
            pl.BlockSpec((Ka, D), lambda i: (0, 0)),
            pl.BlockSpec((Kb, D), lambda i: (1, 0)),
            pl.BlockSpec((tm, D), lambda i: (i, 0)),
            pl.BlockSpec((1, 8, D), seg),
        ],
        out_specs=pl.BlockSpec((tm, D), lambda i: (i, 0)),
        out_shape=jax.ShapeDtypeStruct((n_rows, D), F32),
        compiler_params=_cparams(("parallel",)),
        name="outproj",
    )(xa, xb, w, w, h, mod)


def _rope_tables(T, head_dim):
    half = head_dim // 2
    n_freq = head_dim // 4
    t = jnp.arange(T)
    r = (t // GRID_W).astype(F32)
    cidx = (t % GRID_W).astype(F32)
    inv = ROPE_THETA ** (-jnp.arange(n_freq, dtype=F32) / n_freq)
    ang = jnp.concatenate([r[:, None] * inv, cidx[:, None] * inv], axis=-1)
    cos, sin = jnp.cos(ang), jnp.sin(ang)
    cos_h = jnp.concatenate([cos, cos], axis=-1)
    sin_h = jnp.concatenate([-sin, sin], axis=-1)
    reps = LANE // head_dim
    return jnp.tile(cos_h, (1, reps)), jnp.tile(sin_h, (1, reps))


def _rope128(x, cos, sin):
    return x * cos + pltpu.roll(x, 64, 1) * sin


def _rope64(x, cos, sin):
    lane = lax.broadcasted_iota(jnp.int32, x.shape, 1)
    first = jnp.bitwise_and(lane, 63) < 32
    partner = jnp.where(first, pltpu.roll(x, 96, 1), pltpu.roll(x, 32, 1))
    return x * cos + partner * sin


def _attn_a_kernel(q_ref, k_ref, v_ref, kc_ref, vc_ref, cq_ref, sq_ref, ck_ref, sk_ref, gq_ref, gk_ref,
                   o_ref, kall, vall, *, tc, t, rk, kc):
    grp = A_HEADS // A_KV

    @pl.when(pl.program_id(2) == 0)
    def _():
        gk = gk_ref[...]
        kall[0:tc, :] = (_rms(kc_ref[...]) * gk).astype(BF16)
        vall[0:tc, 0:A_DH] = vc_ref[...].astype(BF16)
        vall[:, A_DH:] = jnp.ones((tc + t, A_DH), BF16)

        def body(c, carry):
            rows = pl.ds(pl.multiple_of(c * rk, rk), rk)
            kn = _rms(k_ref[rows, :]) * gk
            dst = pl.ds(pl.multiple_of(tc + c * rk, math.gcd(tc, rk)), rk)
            kall[dst, :] = _rope128(kn, ck_ref[rows, :], sk_ref[rows, :]).astype(BF16)
            vall[dst, 0:A_DH] = v_ref[rows, :].astype(BF16)
            return carry

        lax.fori_loop(0, t // rk, body, 0)

    q4 = q_ref[...]
    qs = jnp.concatenate([q4[:, h * A_DH:(h + 1) * A_DH] for h in range(grp)], axis=0)
    qn = _rms(qs) * gq_ref[...]
    cos = jnp.concatenate([cq_ref[...]] * grp, axis=0)
    sin = jnp.concatenate([sq_ref[...]] * grp, axis=0)
    qr = (_rope128(qn, cos, sin) * (A_DH ** -0.5 * LOG2E)).astype(BF16)
    m = l = acc = None
    for k0, kn in [(0, tc)] + [(tc + c * kc, kc) for c in range(t // kc)]:
        s = _dot_nt(qr, kall[k0:k0 + kn, :])
        smax = jnp.max(s, axis=-1, keepdims=True)
        if m is None:
            m = smax
            acc = _dot(jnp.exp2(s - m).astype(BF16), vall[k0:k0 + kn, :])
        else:
            m_new = jnp.maximum(m, smax)
            acc = jnp.exp2(m - m_new) * acc + _dot(jnp.exp2(s - m_new).astype(BF16), vall[k0:k0 + kn, :])
            m = m_new
    o = acc[:, 0:A_DH] / acc[:, A_DH:]
    tq = q4.shape[0]
    for h in range(grp):
        o_ref[:, h * A_DH:(h + 1) * A_DH] = o[h * tq:(h + 1) * tq, :].astype(o_ref.dtype)


def _attn_a_call(z, cos, sin, gq, gk, *, B, T, Tc):
    tq = min(2 * Q_BLOCK, T)
    nq = T // tq
    grp_w = (A_HEADS // A_KV) * A_DH
    kcol = A_HEADS
    cbase = B * T // Tc
    rk = min(512, T)
    return pl.pallas_call(
        functools.partial(_attn_a_kernel, tc=Tc, t=T, rk=rk, kc=min(256, T)),
        grid=(B, A_KV, nq),
        in_specs=[
            pl.BlockSpec((tq, grp_w), lambda b, kv, i: (b * nq + i, kv)),
            pl.BlockSpec((T, A_DH), lambda b, kv, i: (b, kcol + kv)),
            pl.BlockSpec((T, A_DH), lambda b, kv, i: (b, kcol + A_KV + kv)),
            pl.BlockSpec((Tc, A_DH), lambda b, kv, i: (cbase + b, kcol + kv)),
            pl.BlockSpec((Tc, A_DH), lambda b, kv, i: (cbase + b, kcol + A_KV + kv)),
            pl.BlockSpec((tq, LANE), lambda b, kv, i: (i, 0)),
            pl.BlockSpec((tq, LANE), lambda b, kv, i: (i, 0)),
            pl.BlockSpec((T, LANE), lambda b, kv, i: (0, 0)),
            pl.BlockSpec((T, LANE), lambda b, kv, i: (0, 0)),
            pl.BlockSpec((1, A_DH), lambda b, kv, i: (0, 0)),
            pl.BlockSpec((1, A_DH), lambda b, kv, i: (0, 0)),
        ],
        out_specs=pl.BlockSpec((tq, grp_w), lambda b, kv, i: (b * nq + i, kv)),
        out_shape=jax.ShapeDtypeStruct((B * T, A_HEADS * A_DH), BF16),
        scratch_shapes=[pltpu.VMEM((Tc + T, A_DH), BF16), pltpu.VMEM((Tc + T, 2 * A_DH), BF16)],
        compiler_params=_cparams(("parallel", "parallel", "arbitrary")),
        name="attn_a",
    )(z, z, z, z, z, cos, sin, cos, sin, gq, gk)


def _attn_a_ctx_kernel(q_ref, kc_ref, vc_ref, gq_ref, gk_ref, o_ref):
    grp = A_HEADS // A_KV
    q4 = q_ref[...]
    tq = q4.shape[0]
    qs = jnp.concatenate([q4[:, h * A_DH:(h + 1) * A_DH] for h in range(grp)], axis=0)
    qn = (_rms(qs) * gq_ref[...] * (A_DH ** -0.5)).astype(BF16)
    kc = (_rms(kc_ref[...]) * gk_ref[...]).astype(BF16)
    s = _dot_nt(qn, kc)
    p = jnp.exp(s - jnp.max(s, axis=-1, keepdims=True))
    l = jnp.sum(p, axis=-1, keepdims=True)
    o = _dot(p.astype(BF16), vc_ref[...].astype(BF16)) / l
    for h in range(grp):
        o_ref[:, h * A_DH:(h + 1) * A_DH] = o[h * tq:(h + 1) * tq, :].astype(o_ref.dtype)


def _attn_a_ctx_call(z, gq, gk, *, B, T, Tc):
    grp_w = (A_HEADS // A_KV) * A_DH
    kcol = A_HEADS
    cbase = B * T // Tc
    return pl.pallas_call(
        _attn_a_ctx_kernel,
        grid=(B, A_KV),
        in_specs=[
            pl.BlockSpec((Tc, grp_w), lambda b, kv: (cbase + b, kv)),
            pl.BlockSpec((Tc, A_DH), lambda b, kv: (cbase + b, kcol + kv)),
            pl.BlockSpec((Tc, A_DH), lambda b, kv: (cbase + b, kcol + A_KV + kv)),
            pl.BlockSpec((1, A_DH), lambda b, kv: (0, 0)),
            pl.BlockSpec((1, A_DH), lambda b, kv: (0, 0)),
        ],
        out_specs=pl.BlockSpec((Tc, grp_w), lambda b, kv: (b, kv)),
        out_shape=jax.ShapeDtypeStruct((B * Tc, A_HEADS * A_DH), BF16),
        compiler_params=_cparams(("parallel", "parallel")),
        name="attn_a_ctx",
    )(z, z, z, gq, gk)


MLSTM_CAUG = B_DV + 8


def _mlstm_local_kernel(q_ref, k_ref, v0_ref, v1_ref, gc_ref, gr_ref, bcol_ref, brow_ref,
                        nloc_ref, stat_ref, scal_ref, uaug_ref, *, cb):
    L = B_CHUNK
    row = lax.broadcasted_iota(jnp.int32, (L, L), 0)
    col = lax.broadcasted_iota(jnp.int32, (L, L), 1)
    gcol = (gc_ref[...] + bcol_ref[...]).reshape(cb, L, LANE)
    grow = gr_ref[...] + brow_ref[...][None]
    sub0 = lax.broadcasted_iota(jnp.int32, (cb, 8, B_DQK), 1) == 0
    n_stat = 3 * B_HEADS
    stat_ref[:, :, n_stat:] = jnp.zeros((2, cb * L, stat_ref.shape[2] - n_stat), F32)
    for hd in range(B_HEADS):
        q = q_ref[:, hd * B_DQK:(hd + 1) * B_DQK].reshape(cb, L, B_DQK)
        k = k_ref[:, hd * B_DQK:(hd + 1) * B_DQK].reshape(cb, L, B_DQK)
        v_ref = v0_ref if hd < 2 else v1_ref
        v = v_ref[:, (hd % 2) * B_DV:(hd % 2 + 1) * B_DV].reshape(cb, L, B_DV)
        qb = (q * (B_DQK ** -0.5)).astype(BF16)
        kb = k.astype(BF16)
        vb = v.astype(BF16)
        qk = jnp.einsum('cld,csd->cls', qb, kb, preferred_element_type=F32)
        for d in range(2):
            gi = 2 * d * B_HEADS + hd
            gf = gi + B_HEADS
            li_col = gcol[:, :, gi:gi + 1]
            lf_col = _log_sigmoid(gcol[:, :, gf:gf + 1])
            li_row = grow[:, gi:gi + 1, :]
            lf_row = _log_sigmoid(grow[:, gf:gf + 1, :])
            mask = ((col <= row) if d == 0 else (col >= row))[None]
            mask_t = ((row <= col) if d == 0 else (row >= col))[None]
            bcum_col = jnp.sum(jnp.where(mask, lf_row, 0.0), axis=2, keepdims=True)
            bcum_row = jnp.sum(jnp.where(mask_t, lf_col, 0.0), axis=1, keepdims=True)
            log_d = jnp.where(mask, bcum_col - bcum_row + li_row, -jnp.inf)
            a = jnp.max(log_d, axis=2, keepdims=True)
            s = qk * jnp.exp(log_d - a)
            nloc = jnp.einsum('cls,csv->clv', s.astype(BF16), vb, preferred_element_type=F32)
            dloc = jnp.sum(s, axis=2, keepdims=True)
            b_last = jnp.sum(lf_row, axis=2, keepdims=True)
            log_w = b_last - bcum_col + li_col
            b_loc = jnp.max(log_w, axis=1, keepdims=True)
            w = jnp.exp(log_w - b_loc)
            u = jnp.einsum('csv,csd->cvd', (w * v).astype(BF16), kb, preferred_element_type=F32)
            nu = jnp.sum(w * kb.astype(F32), axis=1, keepdims=True)
            nloc_ref[d, :, hd * B_DV:(hd + 1) * B_DV] = nloc.reshape(cb * L, B_DV)
            stat_ref[d, :, hd:hd + 1] = (a - bcum_col).reshape(cb * L, 1)
            stat_ref[d, :, B_HEADS + hd:B_HEADS + hd + 1] = dloc.reshape(cb * L, 1)
            stat_ref[d, :, 2 * B_HEADS + hd:2 * B_HEADS + hd + 1] = bcum_col.reshape(cb * L, 1)
            scal_ref[d, :, hd:hd + 1, :] = jnp.broadcast_to(b_last, (cb, 1, LANE))
            scal_ref[d, :, B_HEADS + hd:B_HEADS + hd + 1, :] = jnp.broadcast_to(b_loc, (cb, 1, LANE))
            uaug_ref[d, :, hd, 0:B_DV, :] = u
            uaug_ref[d, :, hd, B_DV:MLSTM_CAUG, :] = jnp.where(sub0, nu, 0.0)


def _mlstm_local_call(z, grow, bcol, brow, *, cb):
    L = B_CHUNK
    R = z.shape[0]
    nchunk = R // L
    W = B_HEADS * B_DV
    return pl.pallas_call(
        functools.partial(_mlstm_local_kernel, cb=cb),
        grid=(nchunk // cb,),
        in_specs=[
            pl.BlockSpec((cb * L, 512), lambda i: (i, 3)),
            pl.BlockSpec((cb * L, 512), lambda i: (i, 4)),
            pl.BlockSpec((cb * L, 512), lambda i: (i, 5)),
            pl.BlockSpec((cb * L, 512), lambda i: (i, 6)),
            pl.BlockSpec((cb * L, LANE), lambda i: (i, AB_GATE_COL // LANE)),
            pl.BlockSpec((cb, 16, L), lambda i: (i, 0, 0)),
            pl.BlockSpec((1, LANE), lambda i: (0, 0)),
            pl.BlockSpec((16, 1), lambda i: (0, 0)),
        ],
        out_specs=[
            pl.BlockSpec((2, cb * L, W), lambda i: (0, i, 0)),
            pl.BlockSpec((2, cb * L, 16), lambda i: (0, i, 0)),
            pl.BlockSpec((2, cb, 8, LANE), lambda i: (0, i, 0, 0)),
            pl.BlockSpec((2, cb, B_HEADS, MLSTM_CAUG, B_DQK), lambda i: (0, i, 0, 0, 0)),
        ],
        out_shape=[
            jax.ShapeDtypeStruct((2, R, W), F32),
            jax.ShapeDtypeStruct((2, R, 16), F32),
            jax.ShapeDtypeStruct((2, nchunk, 8, LANE), F32),
            jax.ShapeDtypeStruct((2, nchunk, B_HEADS, MLSTM_CAUG, B_DQK), F32),
        ],
        compiler_params=_cparams(("parallel",)),
        name="mlstm_local",
    )(z, z, z, z, z, grow, bcol, brow)


def _mlstm_scan_kernel(qf_ref, nf_ref, stf_ref, scf_ref, uf_ref, qb_ref, nb_ref, stb_ref, scb_ref, ub_ref,
                       of_ref, ob_ref, c_s, m_s, *, cps):
    @pl.when(pl.program_id(1) == 0)
    def _():
        c_s[...] = jnp.zeros_like(c_s)
        m_s[...] = jnp.zeros_like(m_s)

    L = B_CHUNK
    units = []
    for d, (q_ref, n_ref, st_ref, sc_ref, u_ref, o_ref) in enumerate((
            (qf_ref, nf_ref, stf_ref, scf_ref, uf_ref, of_ref),
            (qb_ref, nb_ref, stb_ref, scb_ref, ub_ref, ob_ref))):
        for hd in range(B_HEADS):
            u = d * B_HEADS + hd
            units.append((u, d, hd, q_ref, n_ref, st_ref, sc_ref, u_ref, o_ref,
                          c_s[u, 0:B_DV, :], c_s[u, B_DV:, :], m_s[u, 0:1, :]))
    final_state = []
    for u, d, hd, q_ref, n_ref, st_ref, sc_ref, u_ref, o_ref, c_mat, n_rep, m_row in units:
        for step in range(cps):
            ck = step if d == 0 else cps - 1 - step
            rows = slice(ck * L, (ck + 1) * L)
            st = st_ref[0, rows, :]
            rep = lambda j: jnp.broadcast_to(st[:, j:j + 1], (L, LANE))
            g, dloc, bcum = rep(hd), rep(B_HEADS + hd), rep(2 * B_HEADS + hd)
            m = jnp.broadcast_to(m_row, (L, LANE))
            e_intra = jnp.exp(jnp.minimum(g - m, 0.0))
            w_inter = jnp.exp(jnp.minimum(m - g, 0.0))
            floor = jnp.exp(-(bcum + jnp.maximum(m, g)))
            qb = (q_ref[rows, hd * B_DQK:(hd + 1) * B_DQK] * (B_DQK ** -0.5)).astype(BF16)
            c_prev = jnp.concatenate([c_mat, n_rep], axis=0).astype(BF16)
            inter = _dot_nt(qb, c_prev)
            den = e_intra * dloc + w_inter * inter[:, B_DV:B_DV + LANE]
            r = 1.0 / jnp.maximum(jnp.abs(den), floor)
            for j in range(B_DV // LANE):
                c0 = hd * B_DV + j * LANE
                num = e_intra * n_ref[0, rows, c0:c0 + LANE] + w_inter * inter[:, j * LANE:(j + 1) * LANE]
                o_ref[0, rows, c0:c0 + LANE] = num * r
            b_last = sc_ref[0, ck, hd:hd + 1, :]
            b_loc = sc_ref[0, ck, B_HEADS + hd:B_HEADS + hd + 1, :]
            m_new = jnp.maximum(b_last + m_row, b_loc)
            decay = jnp.exp(b_last + m_row - m_new)
            scl = jnp.exp(b_loc - m_new)
            c_mat = decay * c_mat + scl * u_ref[0, ck, hd, 0:B_DV, :]
            n_rep = decay * n_rep + scl * jnp.broadcast_to(u_ref[0, ck, hd, B_DV:B_DV + 1, :], (LANE, B_DQK))
            m_row = m_new
        final_state.append((u, c_mat, n_rep, m_row))
    for u, c_mat, n_rep, m_row in final_state:
        c_s[u, 0:B_DV, :] = c_mat
        c_s[u, B_DV:, :] = n_rep
        m_s[u] = jnp.broadcast_to(m_row, m_s.shape[1:])


def _mlstm_scan_call(z, nloc, stat, scal, uaug, *, B, T, Tc):
    L = B_CHUNK
    ncl, ncc = T // L, Tc // L
    cps = math.gcd(math.gcd(ncl, ncc), 4)
    npl, npc = ncl // cps, ncc // cps
    cbase = B * T // (L * cps)
    R = z.shape[0]
    W = B_HEADS * B_DV

    def blk(b, s, d):
        is_ctx = s < npc
        jc = s if d == 0 else npc - 1 - s
        jl = (s - npc) if d == 0 else (npl - 1 - (s - npc))
        return jnp.where(is_ctx, cbase + b * npc + jc, b * npl + jl)

    def specs(d):
        return [
            pl.BlockSpec((cps * L, 512), lambda b, s: (blk(b, s, d), 3)),
            pl.BlockSpec((1, cps * L, W), lambda b, s: (d, blk(b, s, d), 0)),
            pl.BlockSpec((1, cps * L, 16), lambda b, s: (d, blk(b, s, d), 0)),
            pl.BlockSpec((1, cps, 8, LANE), lambda b, s: (d, blk(b, s, d), 0, 0)),
            pl.BlockSpec((1, cps, B_HEADS, MLSTM_CAUG, B_DQK), lambda b, s: (d, blk(b, s, d), 0, 0, 0)),
        ]

    out_sds = jax.ShapeDtypeStruct((1, R, W), F32)
    return pl.pallas_call(
        functools.partial(_mlstm_scan_kernel, cps=cps),
        grid=(B, npl + npc),
        in_specs=specs(0) + specs(1),
        out_specs=[
            pl.BlockSpec((1, cps * L, W), lambda b, s: (0, blk(b, s, 0), 0)),
            pl.BlockSpec((1, cps * L, W), lambda b, s: (0, blk(b, s, 1), 0)),
        ],
        out_shape=[out_sds, out_sds],
        scratch_shapes=[
            pltpu.VMEM((2 * B_HEADS, B_DV + LANE, B_DQK), F32),
            pltpu.VMEM((2 * B_HEADS, 8, LANE), F32),
        ],
        compiler_params=_cparams(("parallel", "arbitrary")),
        name="mlstm_scan",
    )(z, nloc, stat, scal, uaug, z, nloc, stat, scal, uaug)


def _mlstm_finish_kernel(hf_ref, hb_ref, o0_ref, o1_ref, hg_ref, y_ref):
    for hd in range(B_HEADS):
        sl = slice(hd * B_DV, (hd + 1) * B_DV)
        hh = hf_ref[0, :, sl] + hb_ref[0, :, sl]
        o_ref = o0_ref if hd < 2 else o1_ref
        o = o_ref[:, (hd % 2) * B_DV:(hd % 2 + 1) * B_DV]
        y_ref[:, sl] = (_sigmoid(o) * _rms(hh) * hg_ref[:, sl]).astype(y_ref.dtype)


def _mlstm_finish_call(hf, hb, z, head_g, *, tm):
    R = z.shape[0]
    W = B_HEADS * B_DV
    return pl.pallas_call(
        _mlstm_finish_kernel,
        grid=(R // tm,),
        in_specs=[
            pl.BlockSpec((1, tm, W), lambda i: (0, i, 0)),
            pl.BlockSpec((1, tm, W), lambda i: (0, i, 0)),
            pl.BlockSpec((tm, 512), lambda i: (i, 7)),
            pl.BlockSpec((tm, 512), lambda i: (i, 8)),
            pl.BlockSpec((1, W), lambda i: (0, 0)),
        ],
        out_specs=pl.BlockSpec((tm, W), lambda i: (i, 0)),
        out_shape=jax.ShapeDtypeStruct((R, W), BF16),
        compiler_params=_cparams(("parallel",)),
        name="mlstm_finish",
    )(hf, hb, z, z, head_g)


def _s5_params(a_re, a_im, log_dt, b_re, b_im, c_re, c_im, reverse):
    hp = lax.Precision.HIGHEST
    Lc = S5_CHUNK
    dt = jnp.exp(log_dt)[:, None]
    lam_re, lam_im = a_re * dt, a_im * dt
    mag = jnp.exp(lam_re)
    ab_re, ab_im = mag * jnp.cos(lam_im), mag * jnp.sin(lam_im)
    den = a_re * a_re + a_im * a_im
    nr, ni = ab_re - 1.0, ab_im
    k_re, k_im = (nr * a_re + ni * a_im) / den, (ni * a_re - nr * a_im) / den
    bb_re = k_re[..., None] * b_re - k_im[..., None] * b_im
    bb_im = k_re[..., None] * b_im + k_im[..., None] * b_re
    tau = jnp.arange(Lc + 1, dtype=F32)[:, None, None]
    pm = jnp.exp(tau * lam_re)
    pw_re, pw_im = pm * jnp.cos(tau * lam_im), pm * jnp.sin(tau * lam_im)
    cp_re = c_re[None] * pw_re[:, :, None, :] - c_im[None] * pw_im[:, :, None, :]
    cp_im = c_re[None] * pw_im[:, :, None, :] + c_im[None] * pw_re[:, :, None, :]
    kk = jnp.einsum('tgoq,gqi->tgoi', jnp.concatenate([cp_re, -cp_im], axis=-1),
                    jnp.concatenate([bb_re, bb_im], axis=1), precision=hp)
    G = a_re.shape[0]
    i_idx = jnp.arange(Lc)[:, None]
    j_idx = jnp.arange(Lc)[None, :]
    lag = (i_idx - j_idx) if reverse else (j_idx - i_idx)
    valid = lag >= 0
    kt = jnp.where(valid[:, :, None, None, None], kk[jnp.clip(lag, 0, Lc)], 0.0)
    kt = kt.transpose(2, 0, 4, 1, 3).reshape(G, Lc * C_GROUP, Lc * C_GROUP)
    e = jnp.arange(Lc) if reverse else (Lc - 1 - jnp.arange(Lc))
    w_re = pw_re[e][:, :, :, None] * bb_re[None] - pw_im[e][:, :, :, None] * bb_im[None]
    w_im = pw_re[e][:, :, :, None] * bb_im[None] + pw_im[e][:, :, :, None] * bb_re[None]
    w_re = w_re.transpose(1, 0, 3, 2).reshape(G, Lc * C_GROUP, C_STATE)
    w_im = w_im.transpose(1, 0, 3, 2).reshape(G, Lc * C_GROUP, C_STATE)
    even = (jnp.arange(G) % 2 == 0)[:, None, None]
    zc = jnp.zeros_like(w_re)
    half_cols = lambda m: jnp.where(even, jnp.concatenate([m, zc], axis=-1), jnp.concatenate([zc, m], axis=-1))
    wre, wim = half_cols(w_re), half_cols(w_im)
    f = (Lc - jnp.arange(Lc)) if reverse else (jnp.arange(Lc) + 1)
    cj_re = cp_re[f].transpose(1, 3, 0, 2).reshape(G, C_STATE, Lc * C_GROUP)
    cj_im = -cp_im[f].transpose(1, 3, 0, 2).reshape(G, C_STATE, Lc * C_GROUP)
    zr = jnp.zeros_like(cj_re)
    half_rows = lambda m: jnp.where(even, jnp.concatenate([m, zr], axis=1), jnp.concatenate([zr, m], axis=1))
    cre, cim = half_rows(cj_re), half_rows(cj_im)
    pair = lambda v: v.reshape(G // 2, 1, 2 * C_STATE)
    are, aim = pair(pw_re[Lc]), pair(pw_im[Lc])
    wcat = jnp.concatenate([wre, wim], axis=-1)
    ccat = jnp.concatenate([cre, cim], axis=1)
    return kt.astype(BF16), wcat.astype(BF16), ccat.astype(BF16), are, aim


def _granule_transpose(xs):
    xs = list(xs)
    gran = jnp.right_shift(lax.broadcasted_iota(jnp.int32, (1, LANE), 1), C_GROUP.bit_length() - 1)
    for bit in (1, 2, 4):
        hi = jnp.bitwise_and(gran, bit) != 0
        for a in range(S5_GB):
            if a & bit:
                continue
            p = a | bit
            xa, xp = xs[a], xs[p]
            xs[a] = jnp.where(hi, pltpu.roll(xp, bit * C_GROUP, 1), xa)
            xs[p] = jnp.where(hi, xp, pltpu.roll(xa, LANE - bit * C_GROUP, 1))
    return xs


def _s5_kernel(ul_ref, uc_ref, kt_ref, wcat_ref, ccat_ref, are_ref, aim_ref, y_ref,
               ug_s, wr_s, wi_s, xr_s, xi_s, y_s, *, nb, ncc, ncl):
    Lc = S5_CHUNK
    lat0 = nb * ncc
    for src, row0, n in ((uc_ref, 0, nb * ncc), (ul_ref, lat0, nb * ncl)):
        folded = _granule_transpose([src[pl.ds(i, n, stride=Lc), :] for i in range(Lc)])
        for g in range(S5_GB):
            ug_s[g, row0:row0 + n, :] = folded[g].astype(BF16)

    npair = S5_GB // 2
    for d in range(2):
        for k in range(npair):
            u0, u1 = ug_s[2 * k], ug_s[2 * k + 1]
            w = _dot(u0, wcat_ref[d, 2 * k]) + _dot(u1, wcat_ref[d, 2 * k + 1])
            wr_s[k] = w[:, 0:LANE]
            wi_s[k] = w[:, LANE:]
        are = [jnp.broadcast_to(are_ref[d, k], (nb, LANE)) for k in range(npair)]
        aim = [jnp.broadcast_to(aim_ref[d, k], (nb, LANE)) for k in range(npair)]

        def make_body(base, n):
            def body(s, carry):
                c = s if d == 0 else n - 1 - s
                rows = pl.ds(base + c, nb, stride=n)
                new = []
                for k in range(npair):
                    xr, xi = carry[2 * k], carry[2 * k + 1]
                    xr_s[k, rows, :] = xr
                    xi_s[k, rows, :] = xi
                    new.append(are[k] * xr - aim[k] * xi + wr_s[k, rows, :])
                    new.append(are[k] * xi + aim[k] * xr + wi_s[k, rows, :])
                return tuple(new)
            return body

        state = tuple(jnp.zeros((nb, LANE), F32) for _ in range(2 * npair))
        state = lax.fori_loop(0, ncc, make_body(0, ncc), state, unroll=8)
        lax.fori_loop(0, ncl, make_body(lat0, ncl), state, unroll=8)
        for g in range(S5_GB):
            y = (_dot(ug_s[g, lat0:, :], kt_ref[d, g])
                 + _dot(jnp.concatenate([xr_s[g // 2, lat0:, :], xi_s[g // 2, lat0:, :]], axis=1).astype(BF16),
                        ccat_ref[d, g]))
            if d == 0:
                y_s[g] = y
            else:
                y_s[g] += y

    unfolded = _granule_transpose([y_s[g] for g in range(S5_GB)])
    for j in range(Lc):
        y_ref[pl.ds(j, nb * ncl, stride=Lc), :] = unfolded[j]


def _s5_call(z, kt, wcat, ccat, are, aim, *, B, T, Tc, nb):
    Lc = S5_CHUNK
    ncl, ncc = T // Lc, Tc // Lc
    nr = nb * (ncl + ncc)
    cbase = B * T // (nb * Tc)
    wspec = pl.BlockSpec((2, S5_GB, LANE, LANE), lambda i, h: (0, i, 0, 0))
    aspec = pl.BlockSpec((2, S5_GB // 2, 1, LANE), lambda i, h: (0, i, 0, 0))
    big = lambda dt: pltpu.VMEM((S5_GB, nr, LANE), dt)
    half = pltpu.VMEM((S5_GB // 2, nr, LANE), F32)
    return pl.pallas_call(
        functools.partial(_s5_kernel, nb=nb, ncc=ncc, ncl=ncl),
        grid=(C_GROUPS // S5_GB, B // nb),
        in_specs=[
            pl.BlockSpec((nb * T, LANE), lambda i, h: (h, i)),
            pl.BlockSpec((nb * Tc, LANE), lambda i, h: (cbase + h, i)),
            wspec,
            pl.BlockSpec((2, S5_GB, LANE, 2 * LANE), lambda i, h: (0, i, 0, 0)),
            pl.BlockSpec((2, S5_GB, 2 * LANE, LANE), lambda i, h: (0, i, 0, 0)),
            aspec, aspec],
        out_specs=pl.BlockSpec((nb * T, LANE), lambda i, h: (h, i)),
        out_shape=jax.ShapeDtypeStruct((B * T, C_WIDTH), F32),
        scratch_shapes=[big(BF16), half, half, half, half,
                        pltpu.VMEM((S5_GB, nb * ncl, LANE), F32)],
        compiler_params=_cparams(("parallel", "parallel")),
        name="s5_scan",
    )(z, z, kt, wcat, ccat, are, aim)


def _s5_finish_kernel(y_ref, u_ref, d_ref, w_ref, b_ref, o_ref):
    y = y_ref[...] + d_ref[...] * u_ref[...]
    g = 0.5 * y * (1.0 + jnp.tanh(math.sqrt(2.0 / math.pi) * (y + 0.044715 * (y * y * y))))
    o_ref[...] = (g * _sigmoid(_dot(g.astype(BF16), w_ref[...]) + b_ref[...])).astype(o_ref.dtype)


def _s5_finish_call(y, z, d_skip, glu_w, glu_b, *, n_rows, tm):
    W = C_WIDTH
    return pl.pallas_call(
        _s5_finish_kernel,
        grid=(n_rows // tm,),
        in_specs=[
            pl.BlockSpec((tm, W), lambda i: (i, 0)),
            pl.BlockSpec((tm, W), lambda i: (i, 0)),
            pl.BlockSpec((1, W), lambda i: (0, 0)),
            pl.BlockSpec((W, W), lambda i: (0, 0)),
            pl.BlockSpec((1, W), lambda i: (0, 0)),
        ],
        out_specs=pl.BlockSpec((tm, W), lambda i: (i, 0)),
        out_shape=jax.ShapeDtypeStruct((n_rows, W), BF16),
        compiler_params=_cparams(("parallel",)),
        name="s5_finish",
    )(y, z, d_skip, glu_w, glu_b)


def _ones_lane(par):
    return D_DH if par == 0 else 0


def _attn_d_kernel(q_ref, k_ref, v_ref, kc_ref, vc_ref, cq_ref, sq_ref, ck_ref, sk_ref, sink_ref,
                   o_ref, kpe, kpo, vpe, vpo, kce, kco, vce, vco, band, *, t, tc, rk):
    W = D_WINDOW
    kv = pl.program_id(1)
    i = pl.program_id(2)
    lane = lax.broadcasted_iota(jnp.int32, (1, LANE), 1)
    lo = lane < D_DH

    def halves(x, ones=False):
        sel = jnp.where(kv == 0, x, pltpu.roll(x, 64, 1))
        e = jnp.where(lo, sel, 0.0)
        o = pltpu.roll(e, 64, 1)
        if ones:
            e = jnp.where(lane == _ones_lane(0), 1.0, e)
            o = jnp.where(lane == _ones_lane(1), 1.0, o)
        return e.astype(BF16), o.astype(BF16)

    @pl.when(i == 0)
    def _():
        zpad = jnp.zeros((W, LANE), BF16)
        for ref in (kpe, kpo, vpe, vpo):
            ref[0:W, :] = zpad
            ref[W + t:W + t + W, :] = zpad
        kce[...], kco[...] = halves(kc_ref[...])
        vce[...], vco[...] = halves(vc_ref[...], ones=True)

        def body(c, carry):
            rows = pl.ds(pl.multiple_of(c * rk, rk), rk)
            dst = pl.ds(pl.multiple_of(W + c * rk, math.gcd(W, rk)), rk)
            kpe[dst, :], kpo[dst, :] = halves(_rope64(k_ref[rows, :], ck_ref[rows, :], sk_ref[rows, :]))
            vpe[dst, :], vpo[dst, :] = halves(v_ref[rows, :], ones=True)
            return carry

        lax.fori_loop(0, t // rk, body, 0)

    tq = q_ref.shape[0]
    nslab = q_ref.shape[1] // LANE
    span = tq + 2 * W

    @pl.when(i == 0)
    def _():
        rr = jnp.bitwise_and(lax.broadcasted_iota(jnp.int32, (nslab * tq, span), 0), tq - 1)
        cc = lax.broadcasted_iota(jnp.int32, (nslab * tq, span), 1)
        band[...] = jnp.where(jnp.abs(rr + W - cc) <= W, 0.0, -jnp.inf)

    cos, sin = cq_ref[...], sq_ref[...]
    qs = jnp.concatenate(
        [_rope64(q_ref[:, j * LANE:(j + 1) * LANE], cos, sin) for j in range(nslab)], axis=0)
    qs = (qs * (D_DH ** -0.5 * LOG2E)).astype(BF16)
    rows_k = pl.ds(pl.multiple_of(i * tq, tq), span)
    kpos = i * tq - W + lax.broadcasted_iota(jnp.int32, (1, span), 1)
    bias = band[...] + jnp.where((kpos >= 0) & (kpos < t), 0.0, -jnp.inf)
    out = None
    lane_o = lax.broadcasted_iota(jnp.int32, (1, LANE), 1)
    for par, (kp, vp, kcx, vcx) in enumerate(((kpe, vpe, kce, vce), (kpo, vpo, kco, vco))):
        s = jnp.concatenate([_dot_nt(qs, kcx[...]), _dot_nt(qs, kp[rows_k, :]) + bias], axis=1)
        snk = LOG2E * jnp.concatenate(
            [jnp.broadcast_to(sink_ref[0, 0:1, 2 * j + par:2 * j + par + 1], (tq, 1)) for j in range(nslab)],
            axis=0)
        m = jnp.maximum(jnp.max(s, axis=-1, keepdims=True), snk)
        p = jnp.exp2(s - m).astype(BF16)
        o = _dot(p[:, 0:tc], vcx[...]) + _dot(p[:, tc:], vp[rows_k, :])
        one = _ones_lane(par)
        l = o[:, one:one + 1] + jnp.exp2(snk - m)
        own = (lane_o < D_DH) if par == 0 else (lane_o >= D_DH)
        o = jnp.where(own, o, 0.0) / l
        out = o if out is None else out + o
    for j in range(nslab):
        o_ref[:, j * LANE:(j + 1) * LANE] = out[j * tq:(j + 1) * tq, :].astype(o_ref.dtype)


def _attn_d_call(z, cos, sin, sink, *, B, T, Tc):
    tq = min(2 * Q_BLOCK, T)
    nq = T // tq
    grp_w = (D_HEADS // D_KV) * D_DH
    qcol = C_WIDTH // grp_w
    kcol = (C_WIDTH + D_HEADS * D_DH) // LANE
    cbase = B * T // Tc
    rk = min(512, T)
    pad = lambda n: pltpu.VMEM((n, LANE), BF16)
    return pl.pallas_call(
        functools.partial(_attn_d_kernel, t=T, tc=Tc, rk=rk),
        grid=(B, D_KV, nq),
        in_specs=[
            pl.BlockSpec((tq, grp_w), lambda b, kv, i: (b * nq + i, qcol + kv)),
            pl.BlockSpec((T, LANE), lambda b, kv, i: (b, kcol)),
            pl.BlockSpec((T, LANE), lambda b, kv, i: (b, kcol + 1)),
            pl.BlockSpec((Tc, LANE), lambda b, kv, i: (cbase + b, kcol)),
            pl.BlockSpec((Tc, LANE), lambda b, kv, i: (cbase + b, kcol + 1)),
            pl.BlockSpec((tq, LANE), lambda b, kv, i: (i, 0)),
            pl.BlockSpec((tq, LANE), lambda b, kv, i: (i, 0)),
            pl.BlockSpec((T, LANE), lambda b, kv, i: (0, 0)),
            pl.BlockSpec((T, LANE), lambda b, kv, i: (0, 0)),
            pl.BlockSpec((1, 1, LANE), lambda b, kv, i: (kv, 0, 0)),
        ],
        out_specs=pl.BlockSpec((tq, grp_w), lambda b, kv, i: (b * nq + i, kv)),
        out_shape=jax.ShapeDtypeStruct((B * T, D_HEADS * D_DH), BF16),
        scratch_shapes=[pad(T + 2 * D_WINDOW)] * 4 + [pad(Tc)] * 4
        + [pltpu.VMEM((grp_w // LANE * tq, tq + 2 * D_WINDOW), F32)],
        compiler_params=_cparams(("parallel", "parallel", "arbitrary")),
        name="attn_d",
    )(z, z, z, z, z, cos, sin, cos, sin, sink)


def _mod_rows(ml, idx):
    sel = ml[:, jnp.array(idx), :]
    return jnp.pad(sel, ((0, 0), (0, 8 - len(idx)), (0, 0)))


def kernel(x, c, ctx, c_ctx, mod_w, mod_b, norm_g, ffn_w1, ffn_w3, ffn_w2, ab_w_in, ab_b_gate, ab_w_out, a_gq, a_gk, b_head_g, cd_w_in, cd_w_out, s5_a_re, s5_a_im, s5_log_dt, s5_b_re, s5_b_im, s5_c_re, s5_c_im, s5_d, s5_glu_w, s5_glu_b, d_sink, final_g):
    B, T, D = x.shape
    Tc = ctx.shape[1]
    depth = mod_w.shape[0]
    n_lat, n_ctx = B * T, B * Tc
    n_all = n_lat + n_ctx
    tm = math.gcd(math.gcd(T, n_ctx), 512)
    tm_mm = math.gcd(math.gcd(T, n_ctx), 1024)
    tf = math.gcd(ffn_w1.shape[-1], 512)
    seg = dict(seg_rows=T, n_seg=B)

    h, hc0 = x, ctx
    cvec = jnp.zeros((8, D), F32).at[:B].set(c).at[B].set(c_ctx)
    mods = _modvec_call(cvec, mod_w, mod_b).reshape(depth, 8, N_MOD, D)[:, :B + 1]
    cos_a, sin_a = _rope_tables(T, A_DH)
    cos_d, sin_d = _rope_tables(T, D_DH)
    w1, w3, w2 = ffn_w1.astype(BF16), ffn_w3.astype(BF16), ffn_w2.astype(BF16)
    ones = jnp.ones((1, D), F32)

    for l in range(depth):
        last = l == depth - 1
        want_ctx = not last
        ml = mods[l]
        i = l // 2
        h = _ffn_call(h, hc0 if l == 0 else None, _mod_rows(ml, (0, 1, 2)), norm_g[l, 0][None],
                      w1, w3, w2, ones, wsel=(l, 0),
                      n_rows=n_all, tm=tm, tf=tf, final_norm=False, **seg)
        mod_in = _mod_rows(ml, (3, 4))
        mod_out = _mod_rows(ml, (5,))
        if l % 2 == 0:
            w_in = jnp.pad(ab_w_in[i], ((0, 0), (0, AB_IN_PAD - ab_w_in.shape[-1]))).astype(BF16)
            z = _inproj_call(h, mod_in, norm_g[l, 1][None], w_in, tm=tm_mm, tn=1024, **seg)
            ya = _attn_a_call(z, cos_a, sin_a, a_gq[i][None], a_gk[i][None], B=B, T=T, Tc=Tc)
            yac = _attn_a_ctx_call(z, a_gq[i][None], a_gk[i][None], B=B, T=T, Tc=Tc)
            ng = 4 * B_HEADS
            gates = z[:, AB_GATE_COL:AB_GATE_COL + ng]
            grow = gates.reshape(n_all // B_CHUNK, B_CHUNK, ng).transpose(0, 2, 1)
            bcol = jnp.pad(ab_b_gate[i], (0, LANE - ng))[None]
            brow = ab_b_gate[i][:, None]
            nloc, stat, scal, uaug = _mlstm_local_call(z, grow, bcol, brow, cb=math.gcd(n_all // B_CHUNK, 8))
            hf, hb = _mlstm_scan_call(z, nloc, stat, scal, uaug, B=B, T=T, Tc=Tc)
            yb = _mlstm_finish_call(hf, hb, z, b_head_g[i][None], tm=tm)
            xa = jnp.concatenate([ya, yac], axis=0)
            w_out = ab_w_out[i].astype(BF16)
            n_rows = n_all
        else:
            w_in = cd_w_in[i].astype(BF16)
            z = _inproj_call(h, mod_in, norm_g[l, 1][None], w_in, tm=tm_mm, tn=w_in.shape[1] // 2, **seg)
            prm = [_s5_params(s5_a_re[i, d], s5_a_im[i, d], s5_log_dt[i, d], s5_b_re[i, d], s5_b_im[i, d],
                              s5_c_re[i, d], s5_c_im[i, d], reverse=(d == 1)) for d in range(2)]
            s5w = [jnp.stack([prm[0][k], prm[1][k]]) for k in range(len(prm[0]))]
            ys5 = _s5_call(z, *s5w, B=B, T=T, Tc=Tc, nb=math.gcd(B, 2))
            xa = _s5_finish_call(ys5, z, s5_d[i][None], s5_glu_w[i].astype(BF16), s5_glu_b[i][None],
                                 n_rows=n_lat, tm=tm)
            sink = jnp.pad(d_sink[i].reshape(D_KV, 1, D_HEADS // D_KV), ((0, 0), (0, 0), (0, LANE - D_HEADS // D_KV)))
            yb = _attn_d_call(z, cos_d, sin_d, sink, B=B, T=T, Tc=Tc)
            w_out = cd_w_out[i].astype(BF16)
            n_rows = n_all if want_ctx else n_lat
        if not want_ctx:
            n_rows = n_lat
        h = _outproj_call(xa, yb, w_out, h, mod_out, n_rows=n_rows, tm=tm, **seg)
        h = _ffn_call(h, None, _mod_rows(ml, (6, 7, 8)), norm_g[l, 2][None], w1, w3, w2,
                      final_g[None] if last else ones, wsel=(l, 1), n_rows=n_rows, tm=tm, tf=tf, final_norm=last,
                      out_3d=last, **seg)
    return h
```
